```python
import jax, jax.numpy as jnp
from jax import lax
import numpy as np

D_MODEL = 1024
BATCH = 2
SEQ = 8192
DEPTH = 2
DEC_BATCH = 32
DEC_SEQ = 8
PAST_LEN = 8192
PAGE_SIZE = 128

N_SB_LAYERS = (DEPTH + 1) // 2
N_RET_LAYERS = DEPTH // 2
H_A = 8
HD_A = 64
W_A = H_A * HD_A
SB_BIAS_INIT = -6.0
H_B = 4
DK_B = 128
DV_B = 128
F_B = H_B * DK_B
W_B = H_B * DV_B
H_C = 4
DK_C = 256
DV_C = 512
QK_C = H_C * DK_C
W_C = H_C * DV_C

W_EVEN_IN = 4 * W_A + 2 * F_B + 2 * W_B
W_EVEN_MIX = W_A + W_B
W_ODD_IN = 2 * QK_C + 2 * W_C
SPLITS_EVEN = (W_A, 2 * W_A, 3 * W_A, 4 * W_A, 4 * W_A + F_B, 4 * W_A + 2 * F_B, 4 * W_A + 2 * F_B + W_B)
SPLITS_ODD = (QK_C, 2 * QK_C, 2 * QK_C + W_C)

Q_BLOCK = 128
HGRN_CHUNK = 64
RET_CHUNK = 128
ROPE_BASE = 10000.0
NORM_EPS = 1e-6

kernel_name = 'stickbreak_hgrn2_retention_decoder_step'


def rms_norm(x, g=None):
    xf = x.astype(jnp.float32)
    y = xf * lax.rsqrt(jnp.mean(xf * xf, axis=-1, keepdims=True) + NORM_EPS)
    if g is not None:
        y = y * g.astype(jnp.float32)
    return y.astype(x.dtype)


def rotary(x, pos):
    half = x.shape[-1] // 2
    inv = 1.0 / (ROPE_BASE ** jnp.linspace(0.0, 1.0, half, dtype=jnp.float32))
    ang = pos[:, None] * inv[None, :]
    cos = jnp.cos(ang)[None, :, None, :]
    sin = jnp.sin(ang)[None, :, None, :]
    xf = x.astype(jnp.float32)
    x1, x2 = xf[..., :half], xf[..., half:]
    return jnp.concatenate([x1 * cos - x2 * sin, x2 * cos + x1 * sin], axis=-1)


def to_chunks(a, chunk):
    b, t, h, d = a.shape
    return a.reshape(b, t // chunk, chunk, h, d).transpose(1, 0, 3, 2, 4)


def from_chunks(a):
    n, b, h, c, d = a.shape
    return a.transpose(1, 0, 3, 2, 4).reshape(b, n * c, h, d)


def sb_attend(q, k, v, qpos, kpos, bias):
    scale = q.shape[-1] ** -0.5
    z = jnp.einsum('bqhd,bkhd->bhqk', q.astype(jnp.float32), k.astype(jnp.float32)) * scale
    z = z + bias.astype(jnp.float32)[None, :, None, None]
    mask = (kpos[None, :] < qpos[:, None])[None, None]
    log_keep = jnp.where(mask, jax.nn.log_sigmoid(-z), 0.0)
    log_between = lax.cumsum(log_keep, axis=3, reverse=True) - log_keep
    w = jnp.where(mask, jnp.exp(jax.nn.log_sigmoid(z) + log_between), 0.0)
    return jnp.einsum('bhqk,bkhd->bqhd', w, v.astype(jnp.float32))


def sb_prompt(q, k, v, bias):
    b, t, h, d = q.shape
    nb = t // Q_BLOCK
    q_blocks = q.reshape(b, nb, Q_BLOCK, h, d).transpose(1, 0, 2, 3, 4)
    kpos = jnp.arange(t)
    qpos_blocks = kpos.reshape(nb, Q_BLOCK)
    o = lax.map(lambda a: sb_attend(a[0], k, v, a[1], kpos, bias), (q_blocks, qpos_blocks))
    return o.transpose(1, 0, 2, 3, 4).reshape(b, t, h, d)


def gla_chunk_scan(q, k, v, log_f, s0, chunk):
    mask = jnp.tril(jnp.ones((chunk, chunk), dtype=bool))[None, None, :, :, None]

    def step(S, inp):
        qc, kc, vc, gc = inp
        bcum = jnp.cumsum(gc, axis=2)
        o_inter = jnp.einsum('bhtk,bhkv->bhtv', qc * jnp.exp(bcum), S)
        diff = bcum[:, :, :, None, :] - bcum[:, :, None, :, :]
        decay = jnp.exp(jnp.where(mask, diff, -jnp.inf))
        scores = jnp.einsum('bhtk,bhsk,bhtsk->bhts', qc, kc, decay)
        o_intra = jnp.einsum('bhts,bhsv->bhtv', scores, vc)
        blast = bcum[:, :, -1, :]
        S_new = jnp.exp(blast)[..., None] * S + jnp.einsum('bhsk,bhsv->bhkv', kc * jnp.exp(blast[:, :, None, :] - bcum), vc)
        return S_new, o_inter + o_intra

    S_fin, o = lax.scan(step, s0, (to_chunks(q, chunk), to_chunks(k, chunk), to_chunks(v, chunk), to_chunks(log_f, chunk)))
    return from_chunks(o), S_fin


def retention_chunk_scan(q, k, v, log_gamma, s0, chunk):
    idx = jnp.arange(chunk, dtype=jnp.float32)
    rel = idx[:, None] - idx[None, :]
    causal = rel >= 0
    intra_decay = jnp.exp(jnp.where(causal[None], rel[None] * log_gamma[:, None, None], -jnp.inf))
    q_decay = jnp.exp((idx[None, :] + 1.0) * log_gamma[:, None])[None, :, :, None]
    k_decay = jnp.exp((chunk - 1.0 - idx[None, :]) * log_gamma[:, None])[None, :, :, None]
    chunk_decay = jnp.exp(chunk * log_gamma)[None, :, None, None]

    def step(S, inp):
        qc, kc, vc = inp
        o_inter = jnp.einsum('bhtk,bhkv->bhtv', qc, S) * q_decay
        scores = jnp.einsum('bhtk,bhsk->bhts', qc, kc) * intra_decay[None]
        o_intra = jnp.einsum('bhts,bhsv->bhtv', scores, vc)
        S_new = chunk_decay * S + jnp.einsum('bhsk,bhsv->bhkv', kc * k_decay, vc)
        return S_new, o_inter + o_intra

    S_fin, o = lax.scan(step, s0, (to_chunks(q, chunk), to_chunks(k, chunk), to_chunks(v, chunk)))
    return from_chunks(o), S_fin


def even_mixer(h, w_in, w_out, qg, kg, sbb, lb, og, s0, past_k, past_v, chunk):
    f32 = jnp.float32
    b, t, _ = h.shape
    qa, ka, va, ga, qb, fb, ib, gb = jnp.split(h @ w_in, SPLITS_EVEN, axis=-1)
    qa = rms_norm(qa.reshape(b, t, H_A, HD_A), qg)
    ka = rms_norm(ka.reshape(b, t, H_A, HD_A), kg)
    va = va.reshape(b, t, H_A, HD_A)
    if past_k is None:
        oa = sb_prompt(qa, ka, va, sbb)
    else:
        p = past_k.shape[1]
        keys = jnp.concatenate([past_k, ka.astype(past_k.dtype)], axis=1)
        vals = jnp.concatenate([past_v, va.astype(past_v.dtype)], axis=1)
        oa = sb_attend(qa, keys, vals, p + jnp.arange(t), jnp.arange(p + t), sbb)
    fb32 = fb.astype(f32).reshape(b, t, H_B, DK_B)
    lb = lb.reshape(H_B, DK_B)
    log_f = jnp.log(lb + (1.0 - lb) * jax.nn.sigmoid(fb32))
    k_b = (1.0 - lb) * jax.nn.sigmoid(-fb32)
    q_b = jax.nn.silu(qb.astype(f32)).reshape(b, t, H_B, DK_B)
    v_b = ib.astype(f32).reshape(b, t, H_B, DV_B)
    ob, s_new = gla_chunk_scan(q_b, k_b, v_b, log_f, s0, chunk)
    ob = rms_norm(ob, og)
    mix = jnp.concatenate([oa.reshape(b, t, W_A) * jax.nn.silu(ga.astype(f32)),
                           ob.reshape(b, t, W_B) * jax.nn.silu(gb.astype(f32))], axis=-1)
    return mix.astype(h.dtype) @ w_out, ka, va, s_new


def odd_mixer(h, pos, w_in, w_out, s0, chunk):
    f32 = jnp.float32
    b, t, _ = h.shape
    q, k, v, g = jnp.split(h @ w_in, SPLITS_ODD, axis=-1)
    q = rotary(q.reshape(b, t, H_C, DK_C), pos)
    k = rotary(k.reshape(b, t, H_C, DK_C), pos) * (DK_C ** -0.5)
    v = v.astype(f32).reshape(b, t, H_C, DV_C)
    log_gamma = jnp.log1p(-jnp.exp2(-5.0 - jnp.arange(H_C, dtype=f32)))
    o, s_new = retention_chunk_scan(q, k, v, log_gamma, s0, chunk)
    o = rms_norm(o).reshape(b, t, W_C) * jax.nn.silu(g.astype(f32))
    return o.astype(h.dtype) @ w_out, s_new


def setup_inputs(seed: int = 0) -> dict:
    key = jax.random.key(seed)
    ks = jax.random.split(key, 17)
    f32 = jnp.float32
    n_pages = PAST_LEN // PAGE_SIZE
    n_used = DEC_BATCH * n_pages
    n_pool = (5 * n_used + 3) // 4
    nrm = jax.random.normal
    x_prompt = nrm(ks[0], (BATCH, SEQ, D_MODEL), f32)
    x_sample = nrm(ks[1], (DEC_BATCH, DEC_SEQ, D_MODEL), f32)
    cache_k = nrm(ks[2], (N_SB_LAYERS, n_pool, PAGE_SIZE, H_A, HD_A), f32)
    cache_v = nrm(ks[3], (N_SB_LAYERS, n_pool, PAGE_SIZE, H_A, HD_A), f32)
    page_table = jax.random.permutation(ks[4], n_pool)[:n_used].reshape(DEC_BATCH, n_pages).astype(jnp.int32)
    state_hgrn = 0.5 * nrm(ks[5], (N_SB_LAYERS, DEC_BATCH, H_B, DK_B, DV_B), f32)
    state_ret = 0.5 * nrm(ks[6], (N_RET_LAYERS, DEC_BATCH, H_C, DK_C, DV_C), f32)
    norm_g = 1.0 + 0.1 * nrm(ks[7], (DEPTH, D_MODEL), f32)
    w_in_even = nrm(ks[8], (N_SB_LAYERS, D_MODEL, W_EVEN_IN), f32) * D_MODEL ** -0.5
    w_out_even = nrm(ks[9], (N_SB_LAYERS, W_EVEN_MIX, D_MODEL), f32) * W_EVEN_MIX ** -0.5
    q_norm_g = 1.0 + 0.1 * nrm(ks[10], (N_SB_LAYERS, HD_A), f32)
    k_norm_g = 1.0 + 0.1 * nrm(ks[11], (N_SB_LAYERS, HD_A), f32)
    sb_logit_bias = SB_BIAS_INIT + 0.1 * nrm(ks[16], (N_SB_LAYERS, H_A), f32)
    hgrn_lower_bounds = 0.5 * nrm(ks[12], (N_SB_LAYERS + 1, F_B), f32)
    hgrn_out_norm_g = 1.0 + 0.1 * nrm(ks[13], (N_SB_LAYERS, DV_B), f32)
    w_in_odd = nrm(ks[14], (N_RET_LAYERS, D_MODEL, W_ODD_IN), f32) * D_MODEL ** -0.5
    w_out_odd = nrm(ks[15], (N_RET_LAYERS, W_C, D_MODEL), f32) * W_C ** -0.5
    return {'x_prompt': x_prompt, 'x_sample': x_sample, 'cache_k': cache_k, 'cache_v': cache_v,
            'page_table': page_table, 'state_hgrn': state_hgrn, 'state_ret': state_ret,
            'norm_g': norm_g, 'w_in_even': w_in_even, 'w_out_even': w_out_even,
            'q_norm_g': q_norm_g, 'k_norm_g': k_norm_g, 'sb_logit_bias': sb_logit_bias,
            'hgrn_lower_bounds': hgrn_lower_bounds,
            'hgrn_out_norm_g': hgrn_out_norm_g, 'w_in_odd': w_in_odd, 'w_out_odd': w_out_odd}


def reference(x_prompt, x_sample, cache_k, cache_v, page_table, state_hgrn, state_ret, norm_g,
              w_in_even, w_out_even, q_norm_g, k_norm_g, sb_logit_bias, hgrn_lower_bounds,
              hgrn_out_norm_g, w_in_odd, w_out_odd):
    f32 = jnp.float32
    bp, tp, _ = x_prompt.shape
    bs, ts, _ = x_sample.shape
    past_len = page_table.shape[1] * cache_k.shape[2]
    pos_p = jnp.arange(tp, dtype=f32)
    pos_s = past_len + jnp.arange(ts, dtype=f32)
    lower_bounds = jnp.cumsum(jax.nn.softmax(hgrn_lower_bounds.astype(f32), axis=0), axis=0)
    yp, ys = x_prompt, x_sample
    k_p, v_p, k_s, v_s, hg_p, hg_s, rt_p, rt_s = [], [], [], [], [], [], [], []
    for layer in range(DEPTH):
        e = layer // 2
        hp = rms_norm(yp, norm_g[layer])
        hs = rms_norm(ys, norm_g[layer])
        if layer % 2 == 0:
            wts = (w_in_even[e], w_out_even[e], q_norm_g[e], k_norm_g[e], sb_logit_bias[e],
                   lower_bounds[e], hgrn_out_norm_g[e])
            past_k = jnp.take(cache_k[e], page_table, axis=0).reshape(bs, past_len, H_A, HD_A)
            past_v = jnp.take(cache_v[e], page_table, axis=0).reshape(bs, past_len, H_A, HD_A)
            s0p = jnp.zeros((bp, H_B, DK_B, DV_B), f32)
            op, kn, vn, sn = even_mixer(hp, *wts, s0p, None, None, HGRN_CHUNK)
            os_, kn2, vn2, sn2 = even_mixer(hs, *wts, state_hgrn[e].astype(f32), past_k, past_v, ts)
            k_p.append(kn.astype(cache_k.dtype)); v_p.append(vn.astype(cache_v.dtype))
            k_s.append(kn2.astype(cache_k.dtype)); v_s.append(vn2.astype(cache_v.dtype))
            hg_p.append(sn.astype(state_hgrn.dtype)); hg_s.append(sn2.astype(state_hgrn.dtype))
        else:
            s0p = jnp.zeros((bp, H_C, DK_C, DV_C), f32)
            op, sn = odd_mixer(hp, pos_p, w_in_odd[e], w_out_odd[e], s0p, RET_CHUNK)
            os_, sn2 = odd_mixer(hs, pos_s, w_in_odd[e], w_out_odd[e], state_ret[e].astype(f32), ts)
            rt_p.append(sn.astype(state_ret.dtype)); rt_s.append(sn2.astype(state_ret.dtype))
        yp = yp + op
        ys = ys + os_
    return (yp, ys, jnp.stack(k_p), jnp.stack(v_p), jnp.stack(k_s), jnp.stack(v_s),
            jnp.stack(hg_p), jnp.stack(hg_s), jnp.stack(rt_p), jnp.stack(rt_s))
```

```python
import functools
import math

import numpy as np
import jax
import jax.numpy as jnp
from jax import lax
from jax.experimental import pallas as pl
from jax.experimental.pallas import tpu as pltpu

F32 = jnp.float32
BF16 = jnp.bfloat16

H_A, HD_A = 8, 64
W_A = H_A * HD_A
H_B, DK_B, DV_B = 4, 128, 128
F_B = H_B * DK_B
W_B = H_B * DV_B
H_C, DK_C, DV_C = 4, 256, 512
QK_C = H_C * DK_C
W_C = H_C * DV_C
HGRN_CHUNK = 64
RET_CHUNK = 128
ROPE_BASE = 10000.0
NORM_EPS = 1e-6

LANES = 128
VMEM_LIMIT = 56 * 1024 * 1024
SB_TQ = 256
SB_PAGES_PER_STEP = 4


def _dot(a, b):
    return jnp.dot(a, b, preferred_element_type=F32)


def _dot_nt(a, b):
    return lax.dot_general(a, b, (((1,), (1,)), ((), ())), preferred_element_type=F32)


def _dot_tn(a, b):
    return lax.dot_general(a, b, (((0,), (0,)), ((), ())), preferred_element_type=F32)


def _split2(x):
    hi = x.astype(BF16)
    lo = (x - hi.astype(F32)).astype(BF16)
    return hi, lo


def _split3(x):
    hi = x.astype(BF16)
    r = x - hi.astype(F32)
    mid = r.astype(BF16)
    lo = (r - mid.astype(F32)).astype(BF16)
    return hi, mid, lo


def _sigmoid(x):
    return 1.0 / (1.0 + jnp.exp(-x))


def _silu(x):
    return x * _sigmoid(x)


def _rms(x, eps=NORM_EPS):
    return x * lax.rsqrt(jnp.mean(x * x, axis=-1, keepdims=True) + eps)


def _params(sem):
    return pltpu.CompilerParams(dimension_semantics=sem, vmem_limit_bytes=VMEM_LIMIT)


def _inproj_even_kernel(x_ref, ng_ref, w_ref, qg_ref, kg_ref, grp_ref,
                        q_ref, k_ref, v_ref, k32_ref, v32_ref, ga_ref,
                        qb_ref, fb_ref, ib_ref, gb_ref):
    x = x_ref[...]
    h = (_rms(x) * ng_ref[...]).astype(BF16)

    def proj(c):
        return _dot(h, w_ref[:, c * W_A:(c + 1) * W_A])

    def head_norm(a, gain):
        hi, lo = _split2(a * a)
        ssum = _dot(hi, grp_ref[...]) + _dot(lo, grp_ref[...])
        return a * lax.rsqrt(ssum * (1.0 / HD_A) + NORM_EPS) * gain

    qa = head_norm(proj(0), qg_ref[...])
    q_ref[...] = (qa * (HD_A ** -0.5)).astype(q_ref.dtype)
    ka = head_norm(proj(1), kg_ref[...])
    k32_ref[...] = ka
    k_ref[...] = ka.astype(k_ref.dtype)
    va = proj(2)
    v32_ref[...] = va
    v_ref[...] = va.astype(v_ref.dtype)
    ga_ref[...] = _silu(proj(3))
    qb_ref[...] = proj(4)
    fb_ref[...] = proj(5)
    ib_ref[...] = proj(6)
    gb_ref[...] = _silu(proj(7))


def _inproj_even(x, ng, w, qg, kg, grp, act_dtype, tm):
    n, d = x.shape
    row = lambda i: (i, 0)
    const = lambda i: (0, 0)
    blk = pl.BlockSpec((tm, W_A), row)
    outs = [jax.ShapeDtypeStruct((n, W_A), act_dtype)] * 3 + [jax.ShapeDtypeStruct((n, W_A), F32)] * 7
    return pl.pallas_call(
        _inproj_even_kernel,
        grid=(n // tm,),
        in_specs=[pl.BlockSpec((tm, d), row), pl.BlockSpec((1, d), const),
                  pl.BlockSpec(w.shape, const), pl.BlockSpec((1, W_A), const),
                  pl.BlockSpec((1, W_A), const), pl.BlockSpec((W_A, W_A), const)],
        out_specs=[blk] * 10,
        out_shape=outs,
        compiler_params=_params(("parallel",)),
        name="inproj_even",
    )(x, ng, w, qg, kg, grp)


def _sb_block(qs, kblk, vblk, bias, carry, tri, mask):
    z = _dot_nt(qs, kblk) + bias
    sp = jnp.log(1.0 + jnp.exp(-jnp.abs(z)))
    lk = jnp.minimum(-z, 0.0) - sp
    if mask is not None:
        lk = jnp.where(mask, lk, 0.0)
    hi, lo = _split2(lk)
    incl = _dot(hi, tri) + _dot(lo, tri)
    w = jnp.exp(z + (carry + incl))
    if mask is not None:
        w = jnp.where(mask, w, 0.0)
    return _dot(w.astype(BF16), vblk), carry + incl[:, 0:1]


def _sb_prompt_kernel(bias_ref, q_ref, k_ref, v_ref, ga_ref, tri_ref, o_ref, acc_ref, carry_ref, *, tq):
    hp = pl.program_id(1)
    qi = pl.program_id(2)
    lane = lax.broadcasted_iota(jnp.int32, (tq, LANES), 1)
    q = q_ref[...].astype(F32)
    qs = jnp.concatenate([jnp.where(lane < HD_A, q, 0.0), jnp.where(lane >= HD_A, q, 0.0)], axis=0).astype(BF16)
    row = lax.broadcasted_iota(jnp.int32, (2 * tq, 1), 0)
    bias = jnp.where(row < tq, bias_ref[2 * hp], bias_ref[2 * hp + 1])
    tri = tri_ref[...]

    r2 = lax.broadcasted_iota(jnp.int32, (2 * tq, tq), 0)
    c2 = lax.broadcasted_iota(jnp.int32, (2 * tq, tq), 1)
    mask = c2 < jnp.where(r2 >= tq, r2 - tq, r2)
    d0 = pl.multiple_of(qi * tq, tq)
    contrib, carry = _sb_block(qs, k_ref[pl.ds(d0, tq), :], v_ref[pl.ds(d0, tq), :], bias,
                               jnp.zeros((2 * tq, 1), F32), tri, mask)
    acc_ref[...] = contrib
    carry_ref[...] = carry

    def body(i, _):
        k0 = pl.multiple_of((qi - 1 - i) * tq, tq)
        contrib, carry = _sb_block(qs, k_ref[pl.ds(k0, tq), :], v_ref[pl.ds(k0, tq), :], bias,
                                   carry_ref[...], tri, None)
        acc_ref[...] += contrib
        carry_ref[...] = carry
        return 0

    lax.fori_loop(0, qi, body, 0)
    acc = acc_ref[...]
    o = jnp.where(lane < HD_A, acc[:tq], acc[tq:]) * ga_ref[...]
    o_ref[...] = o.astype(o_ref.dtype)


def _sb_prompt(q, k, v, ga, bias, tri, batch, seq, tq):
    n = q.shape[0]
    nq = seq // tq
    qmap = lambda b, hp, qi: (b * nq + qi, hp)
    kvmap = lambda b, hp, qi: (b, hp)
    return pl.pallas_call(
        functools.partial(_sb_prompt_kernel, tq=tq),
        grid=(batch, H_A // 2, nq),
        in_specs=[pl.BlockSpec(memory_space=pltpu.SMEM),
                  pl.BlockSpec((tq, LANES), qmap),
                  pl.BlockSpec((seq, LANES), kvmap),
                  pl.BlockSpec((seq, LANES), kvmap),
                  pl.BlockSpec((tq, LANES), qmap),
                  pl.BlockSpec((tq, tq), lambda b, hp, qi: (0, 0))],
        out_specs=pl.BlockSpec((tq, LANES), qmap),
        out_shape=jax.ShapeDtypeStruct((n, W_A), BF16),
        scratch_shapes=[pltpu.VMEM((2 * tq, LANES), F32), pltpu.VMEM((2 * tq, 1), F32)],
        compiler_params=_params(("parallel", "parallel", "arbitrary")),
        name="sb_prompt",
    )(bias, q, k, v, ga, tri)


def _sb_sample_kernel(pt_ref, q_ref, kn_ref, vn_ref, ga_ref, bias_ref, tri_ref, *rest, ts, pages):
    page_refs = rest[:2 * pages]
    o_ref, acc_ref, carry_ref = rest[2 * pages:]
    s = pl.program_id(1)
    rows = H_A * ts
    ps = tri_ref.shape[0]
    lane_head = lax.broadcasted_iota(jnp.int32, (ts, W_A), 1) // HD_A
    q = q_ref[...]
    qs = jnp.concatenate([jnp.where(lane_head == h, q, 0.0) for h in range(H_A)], axis=0).astype(BF16)
    bias = bias_ref[...]
    tri = tri_ref[...]

    @pl.when(s == 0)
    def _():
        pad = jnp.zeros((ps - ts, W_A), F32)
        kn = jnp.concatenate([kn_ref[...], pad], axis=0).astype(BF16)
        vn = jnp.concatenate([vn_ref[...], pad], axis=0).astype(BF16)
        r2 = lax.broadcasted_iota(jnp.int32, (rows, ps), 0)
        c2 = lax.broadcasted_iota(jnp.int32, (rows, ps), 1)
        mask = c2 < lax.rem(r2, ts)
        contrib, carry = _sb_block(qs, kn, vn, bias, jnp.zeros((rows, 1), F32), tri, mask)
        acc_ref[...] = contrib
        carry_ref[...] = carry

    for j in range(pages):
        kp = page_refs[2 * j][...].astype(BF16)
        vp = page_refs[2 * j + 1][...].astype(BF16)
        contrib, carry = _sb_block(qs, kp, vp, bias, carry_ref[...], tri, None)
        acc_ref[...] += contrib
        carry_ref[...] = carry

    @pl.when(s == pl.num_programs(1) - 1)
    def _():
        acc = acc_ref[...]
        o = jnp.zeros((ts, W_A), F32)
        for h in range(H_A):
            o = jnp.where(lane_head == h, acc[h * ts:(h + 1) * ts], o)
        o_ref[...] = (o * ga_ref[...]).astype(o_ref.dtype)


def _sb_sample(q, kn, vn, ga, bias_rows, tri, cache_k, cache_v, page_table, ts, pages):
    n = q.shape[0]
    nb, n_pages = page_table.shape
    ps = cache_k.shape[1]
    steps = n_pages // pages
    tok = lambda b, s, pt: (b, 0)
    const = lambda b, s, pt: (0, 0)

    def page_map(j):
        return lambda b, s, pt: (pt[b, n_pages - 1 - (s * pages + j)], 0, 0)

    page_specs, page_args = [], []
    for j in range(pages):
        page_specs += [pl.BlockSpec((None, ps, W_A), page_map(j))] * 2
        page_args += [cache_k, cache_v]
    rows = H_A * ts
    grid_spec = pltpu.PrefetchScalarGridSpec(
        num_scalar_prefetch=1,
        grid=(nb, steps),
        in_specs=[pl.BlockSpec((ts, W_A), tok)] * 4
                 + [pl.BlockSpec((rows, 1), const), pl.BlockSpec((ps, ps), const)] + page_specs,
        out_specs=pl.BlockSpec((ts, W_A), tok),
        scratch_shapes=[pltpu.VMEM((rows, W_A), F32), pltpu.VMEM((rows, 1), F32)],
    )
    return pl.pallas_call(
        functools.partial(_sb_sample_kernel, ts=ts, pages=pages),
        grid_spec=grid_spec,
        out_shape=jax.ShapeDtypeStruct((n, W_A), F32),
        compiler_params=_params(("parallel", "arbitrary")),
        name="sb_sample",
    )(page_table, q, kn, vn, ga, bias_rows, tri, *page_args)


def _hgrn_tables(c, n_valid):
    levels = int(math.log2(c))
    a = np.zeros((levels + 2, c, c), np.float32)
    lm = np.zeros((levels, c, c), np.float32)
    r = np.arange(c)
    for l in range(levels):
        m = 2 ** l
        mid = (r // (2 * m)) * (2 * m) + m - 1
        second = (r // m) % 2 == 1
        for t in range(c):
            if second[t]:
                a[l, t, mid[t] + 1:t + 1] = 1.0
            else:
                a[l, t, t + 1:mid[t] + 1] = 1.0
        same = (r[:, None] // (2 * m)) == (r[None, :] // (2 * m))
        lm[l] = (same & second[:, None] & (~second)[None, :]).astype(np.float32)
    a[levels] = np.tril(np.ones((c, c), np.float32))
    a[levels + 1] = np.triu(np.ones((c, c), np.float32), 1)
    a[:, :, n_valid:] = 0.0
    return a.reshape((levels + 2) * c, c), lm


def _hgrn_kernel(q_ref, f_ref, i_ref, g_ref, s0_ref, lbp_ref, og_ref, a_ref, lm_ref,
                 o_ref, sout_ref, st_ref, *, c, n_chunks, n_valid, layer):
    t = pl.program_id(2)
    levels = lm_ref.shape[0]

    @pl.when(t == 0)
    def _():
        st_ref[...] = s0_ref[...].T

    lbp = lbp_ref[...]
    p = jnp.exp(lbp - jnp.max(lbp, axis=0, keepdims=True))
    lb = jnp.sum(p[:layer + 1], axis=0, keepdims=True) / jnp.sum(p, axis=0, keepdims=True)
    og = og_ref[...]
    a = a_ref[...]
    rowi = lax.broadcasted_iota(jnp.int32, (c, 1), 0)
    valid = rowi < n_valid

    def load(ref, ci):
        if n_valid < c:
            x = ref[...]
            return jnp.concatenate([x, jnp.zeros((c - x.shape[0], x.shape[1]), F32)], axis=0)
        return ref[pl.ds(pl.multiple_of(ci * c, c), c), :]

    def chunk(ci, _):
        xf = load(f_ref, ci)
        e = jnp.exp(-jnp.abs(xf))
        r = 1.0 / (1.0 + e)
        sig_pos = jnp.where(xf >= 0, r, e * r)
        sig_neg = jnp.where(xf >= 0, e * r, r)
        g = jnp.log(lb + (1.0 - lb) * sig_pos)
        kk = (1.0 - lb) * sig_neg
        if n_valid < c:
            g = jnp.where(valid, g, 0.0)
            kk = jnp.where(valid, kk, 0.0)
        qq = _silu(load(q_ref, ci))
        vv = load(i_ref, ci)
        vv16 = vv.astype(BF16)
        ghi, gmid, glo = _split3(g)
        ee = _dot(a, ghi) + _dot(a, gmid) + _dot(a, glo)
        bcum = ee[levels * c:(levels + 1) * c]
        rem = ee[(levels + 1) * c:]
        st = st_ref[...]
        o = _dot_nt((qq * jnp.exp(bcum)).astype(BF16), st.astype(BF16))
        scores = jnp.zeros((c, c), F32)
        for l in range(levels):
            x = jnp.exp(ee[l * c:(l + 1) * c])
            second = jnp.bitwise_and(jnp.right_shift(rowi, l), 1) == 1
            qt = jnp.where(second, qq * x, 0.0).astype(BF16)
            kt = jnp.where(second, 0.0, kk * x).astype(BF16)
            scores = scores + _dot_nt(qt, kt) * lm_ref[l]
        o = o + _dot(scores.astype(BF16), vv16)
        o = o + jnp.sum(qq * kk, axis=-1, keepdims=True) * vv
        dec_last = jnp.exp(bcum[c - 1:c, :])
        st_ref[...] = dec_last * st + _dot_tn(vv16, (kk * jnp.exp(rem)).astype(BF16))
        ob = _rms(o) * og * load(g_ref, ci)
        if n_valid < c:
            o_ref[...] = ob[:n_valid].astype(o_ref.dtype)
        else:
            o_ref[pl.ds(pl.multiple_of(ci * c, c), c), :] = ob.astype(o_ref.dtype)
        return 0

    lax.fori_loop(0, n_chunks, chunk, 0)

    @pl.when(t == pl.num_programs(2) - 1)
    def _():
        sout_ref[...] = st_ref[...].T


def _hgrn(qb, fb, ib, gb, s0, lbp, og, batch, seq, layer, out_dtype, tb):
    n = qb.shape[0]
    c = HGRN_CHUNK
    if seq >= c:
        n_valid, rows_blk, nt, n_chunks = c, tb, seq // tb, tb // c
    else:
        n_valid, rows_blk, nt, n_chunks = seq, seq, 1, 1
    a_np, lm_np = _hgrn_tables(c, n_valid)
    a = jnp.asarray(a_np, BF16)
    lm = jnp.asarray(lm_np, F32)
    tok = lambda b, h, t: (b * nt + t, h)
    st = lambda b, h, t: (b, h, 0, 0)
    blk = pl.BlockSpec((rows_blk, LANES), tok)
    return pl.pallas_call(
        functools.partial(_hgrn_kernel, c=c, n_chunks=n_chunks, n_valid=n_valid, layer=layer),
        grid=(batch, H_B, nt),
        in_specs=[blk, blk, blk, blk,
                  pl.BlockSpec((None, None, DK_B, DV_B), st),
                  pl.BlockSpec((lbp.shape[0], LANES), lambda b, h, t: (0, h)),
                  pl.BlockSpec((1, LANES), lambda b, h, t: (0, 0)),
                  pl.BlockSpec(a.shape, lambda b, h, t: (0, 0)),
                  pl.BlockSpec(lm.shape, lambda b, h, t: (0, 0, 0))],
        out_specs=[blk, pl.BlockSpec((None, None, DK_B, DV_B), st)],
        out_shape=[jax.ShapeDtypeStruct((n, W_B), out_dtype),
                   jax.ShapeDtypeStruct((batch, H_B, DK_B, DV_B), F32)],
        scratch_shapes=[pltpu.VMEM((DV_B, DK_B), F32)],
        compiler_params=_params(("parallel", "parallel", "arbitrary")),
        name="hgrn",
    )(qb, fb, ib, gb, s0, lbp, og, a, lm)


def _outproj_even_kernel(x_ref, ma_ref, mb_ref, w_ref, y_ref):
    y = x_ref[...] + _dot(ma_ref[...].astype(BF16), w_ref[:W_A, :]) + _dot(mb_ref[...].astype(BF16), w_ref[W_A:, :])
    y_ref[...] = y


def _outproj_even(x, ma, mb, w, tm):
    n, d = x.shape
    row = lambda i: (i, 0)
    return pl.pallas_call(
        _outproj_even_kernel,
        grid=(n // tm,),
        in_specs=[pl.BlockSpec((tm, d), row), pl.BlockSpec((tm, W_A), row), pl.BlockSpec((tm, W_B), row),
                  pl.BlockSpec(w.shape, lambda i: (0, 0))],
        out_specs=pl.BlockSpec((tm, d), row),
        out_shape=jax.ShapeDtypeStruct((n, d), F32),
        compiler_params=_params(("parallel",)),
        name="outproj_even",
    )(x, ma, mb, w)


def _inproj_odd_kernel(x_ref, ng_ref, w_ref, cos_ref, sin_ref, q_ref, k_ref, v_ref, g_ref):
    x = x_ref[...]
    h = (_rms(x) * ng_ref[...]).astype(BF16)
    cos = cos_ref[...]
    sin = sin_ref[...]
    half = DK_C // 2

    def rot(a, scale):
        outs = []
        for hd in range(H_C):
            x1 = a[:, hd * DK_C:hd * DK_C + half]
            x2 = a[:, hd * DK_C + half:(hd + 1) * DK_C]
            outs += [(x1 * cos - x2 * sin) * scale, (x2 * cos + x1 * sin) * scale]
        return jnp.concatenate(outs, axis=-1)

    q_ref[...] = rot(_dot(h, w_ref[:, :QK_C]), 1.0).astype(q_ref.dtype)
    k_ref[...] = rot(_dot(h, w_ref[:, QK_C:2 * QK_C]), DK_C ** -0.5)
    for j in range(2):
        lo = 2 * QK_C + j * QK_C
        v_ref[:, j * QK_C:(j + 1) * QK_C] = _dot(h, w_ref[:, lo:lo + QK_C]).astype(v_ref.dtype)
    for j in range(2):
        lo = 2 * QK_C + W_C + j * QK_C
        g_ref[:, j * QK_C:(j + 1) * QK_C] = _silu(_dot(h, w_ref[:, lo:lo + QK_C]))


def _inproj_odd(x, ng, w, cos, sin, act_dtype, tm):
    n, d = x.shape
    npos = cos.shape[0] // tm
    row = lambda i: (i, 0)
    const = lambda i: (0, 0)
    pos = lambda i: (i % npos, 0)
    return pl.pallas_call(
        _inproj_odd_kernel,
        grid=(n // tm,),
        in_specs=[pl.BlockSpec((tm, d), row), pl.BlockSpec((1, d), const), pl.BlockSpec(w.shape, const),
                  pl.BlockSpec((tm, DK_C // 2), pos), pl.BlockSpec((tm, DK_C // 2), pos)],
        out_specs=[pl.BlockSpec((tm, QK_C), row), pl.BlockSpec((tm, QK_C), row),
                   pl.BlockSpec((tm, W_C), row), pl.BlockSpec((tm, W_C), row)],
        out_shape=[jax.ShapeDtypeStruct((n, QK_C), act_dtype), jax.ShapeDtypeStruct((n, QK_C), F32),
                   jax.ShapeDtypeStruct((n, W_C), act_dtype), jax.ShapeDtypeStruct((n, W_C), F32)],
        compiler_params=_params(("parallel",)),
        name="inproj_odd",
    )(x, ng, w, cos, sin)


def _ret_kernel(q_ref, k_ref, v_ref, g_ref, s0_ref, idec_ref, qdec_ref, kdec_ref, cdec_ref,
                o_ref, sout_ref, s_ref, *, c, n_chunks, n_valid):
    t = pl.program_id(2)

    @pl.when(t == 0)
    def _():
        s_ref[...] = s0_ref[...]

    idec = idec_ref[...]
    qdec = qdec_ref[:, 0:1]
    kdec = kdec_ref[:, 0:1]
    cdec = cdec_ref[0:1, 0:1]

    def load(ref, ci):
        if n_valid < c:
            x = ref[...].astype(F32)
            return jnp.concatenate([x, jnp.zeros((c - x.shape[0], x.shape[1]), F32)], axis=0)
        return ref[pl.ds(pl.multiple_of(ci * c, c), c), :]

    def chunk(ci, _):
        qc = load(q_ref, ci).astype(BF16)
        kc = load(k_ref, ci)
        vc = load(v_ref, ci).astype(BF16)
        s = s_ref[...]
        scores = _dot_nt(qc, kc.astype(BF16)) * idec
        o = _dot(scores.astype(BF16), vc) + _dot(qc, s.astype(BF16)) * qdec
        s_ref[...] = cdec * s + _dot_tn((kc * kdec).astype(BF16), vc)
        ob = _rms(o) * load(g_ref, ci)
        if n_valid < c:
            o_ref[...] = ob[:n_valid].astype(o_ref.dtype)
        else:
            o_ref[pl.ds(pl.multiple_of(ci * c, c), c), :] = ob.astype(o_ref.dtype)
        return 0

    lax.fori_loop(0, n_chunks, chunk, 0)

    @pl.when(t == pl.num_programs(2) - 1)
    def _():
        sout_ref[...] = s_ref[...]


def _ret_tables(c, chunk):
    f32 = jnp.float32
    log_gamma = jnp.log1p(-jnp.exp2(-5.0 - jnp.arange(H_C, dtype=f32)))
    idx = jnp.arange(c, dtype=f32)
    real = idx < chunk
    rel = idx[:, None] - idx[None, :]
    ok = (rel >= 0) & real[:, None] & real[None, :]
    idec = jnp.exp(jnp.where(ok[None], rel[None] * log_gamma[:, None, None], -jnp.inf))
    qdec = jnp.where(real[None, :], jnp.exp((idx[None, :] + 1.0) * log_gamma[:, None]), 0.0)
    kdec = jnp.where(real[None, :], jnp.exp((chunk - 1.0 - idx[None, :]) * log_gamma[:, None]), 0.0)
    cdec = jnp.exp(chunk * log_gamma)
    bc = lambda x: jnp.broadcast_to(x[:, :, None], (H_C, c, LANES))
    return idec, bc(qdec), bc(kdec), jnp.broadcast_to(cdec[:, None, None], (H_C, 8, LANES))


def _ret(q, k, v, g, s0, batch, seq, out_dtype, tb):
    n = q.shape[0]
    c = RET_CHUNK
    if seq >= c:
        n_valid, rows_blk, nt, n_chunks = c, tb, seq // tb, tb // c
    else:
        n_valid, rows_blk, nt, n_chunks = seq, seq, 1, 1
    idec, qdec, kdec, cdec = _ret_tables(c, n_valid)
    tok = lambda b, h, t: (b * nt + t, h)
    st = lambda b, h, t: (b, h, 0, 0)
    hd = lambda b, h, t: (h, 0, 0)
    return pl.pallas_call(
        functools.partial(_ret_kernel, c=c, n_chunks=n_chunks, n_valid=n_valid),
        grid=(batch, H_C, nt),
        in_specs=[pl.BlockSpec((rows_blk, DK_C), tok), pl.BlockSpec((rows_blk, DK_C), tok),
                  pl.BlockSpec((rows_blk, DV_C), tok), pl.BlockSpec((rows_blk, DV_C), tok),
                  pl.BlockSpec((None, None, DK_C, DV_C), st),
                  pl.BlockSpec((None, c, c), hd), pl.BlockSpec((None, c, LANES), hd),
                  pl.BlockSpec((None, c, LANES), hd), pl.BlockSpec((None, 8, LANES), hd)],
        out_specs=[pl.BlockSpec((rows_blk, DV_C), tok), pl.BlockSpec((None, None, DK_C, DV_C), st)],
        out_shape=[jax.ShapeDtypeStruct((n, W_C), out_dtype),
                   jax.ShapeDtypeStruct((batch, H_C, DK_C, DV_C), F32)],
        scratch_shapes=[pltpu.VMEM((DK_C, DV_C), F32)],
        compiler_params=_params(("parallel", "parallel", "arbitrary")),
        name="retention",
    )(q, k, v, g, s0, idec, qdec, kdec, cdec)


def _outproj_odd_kernel(x_ref, m_ref, w_ref, y_ref):
    y_ref[...] = x_ref[...] + _dot(m_ref[...].astype(BF16), w_ref[...])


def _outproj_odd(x, m, w, tm):
    n, d = x.shape
    row = lambda i: (i, 0)
    return pl.pallas_call(
        _outproj_odd_kernel,
        grid=(n // tm,),
        in_specs=[pl.BlockSpec((tm, d), row), pl.BlockSpec((tm, W_C), row), pl.BlockSpec(w.shape, lambda i: (0, 0))],
        out_specs=pl.BlockSpec((tm, d), row),
        out_shape=jax.ShapeDtypeStruct((n, d), F32),
        compiler_params=_params(("parallel",)),
        name="outproj_odd",
    )(x, m, w)


def _rope_tables(pos):
    half = DK_C // 2
    inv = 1.0 / (ROPE_BASE ** jnp.linspace(0.0, 1.0, half, dtype=F32))
    ang = pos[:, None] * inv[None, :]
    return jnp.cos(ang), jnp.sin(ang)


def _tri(n):
    return jnp.asarray(np.tril(np.ones((n, n), np.float32)), BF16)


def kernel(x_prompt, x_sample, cache_k, cache_v, page_table, state_hgrn, state_ret, norm_g,
           w_in_even, w_out_even, q_norm_g, k_norm_g, sb_logit_bias, hgrn_lower_bounds,
           hgrn_out_norm_g, w_in_odd, w_out_odd):
    bp, tp, d = x_prompt.shape
    bs, ts, _ = x_sample.shape
    n_pool, page_size = cache_k.shape[1], cache_k.shape[2]
    past_len = page_table.shape[1] * page_size
    depth = norm_g.shape[0]
    tm_p = 256
    tm_s = bs * ts

    yp = x_prompt.reshape(bp * tp, d)
    ys = x_sample.reshape(bs * ts, d)
    grp = jnp.asarray(np.kron(np.eye(H_A, dtype=np.float32), np.ones((HD_A, HD_A), np.float32)), BF16)
    cos_p, sin_p = _rope_tables(jnp.arange(tp, dtype=F32))
    cos_s, sin_s = _rope_tables(past_len + jnp.arange(ts, dtype=F32))
    cos_s, sin_s = jnp.tile(cos_s, (bs, 1)), jnp.tile(sin_s, (bs, 1))

    k_p, v_p, k_s, v_s, hg_p, hg_s, rt_p, rt_s = [], [], [], [], [], [], [], []
    for layer in range(depth):
        e = layer // 2
        ng = norm_g[layer].reshape(1, d)
        if layer % 2 == 0:
            w_in = w_in_even[e].astype(BF16)
            w_out = w_out_even[e].astype(BF16)
            qg = jnp.tile(q_norm_g[e], H_A).reshape(1, W_A)
            kg = jnp.tile(k_norm_g[e], H_A).reshape(1, W_A)
            og = hgrn_out_norm_g[e].reshape(1, DV_B)
            bias = sb_logit_bias[e].astype(F32)
            q, k, v, k32, v32, ga, qb, fb, ib, gb = _inproj_even(yp, ng, w_in, qg, kg, grp, BF16, tm_p)
            ma = _sb_prompt(q, k, v, ga, bias, _tri(SB_TQ), bp, tp, SB_TQ)
            mb, hs = _hgrn(qb, fb, ib, gb, jnp.zeros((bp, H_B, DK_B, DV_B), F32), hgrn_lower_bounds,
                           og, bp, tp, e, BF16, 512)
            yp = _outproj_even(yp, ma, mb, w_out, tm_p)
            k_p.append(k32.reshape(bp, tp, H_A, HD_A))
            v_p.append(v32.reshape(bp, tp, H_A, HD_A))
            hg_p.append(hs)
            q, k, v, k32, v32, ga, qb, fb, ib, gb = _inproj_even(ys, ng, w_in, qg, kg, grp, F32, tm_s)
            ck = cache_k[e].reshape(n_pool, page_size, W_A)
            cv = cache_v[e].reshape(n_pool, page_size, W_A)
            ma = _sb_sample(q, k, v, ga, jnp.repeat(bias, ts).reshape(H_A * ts, 1), _tri(page_size),
                            ck, cv, page_table, ts, SB_PAGES_PER_STEP)
            mb, hs = _hgrn(qb, fb, ib, gb, state_hgrn[e].astype(F32), hgrn_lower_bounds, og, bs, ts, e, F32, ts)
            ys = _outproj_even(ys, ma, mb, w_out, tm_s)
            k_s.append(k32.reshape(bs, ts, H_A, HD_A))
            v_s.append(v32.reshape(bs, ts, H_A, HD_A))
            hg_s.append(hs)
        else:
            w_in = w_in_odd[e].astype(BF16)
            w_out = w_out_odd[e].astype(BF16)
            q, k, v, g = _inproj_odd(yp, ng, w_in, cos_p, sin_p, BF16, tm_p)
            m, s = _ret(q, k, v, g, jnp.zeros((bp, H_C, DK_C, DV_C), F32), bp, tp, BF16, 512)
            yp = _outproj_odd(yp, m, w_out, tm_p)
            rt_p.append(s)
            q, k, v, g = _inproj_odd(ys, ng, w_in, cos_s, sin_s, F32, tm_s)
            m, s = _ret(q, k, v, g, state_ret[e].astype(F32), bs, ts, F32, ts)
            ys = _outproj_odd(ys, m, w_out, tm_s)
            rt_s.append(s)
    return (yp.reshape(bp, tp, d), ys.reshape(bs, ts, d), jnp.stack(k_p), jnp.stack(v_p), jnp.stack(k_s),
            jnp.stack(v_s), jnp.stack(hg_p), jnp.stack(hg_s), jnp.stack(rt_p), jnp.stack(rt_s))
```

```python
import functools
import math

import numpy as np
import jax
import jax.numpy as jnp
from jax import lax
from jax.experimental import pallas as pl
from jax.experimental.pallas import tpu as pltpu

F32 = jnp.float32
BF16 = jnp.bfloat16

H_A, HD_A = 8, 64
W_A = H_A * HD_A
H_B, DK_B, DV_B = 4, 128, 128
F_B = H_B * DK_B
W_B = H_B * DV_B
H_C, DK_C, DV_C = 4, 256, 512
QK_C = H_C * DK_C
W_C = H_C * DV_C
HGRN_CHUNK = 64
RET_CHUNK = 128
ROPE_BASE = 10000.0
NORM_EPS = 1e-6

LANES = 128
VMEM_LIMIT = 56 * 1024 * 1024
SB_TQ = 256
RET_NB, RET_HPG = 2, 2
HGRN_NB = 2
SB_RING = 3
SB_PAGES_PER_STEP = 8
SB_SPLIT = False
LOG2E = 1.4426950408889634


def _dot(a, b):
    return jnp.dot(a, b, preferred_element_type=F32)


def _dot_nt(a, b):
    return lax.dot_general(a, b, (((1,), (1,)), ((), ())), preferred_element_type=F32)


def _dot_tn(a, b):
    return lax.dot_general(a, b, (((0,), (0,)), ((), ())), preferred_element_type=F32)


def _split2(x):
    hi = x.astype(BF16)
    lo = (x - hi.astype(F32)).astype(BF16)
    return hi, lo


def _split3(x):
    hi = x.astype(BF16)
    r = x - hi.astype(F32)
    mid = r.astype(BF16)
    lo = (r - mid.astype(F32)).astype(BF16)
    return hi, mid, lo


def _sigmoid(x):
    return 1.0 / (1.0 + jnp.exp(-x))


def _silu(x):
    return x * _sigmoid(x)


def _rms(x, eps=NORM_EPS):
    return x * lax.rsqrt(jnp.mean(x * x, axis=-1, keepdims=True) + eps)


def _params(sem):
    return pltpu.CompilerParams(dimension_semantics=sem, vmem_limit_bytes=VMEM_LIMIT)


def _inproj_even_kernel(x_ref, ng_ref, w_ref, qg_ref, kg_ref, grp_ref,
                        q_ref, k_ref, v_ref, k32_ref, v32_ref, ga_ref,
                        qb_ref, fb_ref, ib_ref, gb_ref):
    x = x_ref[...]
    h = (_rms(x) * ng_ref[...]).astype(BF16)

    def proj(c):
        return _dot(h, w_ref[:, c * W_A:(c + 1) * W_A])

    def head_norm(a, gain):
        hi, lo = _split2(a * a)
        ssum = _dot(hi, grp_ref[...]) + _dot(lo, grp_ref[...])
        return a * lax.rsqrt(ssum * (1.0 / HD_A) + NORM_EPS) * gain

    qa = head_norm(proj(0), qg_ref[...])
    q_ref[...] = (qa * (HD_A ** -0.5 * LOG2E)).astype(q_ref.dtype)
    ka = head_norm(proj(1), kg_ref[...])
    k32_ref[...] = ka
    k_ref[...] = ka.astype(k_ref.dtype)
    va = proj(2)
    v32_ref[...] = va
    v_ref[...] = va.astype(v_ref.dtype)
    ga_ref[...] = _silu(proj(3))
    qb_ref[...] = proj(4)
    fb_ref[...] = proj(5)
    ib_ref[...] = proj(6)
    gb_ref[...] = _silu(proj(7))


def _inproj_even(x, ng, w, qg, kg, grp, act_dtype, tm):
    n, d = x.shape
    row = lambda i: (i, 0)
    const = lambda i: (0, 0)
    blk = pl.BlockSpec((tm, W_A), row)
    outs = [jax.ShapeDtypeStruct((n, W_A), act_dtype)] * 3 + [jax.ShapeDtypeStruct((n, W_A), F32)] * 7
    return pl.pallas_call(
        _inproj_even_kernel,
        grid=(n // tm,),
        in_specs=[pl.BlockSpec((tm, d), row), pl.BlockSpec((1, d), const),
                  pl.BlockSpec(w.shape, const), pl.BlockSpec((1, W_A), const),
                  pl.BlockSpec((1, W_A), const), pl.BlockSpec((W_A, W_A), const)],
        out_specs=[blk] * 10,
        out_shape=outs,
        compiler_params=_params(("parallel",)),
        name="inproj_even",
    )(x, ng, w, qg, kg, grp)


def _sb_block(z, pv, carry, ntri, mask):
    incl = _sb_incl(_sb_nl(z, mask), ntri)
    w = jnp.exp2(z + (carry + incl))
    if mask is not None:
        w = jnp.where(mask, w, 0.0)
    return pv(w), carry + incl[:, 0:1]


def _sb_nl(z, mask):
    neg_abs = lax.bitcast_convert_type(lax.bitcast_convert_type(z, jnp.uint32) | jnp.uint32(0x80000000), F32)
    nl = jnp.maximum(z, 0.0) + jnp.log(1.0 + jnp.exp2(neg_abs)) * LOG2E
    if mask is not None:
        nl = jnp.where(mask, nl, 0.0)
    return nl


def _sb_incl(nl, ntri):
    if SB_SPLIT:
        hi, lo = _split2(nl)
        return _dot(hi, ntri) + _dot(lo, ntri)
    return _dot(nl.astype(BF16), ntri)


def _sb_prompt_kernel(bias_ref, q_ref, k_ref, v_ref, ga_ref, ntri_ref, o_ref, qs_ref, acc_ref, carry_ref,
                      z_ref, zi_ref, col_ref, *, tq, nq):
    hp = pl.program_id(1)
    rows = 2 * tq
    lane = lax.broadcasted_iota(jnp.int32, (tq, LANES), 1)
    b0 = bias_ref[2 * hp] * LOG2E
    b1 = bias_ref[2 * hp + 1] * LOG2E

    def blk(i):
        return pl.ds(pl.multiple_of(i * tq, tq), tq)

    def logits(qs, kb):
        zr = _dot_nt(qs, k_ref[blk(kb), :])
        return jnp.concatenate([zr[:tq] + b0, zr[tq:] + b1], axis=0)

    r2 = lax.broadcasted_iota(jnp.int32, (rows, tq), 0)
    c2 = lax.broadcasted_iota(jnp.int32, (rows, tq), 1)
    mask = c2 < jnp.where(r2 >= tq, r2 - tq, r2)

    def diag(qi, _):
        q = q_ref[blk(qi), :].astype(F32)
        qs = jnp.concatenate([jnp.where(lane < HD_A, q, 0.0), jnp.where(lane >= HD_A, q, 0.0)], axis=0).astype(BF16)
        qs_ref[qi] = qs
        contrib, carry = _sb_block(logits(qs, qi), lambda w: _dot(w.astype(BF16), v_ref[blk(qi), :]),
                                   jnp.zeros((rows, 1), F32), ntri_ref[...], mask)
        acc_ref[qi] = contrib
        carry_ref[qi] = carry
        return 0

    lax.fori_loop(0, nq, diag, 0)

    def stage_a(p, slot):
        z_ref[slot] = logits(qs_ref[p[0]], p[1])

    def stage_b(slot):
        z = z_ref[slot]
        incl = _sb_incl(_sb_nl(z, None), ntri_ref[...])
        zi_ref[slot] = z + incl
        col_ref[slot] = incl[:, 0:1]

    def stage_c(p, slot):
        carry = carry_ref[p[0]]
        w = jnp.exp2(zi_ref[slot] + carry)
        acc_ref[p[0]] += _dot(w.astype(BF16), v_ref[blk(p[1]), :])
        carry_ref[p[0]] = carry + col_ref[slot]

    def nxt(p):
        wrap = p[1] == 0
        return jnp.where(wrap, p[0] + 1, p[0]), jnp.where(wrap, p[0], p[1] - 1)

    def step(t, pc, pa):
        stage_c(pc, t % SB_RING)
        stage_b((t + 1) % SB_RING)
        stage_a(pa, (t + 2) % SB_RING)
        return nxt(pc), nxt(pa)

    n_pairs = nq * (nq - 1) // 2
    p0 = (jnp.int32(1), jnp.int32(0))
    if n_pairs == 1:
        stage_a(p0, 0)
        stage_b(0)
        stage_c(p0, 0)
    elif n_pairs >= 2:
        p1 = nxt(p0)
        stage_a(p0, 0)
        stage_b(0)
        stage_a(p1, 1)
        n_steady = n_pairs - 2
        n_trips = n_steady // SB_RING

        def body(_, st):
            pc, pa = st[:2], st[2:]
            for u in range(SB_RING):
                pc, pa = step(u, pc, pa)
            return (*pc, *pa)

        st = lax.fori_loop(0, n_trips, body, (*p0, *nxt(p1)))
        pc, pa = st[:2], st[2:]
        for t in range(n_trips * SB_RING, n_steady):
            pc, pa = step(t, pc, pa)
        stage_c(pc, n_steady % SB_RING)
        stage_b((n_steady + 1) % SB_RING)
        stage_c(nxt(pc), (n_steady + 1) % SB_RING)

    def finish(qi, _):
        acc = acc_ref[qi]
        o = jnp.where(lane < HD_A, acc[:tq], acc[tq:]) * ga_ref[blk(qi), :]
        o_ref[blk(qi), :] = o.astype(o_ref.dtype)
        return 0

    lax.fori_loop(0, nq, finish, 0)


def _sb_prompt(q, k, v, ga, bias, ntri, batch, seq, tq):
    n = q.shape[0]
    nq = seq // tq
    seqmap = lambda b, hp: (b, hp)
    blk = pl.BlockSpec((seq, LANES), seqmap)
    return pl.pallas_call(
        functools.partial(_sb_prompt_kernel, tq=tq, nq=nq),
        grid=(batch, H_A // 2),
        in_specs=[pl.BlockSpec(memory_space=pltpu.SMEM), blk, blk, blk, blk,
                  pl.BlockSpec((tq, tq), lambda b, hp: (0, 0))],
        out_specs=blk,
        out_shape=jax.ShapeDtypeStruct((n, W_A), BF16),
        scratch_shapes=[pltpu.VMEM((nq, 2 * tq, LANES), BF16), pltpu.VMEM((nq, 2 * tq, LANES), F32),
                        pltpu.VMEM((nq, 2 * tq, 1), F32),
                        pltpu.VMEM((SB_RING, 2 * tq, tq), F32), pltpu.VMEM((SB_RING, 2 * tq, tq), F32),
                        pltpu.VMEM((SB_RING, 2 * tq, 1), F32)],
        compiler_params=_params(("parallel", "parallel")),
        name="sb_prompt",
    )(bias, q, k, v, ga, ntri)


def _sb_sample_kernel(pt_ref, q_ref, kn_ref, vn_ref, ga_ref, bias_ref, ntri_ref, *rest, ts, pages):
    page_refs = rest[:2 * pages]
    o_ref, acc_ref, carry_ref = rest[2 * pages:]
    s = pl.program_id(1)
    rows = H_A * ts
    ps = ntri_ref.shape[0]
    q = q_ref[...]
    qpair = [jnp.concatenate([q[:, (2 * p) * HD_A:(2 * p + 1) * HD_A], q[:, (2 * p + 1) * HD_A:(2 * p + 2) * HD_A]],
                             axis=0).astype(BF16) for p in range(H_A // 2)]
    bias = bias_ref[...]
    ntri = ntri_ref[...]

    def own_rows(x, h):
        return x[(h % 2) * ts:(h % 2 + 1) * ts]

    def sweep(n, get_k, get_v, carry, mask):
        z = jnp.concatenate(
            [jnp.concatenate([own_rows(_dot_nt(qpair[h // 2], get_k(j, h)), h) for h in range(H_A)], axis=0)
             for j in range(n)], axis=1) + bias
        nl = _sb_nl(z, mask)
        contrib = None
        for j in range(n):
            cols = slice(j * ps, (j + 1) * ps)
            incl = _sb_incl(nl[:, cols], ntri)
            w = jnp.exp2(z[:, cols] + (carry + incl))
            if mask is not None:
                w = jnp.where(mask[:, cols], w, 0.0)
            w = w.astype(BF16)
            c = jnp.concatenate([own_rows(_dot(w[(h // 2) * 2 * ts:(h // 2 + 1) * 2 * ts], get_v(j, h)), h)
                                 for h in range(H_A)], axis=0)
            contrib = c if contrib is None else contrib + c
            carry = carry + incl[:, 0:1]
        return contrib, carry

    @pl.when(s == 0)
    def _():
        pad = jnp.zeros((ps - ts, HD_A), F32)
        kn = kn_ref[...]
        vn = vn_ref[...]
        new_k = lambda j, h: jnp.concatenate([kn[:, h * HD_A:(h + 1) * HD_A], pad], axis=0).astype(BF16)
        new_v = lambda j, h: jnp.concatenate([vn[:, h * HD_A:(h + 1) * HD_A], pad], axis=0).astype(BF16)
        r2 = lax.broadcasted_iota(jnp.int32, (rows, ps), 0)
        c2 = lax.broadcasted_iota(jnp.int32, (rows, ps), 1)
        mask = c2 < lax.rem(r2, ts)
        contrib, carry = sweep(1, new_k, new_v, jnp.zeros((rows, 1), F32), mask)
        acc_ref[...] = contrib
        carry_ref[...] = carry

    contrib, carry = sweep(pages,
                           lambda j, h: page_refs[2 * j][pl.ds(h, ps, stride=H_A), :].astype(BF16),
                           lambda j, h: page_refs[2 * j + 1][pl.ds(h, ps, stride=H_A), :].astype(BF16),
                           carry_ref[...], None)
    acc_ref[...] += contrib
    carry_ref[...] = carry

    @pl.when(s == pl.num_programs(1) - 1)
    def _():
        acc = acc_ref[...]
        o = jnp.concatenate([acc[h * ts:(h + 1) * ts] for h in range(H_A)], axis=-1)
        o_ref[...] = (o * ga_ref[...]).astype(o_ref.dtype)


def _sb_sample(q, kn, vn, ga, bias_rows, ntri, cache_k, cache_v, page_table, ts, pages):
    n = q.shape[0]
    nb, n_pages = page_table.shape
    ps = cache_k.shape[1] // H_A
    steps = n_pages // pages
    tok = lambda b, s, pt: (b, 0)
    const = lambda b, s, pt: (0, 0)

    def page_map(j):
        return lambda b, s, pt: (pt[b, n_pages - 1 - (s * pages + j)], 0, 0)

    page_specs, page_args = [], []
    for j in range(pages):
        page_specs += [pl.BlockSpec((None, ps * H_A, HD_A), page_map(j))] * 2
        page_args += [cache_k, cache_v]
    rows = H_A * ts
    grid_spec = pltpu.PrefetchScalarGridSpec(
        num_scalar_prefetch=1,
        grid=(nb, steps),
        in_specs=[pl.BlockSpec((ts, W_A), tok)] * 4
                 + [pl.BlockSpec((rows, 1), const), pl.BlockSpec((ps, ps), const)] + page_specs,
        out_specs=pl.BlockSpec((ts, W_A), tok),
        scratch_shapes=[pltpu.VMEM((rows, HD_A), F32), pltpu.VMEM((rows, 1), F32)],
    )
    return pl.pallas_call(
        functools.partial(_sb_sample_kernel, ts=ts, pages=pages),
        grid_spec=grid_spec,
        out_shape=jax.ShapeDtypeStruct((n, W_A), F32),
        compiler_params=_params(("parallel", "arbitrary")),
        name="sb_sample",
    )(page_table, q, kn, vn, ga, bias_rows, ntri, *page_args)


def _hgrn_tables(c, n_valid):
    levels = int(math.log2(c))
    a = np.zeros((levels + 2, c, c), np.float32)
    lm = np.zeros((levels, c, c), np.float32)
    r = np.arange(c)
    for l in range(levels):
        m = 2 ** l
        mid = (r // (2 * m)) * (2 * m) + m - 1
        second = (r // m) % 2 == 1
        for t in range(c):
            if second[t]:
                a[l, t, mid[t] + 1:t + 1] = 1.0
            else:
                a[l, t, t + 1:mid[t] + 1] = 1.0
        same = (r[:, None] // (2 * m)) == (r[None, :] // (2 * m))
        lm[l] = (same & second[:, None] & (~second)[None, :]).astype(np.float32)
    a[levels] = np.tril(np.ones((c, c), np.float32))
    a[levels + 1] = np.triu(np.ones((c, c), np.float32), 1)
    a[:, :, n_valid:] = 0.0
    return a.reshape((levels + 2) * c, c), lm


def _hgrn_kernel(q_ref, f_ref, i_ref, g_ref, s0_ref, lbp_ref, og_ref, a_ref, lm_ref,
                 o_ref, sout_ref, st_ref, *, c, n_chunks, n_valid, layer, nb):
    t = pl.program_id(1)
    levels = lm_ref.shape[0]
    chains = [(bi, h) for bi in range(nb) for h in range(H_B)]

    @pl.when(t == 0)
    def _():
        for bi, h in chains:
            st_ref[bi, h] = s0_ref[bi, h].T

    lbp = lbp_ref[...]
    p = jnp.exp(lbp - jnp.max(lbp, axis=0, keepdims=True))
    lb = jnp.sum(p[:layer + 1], axis=0, keepdims=True) / jnp.sum(p, axis=0, keepdims=True)
    og = og_ref[...]
    rowi = lax.broadcasted_iota(jnp.int32, (c, 1), 0)
    valid = rowi < n_valid

    def load(ref, bi, ci):
        if n_valid < c:
            x = ref[bi]
            return jnp.concatenate([x, jnp.zeros((c - x.shape[0], x.shape[1]), F32)], axis=0)
        return ref[bi, pl.ds(pl.multiple_of(ci * c, c), c), :]

    def chunk(ci, _):
        gs, kks, qqs, vvs = [], [], [], []
        for bi in range(nb):
            xf = load(f_ref, bi, ci)
            e = jnp.exp(-jnp.abs(xf))
            r = 1.0 / (1.0 + e)
            sig_pos = jnp.where(xf >= 0, r, e * r)
            sig_neg = jnp.where(xf >= 0, e * r, r)
            g = jnp.log(lb + (1.0 - lb) * sig_pos)
            kk = (1.0 - lb) * sig_neg
            if n_valid < c:
                g = jnp.where(valid, g, 0.0)
                kk = jnp.where(valid, kk, 0.0)
            gs.append(g)
            kks.append(kk)
            qqs.append(_silu(load(q_ref, bi, ci)))
            vvs.append(load(i_ref, bi, ci))
        ghi, gmid, glo = _split3(jnp.concatenate(gs, axis=1))
        a = a_ref[...]
        ee = _dot(a, ghi) + _dot(a, gmid) + _dot(a, glo)
        outs = [[] for _ in range(nb)]
        for bi, h in chains:
            hl = slice(h * LANES, (h + 1) * LANES)
            el = slice((bi * H_B + h) * LANES, (bi * H_B + h + 1) * LANES)
            qq, kk, vv = qqs[bi][:, hl], kks[bi][:, hl], vvs[bi][:, hl]
            vv16 = vv.astype(BF16)
            bcum = ee[levels * c:(levels + 1) * c, el]
            rem = ee[(levels + 1) * c:, el]
            st = st_ref[bi, h]
            o = _dot_nt((qq * jnp.exp(bcum)).astype(BF16), st.astype(BF16))
            scores = jnp.zeros((c, c), F32)
            for l in range(levels):
                x = jnp.exp(ee[l * c:(l + 1) * c, el])
                second = jnp.bitwise_and(jnp.right_shift(rowi, l), 1) == 1
                qt = jnp.where(second, qq * x, 0.0).astype(BF16)
                kt = jnp.where(second, 0.0, kk * x).astype(BF16)
                scores = scores + _dot_nt(qt, kt) * lm_ref[l]
            o = o + _dot(scores.astype(BF16), vv16)
            o = o + jnp.sum(qq * kk, axis=-1, keepdims=True) * vv
            dec_last = jnp.exp(bcum[c - 1:c, :])
            st_ref[bi, h] = dec_last * st + _dot_tn(vv16, (kk * jnp.exp(rem)).astype(BF16))
            outs[bi].append(_rms(o) * og)
        for bi in range(nb):
            ob = jnp.concatenate(outs[bi], axis=1) * load(g_ref, bi, ci)
            if n_valid < c:
                o_ref[bi] = ob[:n_valid].astype(o_ref.dtype)
            else:
                o_ref[bi, pl.ds(pl.multiple_of(ci * c, c), c), :] = ob.astype(o_ref.dtype)
        return 0

    lax.fori_loop(0, n_chunks, chunk, 0)

    @pl.when(t == pl.num_programs(1) - 1)
    def _():
        for bi, h in chains:
            sout_ref[bi, h] = st_ref[bi, h].T


def _hgrn(qb, fb, ib, gb, s0, lbp, og, batch, seq, layer, out_dtype, tb, nb):
    n = qb.shape[0]
    c = HGRN_CHUNK
    if seq >= c:
        n_valid, rows_blk, nt, n_chunks = c, tb, seq // tb, tb // c
    else:
        n_valid, rows_blk, nt, n_chunks = seq, seq, 1, 1
    a_np, lm_np = _hgrn_tables(c, n_valid)
    a = jnp.asarray(a_np, BF16)
    lm = jnp.asarray(lm_np, F32)
    tok = lambda b, t: (b, t, 0)
    st = lambda b, t: (b, 0, 0, 0)
    blk = pl.BlockSpec((nb, rows_blk, W_B), tok)
    st_blk = pl.BlockSpec((nb, H_B, DK_B, DV_B), st)
    r3 = lambda x: x.reshape(batch, seq, W_B)
    mix, s_out = pl.pallas_call(
        functools.partial(_hgrn_kernel, c=c, n_chunks=n_chunks, n_valid=n_valid, layer=layer, nb=nb),
        grid=(batch // nb, nt),
        in_specs=[blk, blk, blk, blk, st_blk,
                  pl.BlockSpec(lbp.shape, lambda b, t: (0, 0)),
                  pl.BlockSpec((1, LANES), lambda b, t: (0, 0)),
                  pl.BlockSpec(a.shape, lambda b, t: (0, 0)),
                  pl.BlockSpec(lm.shape, lambda b, t: (0, 0, 0))],
        out_specs=[blk, st_blk],
        out_shape=[jax.ShapeDtypeStruct((batch, seq, W_B), out_dtype),
                   jax.ShapeDtypeStruct((batch, H_B, DK_B, DV_B), F32)],
        scratch_shapes=[pltpu.VMEM((nb, H_B, DV_B, DK_B), F32)],
        compiler_params=_params(("parallel", "arbitrary")),
        name="hgrn",
    )(r3(qb), r3(fb), r3(ib), r3(gb), s0, lbp, og, a, lm)
    return mix.reshape(n, W_B), s_out


def _outproj_even_kernel(x_ref, ma_ref, mb_ref, w_ref, y_ref):
    y = x_ref[...] + _dot(ma_ref[...].astype(BF16), w_ref[:W_A, :]) + _dot(mb_ref[...].astype(BF16), w_ref[W_A:, :])
    y_ref[...] = y


def _outproj_even(x, ma, mb, w, tm):
    n, d = x.shape
    row = lambda i: (i, 0)
    return pl.pallas_call(
        _outproj_even_kernel,
        grid=(n // tm,),
        in_specs=[pl.BlockSpec((tm, d), row), pl.BlockSpec((tm, W_A), row), pl.BlockSpec((tm, W_B), row),
                  pl.BlockSpec(w.shape, lambda i: (0, 0))],
        out_specs=pl.BlockSpec((tm, d), row),
        out_shape=jax.ShapeDtypeStruct((n, d), F32),
        compiler_params=_params(("parallel",)),
        name="outproj_even",
    )(x, ma, mb, w)


def _inproj_odd_kernel(x_ref, ng_ref, w_ref, cos_ref, sin_ref, q_ref, k_ref, v_ref, g_ref):
    x = x_ref[...]
    h = (_rms(x) * ng_ref[...]).astype(BF16)
    cos = cos_ref[...]
    sin = sin_ref[...]
    half = DK_C // 2

    def rot(a, scale):
        outs = []
        for hd in range(H_C):
            x1 = a[:, hd * DK_C:hd * DK_C + half]
            x2 = a[:, hd * DK_C + half:(hd + 1) * DK_C]
            outs += [(x1 * cos - x2 * sin) * scale, (x2 * cos + x1 * sin) * scale]
        return jnp.concatenate(outs, axis=-1)

    q_ref[...] = rot(_dot(h, w_ref[:, :QK_C]), 1.0).astype(q_ref.dtype)
    k_ref[...] = rot(_dot(h, w_ref[:, QK_C:2 * QK_C]), DK_C ** -0.5)
    for j in range(2):
        lo = 2 * QK_C + j * QK_C
        v_ref[:, j * QK_C:(j + 1) * QK_C] = _dot(h, w_ref[:, lo:lo + QK_C]).astype(v_ref.dtype)
    for j in range(2):
        lo = 2 * QK_C + W_C + j * QK_C
        g_ref[:, j * QK_C:(j + 1) * QK_C] = _silu(_dot(h, w_ref[:, lo:lo + QK_C]))


def _inproj_odd(x, ng, w, cos, sin, act_dtype, tm):
    n, d = x.shape
    npos = cos.shape[0] // tm
    row = lambda i: (i, 0)
    const = lambda i: (0, 0)
    pos = lambda i: (i % npos, 0)
    return pl.pallas_call(
        _inproj_odd_kernel,
        grid=(n // tm,),
        in_specs=[pl.BlockSpec((tm, d), row), pl.BlockSpec((1, d), const), pl.BlockSpec(w.shape, const),
                  pl.BlockSpec((tm, DK_C // 2), pos), pl.BlockSpec((tm, DK_C // 2), pos)],
        out_specs=[pl.BlockSpec((tm, QK_C), row), pl.BlockSpec((tm, QK_C), row),
                   pl.BlockSpec((tm, W_C), row), pl.BlockSpec((tm, W_C), row)],
        out_shape=[jax.ShapeDtypeStruct((n, QK_C), act_dtype), jax.ShapeDtypeStruct((n, QK_C), F32),
                   jax.ShapeDtypeStruct((n, W_C), act_dtype), jax.ShapeDtypeStruct((n, W_C), F32)],
        compiler_params=_params(("parallel",)),
        name="inproj_odd",
    )(x, ng, w, cos, sin)


def _ret_kernel(q_ref, k_ref, v_ref, g_ref, s0_ref, idec_ref, qdec_ref, kdec_ref, cdec_ref,
                o_ref, sout_ref, s_ref, *, c, n_chunks, n_valid, nb, hpg):
    t = pl.program_id(2)
    chains = [(bi, hh) for bi in range(nb) for hh in range(hpg)]

    @pl.when(t == 0)
    def _():
        s_ref[...] = s0_ref[...]

    def load(ref, bi, hh, width, ci):
        cols = slice(hh * width, (hh + 1) * width)
        if n_valid < c:
            x = ref[bi, :, cols].astype(F32)
            return jnp.concatenate([x, jnp.zeros((c - x.shape[0], x.shape[1]), F32)], axis=0)
        return ref[bi, pl.ds(pl.multiple_of(ci * c, c), c), cols]

    def chunk(ci, _):
        for bi, hh in chains:
            qc = load(q_ref, bi, hh, DK_C, ci).astype(BF16)
            kc = load(k_ref, bi, hh, DK_C, ci)
            vc = load(v_ref, bi, hh, DV_C, ci).astype(BF16)
            s = s_ref[bi, hh]
            scores = _dot_nt(qc, kc.astype(BF16)) * idec_ref[hh]
            o = _dot(scores.astype(BF16), vc) + _dot(qc, s.astype(BF16)) * qdec_ref[hh, :, 0:1]
            s_ref[bi, hh] = cdec_ref[hh, 0:1, 0:1] * s + _dot_tn((kc * kdec_ref[hh, :, 0:1]).astype(BF16), vc)
            ob = _rms(o) * load(g_ref, bi, hh, DV_C, ci)
            cols = slice(hh * DV_C, (hh + 1) * DV_C)
            if n_valid < c:
                o_ref[bi, :, cols] = ob[:n_valid].astype(o_ref.dtype)
            else:
                o_ref[bi, pl.ds(pl.multiple_of(ci * c, c), c), cols] = ob.astype(o_ref.dtype)
        return 0

    lax.fori_loop(0, n_chunks, chunk, 0)

    @pl.when(t == pl.num_programs(2) - 1)
    def _():
        sout_ref[...] = s_ref[...]


def _ret_tables(c, chunk):
    f32 = jnp.float32
    log_gamma = jnp.log1p(-jnp.exp2(-5.0 - jnp.arange(H_C, dtype=f32)))
    idx = jnp.arange(c, dtype=f32)
    real = idx < chunk
    rel = idx[:, None] - idx[None, :]
    ok = (rel >= 0) & real[:, None] & real[None, :]
    idec = jnp.exp(jnp.where(ok[None], rel[None] * log_gamma[:, None, None], -jnp.inf))
    qdec = jnp.where(real[None, :], jnp.exp((idx[None, :] + 1.0) * log_gamma[:, None]), 0.0)
    kdec = jnp.where(real[None, :], jnp.exp((chunk - 1.0 - idx[None, :]) * log_gamma[:, None]), 0.0)
    cdec = jnp.exp(chunk * log_gamma)
    bc = lambda x: jnp.broadcast_to(x[:, :, None], (H_C, c, LANES))
    return idec, bc(qdec), bc(kdec), jnp.broadcast_to(cdec[:, None, None], (H_C, 8, LANES))


def _ret(q, k, v, g, s0, batch, seq, out_dtype, tb, nb, hpg):
    n = q.shape[0]
    c = RET_CHUNK
    if seq >= c:
        n_valid, rows_blk, nt, n_chunks = c, tb, seq // tb, tb // c
    else:
        n_valid, rows_blk, nt, n_chunks = seq, seq, 1, 1
    idec, qdec, kdec, cdec = _ret_tables(c, n_valid)
    tok = lambda hg, b, t: (b, t, hg)
    st = lambda hg, b, t: (b, hg, 0, 0)
    hd = lambda hg, b, t: (hg, 0, 0)
    qk_blk = pl.BlockSpec((nb, rows_blk, hpg * DK_C), tok)
    vg_blk = pl.BlockSpec((nb, rows_blk, hpg * DV_C), tok)
    st_blk = pl.BlockSpec((nb, hpg, DK_C, DV_C), st)
    r3 = lambda x: x.reshape(batch, seq, x.shape[-1])
    mix, s_out = pl.pallas_call(
        functools.partial(_ret_kernel, c=c, n_chunks=n_chunks, n_valid=n_valid, nb=nb, hpg=hpg),
        grid=(H_C // hpg, batch // nb, nt),
        in_specs=[qk_blk, qk_blk, vg_blk, vg_blk, st_blk,
                  pl.BlockSpec((hpg, c, c), hd), pl.BlockSpec((hpg, c, LANES), hd),
                  pl.BlockSpec((hpg, c, LANES), hd), pl.BlockSpec((hpg, 8, LANES), hd)],
        out_specs=[vg_blk, st_blk],
        out_shape=[jax.ShapeDtypeStruct((batch, seq, W_C), out_dtype),
                   jax.ShapeDtypeStruct((batch, H_C, DK_C, DV_C), F32)],
        scratch_shapes=[pltpu.VMEM((nb, hpg, DK_C, DV_C), F32)],
        compiler_params=_params(("parallel", "parallel", "arbitrary")),
        name="retention",
    )(r3(q), r3(k), r3(v), r3(g), s0, idec, qdec, kdec, cdec)
    return mix.reshape(n, W_C), s_out


def _outproj_odd_kernel(x_ref, m_ref, w_ref, y_ref):
    y_ref[...] = x_ref[...] + _dot(m_ref[...].astype(BF16), w_ref[...])


def _outproj_odd(x, m, w, tm):
    n, d = x.shape
    row = lambda i: (i, 0)
    return pl.pallas_call(
        _outproj_odd_kernel,
        grid=(n // tm,),
        in_specs=[pl.BlockSpec((tm, d), row), pl.BlockSpec((tm, W_C), row), pl.BlockSpec(w.shape, lambda i: (0, 0))],
        out_specs=pl.BlockSpec((tm, d), row),
        out_shape=jax.ShapeDtypeStruct((n, d), F32),
        compiler_params=_params(("parallel",)),
        name="outproj_odd",
    )(x, m, w)


def _rope_tables(pos):
    half = DK_C // 2
    inv = 1.0 / (ROPE_BASE ** jnp.linspace(0.0, 1.0, half, dtype=F32))
    ang = pos[:, None] * inv[None, :]
    return jnp.cos(ang), jnp.sin(ang)


def _ntri(n):
    return jnp.asarray(-np.tril(np.ones((n, n), np.float32)), BF16)


def kernel(x_prompt, x_sample, cache_k, cache_v, page_table, state_hgrn, state_ret, norm_g,
           w_in_even, w_out_even, q_norm_g, k_norm_g, sb_logit_bias, hgrn_lower_bounds,
           hgrn_out_norm_g, w_in_odd, w_out_odd):
    bp, tp, d = x_prompt.shape
    bs, ts, _ = x_sample.shape
    n_pool, page_size = cache_k.shape[1], cache_k.shape[2]
    past_len = page_table.shape[1] * page_size
    depth = norm_g.shape[0]
    tm_p = 256
    tm_s = bs * ts

    yp = x_prompt.reshape(bp * tp, d)
    ys = x_sample.reshape(bs * ts, d)
    grp = jnp.asarray(np.kron(np.eye(H_A, dtype=np.float32), np.ones((HD_A, HD_A), np.float32)), BF16)
    cos_p, sin_p = _rope_tables(jnp.arange(tp, dtype=F32))
    cos_s, sin_s = _rope_tables(past_len + jnp.arange(ts, dtype=F32))
    cos_s, sin_s = jnp.tile(cos_s, (bs, 1)), jnp.tile(sin_s, (bs, 1))

    k_p, v_p, k_s, v_s, hg_p, hg_s, rt_p, rt_s = [], [], [], [], [], [], [], []
    for layer in range(depth):
        e = layer // 2
        ng = norm_g[layer].reshape(1, d)
        if layer % 2 == 0:
            w_in = w_in_even[e].astype(BF16)
            w_out = w_out_even[e].astype(BF16)
            qg = jnp.tile(q_norm_g[e], H_A).reshape(1, W_A)
            kg = jnp.tile(k_norm_g[e], H_A).reshape(1, W_A)
            og = hgrn_out_norm_g[e].reshape(1, DV_B)
            bias = sb_logit_bias[e].astype(F32)
            q, k, v, k32, v32, ga, qb, fb, ib, gb = _inproj_even(yp, ng, w_in, qg, kg, grp, BF16, tm_p)
            ma = _sb_prompt(q, k, v, ga, bias, _ntri(SB_TQ), bp, tp, SB_TQ)
            mb, hs = _hgrn(qb, fb, ib, gb, jnp.zeros((bp, H_B, DK_B, DV_B), F32), hgrn_lower_bounds,
                           og, bp, tp, e, BF16, 512, HGRN_NB)
            yp = _outproj_even(yp, ma, mb, w_out, tm_p)
            k_p.append(k32.reshape(bp, tp, H_A, HD_A))
            v_p.append(v32.reshape(bp, tp, H_A, HD_A))
            hg_p.append(hs)
            q, k, v, k32, v32, ga, qb, fb, ib, gb = _inproj_even(ys, ng, w_in, qg, kg, grp, F32, tm_s)
            ma = _sb_sample(q, k, v, ga, jnp.repeat(bias * LOG2E, ts).reshape(H_A * ts, 1), _ntri(page_size),
                            cache_k[e].reshape(n_pool, page_size * H_A, HD_A),
                            cache_v[e].reshape(n_pool, page_size * H_A, HD_A), page_table, ts, SB_PAGES_PER_STEP)
            mb, hs = _hgrn(qb, fb, ib, gb, state_hgrn[e].astype(F32), hgrn_lower_bounds, og, bs, ts, e, F32, ts,
                           HGRN_NB)
            ys = _outproj_even(ys, ma, mb, w_out, tm_s)
            k_s.append(k32.reshape(bs, ts, H_A, HD_A))
            v_s.append(v32.reshape(bs, ts, H_A, HD_A))
            hg_s.append(hs)
        else:
            w_in = w_in_odd[e].astype(BF16)
            w_out = w_out_odd[e].astype(BF16)
            q, k, v, g = _inproj_odd(yp, ng, w_in, cos_p, sin_p, BF16, tm_p)
            m, s = _ret(q, k, v, g, jnp.zeros((bp, H_C, DK_C, DV_C), F32), bp, tp, BF16, 512, RET_NB, RET_HPG)
            yp = _outproj_odd(yp, m, w_out, tm_p)
            rt_p.append(s)
            q, k, v, g = _inproj_odd(ys, ng, w_in, cos_s, sin_s, F32, tm_s)
            m, s = _ret(q, k, v, g, state_ret[e].astype(F32), bs, ts, F32, ts, RET_NB, RET_HPG)
            ys = _outproj_odd(ys, m, w_out, tm_s)
            rt_s.append(s)
    return (yp.reshape(bp, tp, d), ys.reshape(bs, ts, d), jnp.stack(k_p), jnp.stack(v_p), jnp.stack(k_s),
            jnp.stack(v_s), jnp.stack(hg_p), jnp.stack(hg_s), jnp.stack(rt_p), jnp.stack(rt_s))
```

```python
import functools
import math

import numpy as np
import jax
import jax.numpy as jnp
from jax import lax
from jax.experimental import pallas as pl
from jax.experimental.pallas import tpu as pltpu

F32 = jnp.float32
BF16 = jnp.bfloat16

H_A, HD_A = 8, 64
W_A = H_A * HD_A
H_B, DK_B, DV_B = 4, 128, 128
F_B = H_B * DK_B
W_B = H_B * DV_B
H_C, DK_C, DV_C = 4, 256, 512
QK_C = H_C * DK_C
W_C = H_C * DV_C
HGRN_CHUNK = 64
RET_CHUNK = 128
ROPE_BASE = 10000.0
NORM_EPS = 1e-6

LANES = 128
VMEM_LIMIT = 56 * 1024 * 1024
SB_TQ = 256
RET_NB, RET_HPG = 2, 2
HGRN_NB = 2
SB_RING = 3
SB_PAGES_PER_STEP = 16
SB_SPLIT = False
LOG2E = 1.4426950408889634


def _dot(a, b):
    return jnp.dot(a, b, preferred_element_type=F32)


def _dot_nt(a, b):
    return lax.dot_general(a, b, (((1,), (1,)), ((), ())), preferred_element_type=F32)


def _dot_tn(a, b):
    return lax.dot_general(a, b, (((0,), (0,)), ((), ())), preferred_element_type=F32)


def _split2(x):
    hi = x.astype(BF16)
    lo = (x - hi.astype(F32)).astype(BF16)
    return hi, lo


def _split3(x):
    hi = x.astype(BF16)
    r = x - hi.astype(F32)
    mid = r.astype(BF16)
    lo = (r - mid.astype(F32)).astype(BF16)
    return hi, mid, lo


def _sigmoid(x):
    return 1.0 / (1.0 + jnp.exp(-x))


def _silu(x):
    return x * _sigmoid(x)


def _rms(x, eps=NORM_EPS):
    return x * lax.rsqrt(jnp.mean(x * x, axis=-1, keepdims=True) + eps)


def _params(sem):
    return pltpu.CompilerParams(dimension_semantics=sem, vmem_limit_bytes=VMEM_LIMIT)


def _inproj_even_kernel(x_ref, ng_ref, w_ref, qg_ref, kg_ref, grp_ref,
                        q_ref, k_ref, v_ref, k32_ref, v32_ref, ga_ref,
                        qb_ref, fb_ref, ib_ref, gb_ref, *, kv_token_minor):
    x = x_ref[...]
    h = (_rms(x) * ng_ref[...]).astype(BF16)

    def proj(c):
        return _dot(h, w_ref[:, c * W_A:(c + 1) * W_A])

    def head_norm(a, gain):
        hi, lo = _split2(a * a)
        ssum = _dot(hi, grp_ref[...]) + _dot(lo, grp_ref[...])
        return a * lax.rsqrt(ssum * (1.0 / HD_A) + NORM_EPS) * gain

    qa = head_norm(proj(0), qg_ref[...])
    q_ref[...] = (qa * (HD_A ** -0.5 * LOG2E)).astype(q_ref.dtype)
    ka = head_norm(proj(1), kg_ref[...])
    k32_ref[...] = ka.T if kv_token_minor else ka
    k_ref[...] = ka.astype(k_ref.dtype)
    va = proj(2)
    v32_ref[...] = va.T if kv_token_minor else va
    v_ref[...] = va.astype(v_ref.dtype)
    ga_ref[...] = _silu(proj(3))
    qb_ref[...] = proj(4)
    fb_ref[...] = proj(5)
    ib_ref[...] = proj(6)
    gb_ref[...] = _silu(proj(7))


def _inproj_even(x, ng, w, qg, kg, grp, act_dtype, tm, kv_seq=None):
    n, d = x.shape
    row = lambda i: (i, 0)
    const = lambda i: (0, 0)
    blk = pl.BlockSpec((tm, W_A), row)
    outs = [jax.ShapeDtypeStruct((n, W_A), act_dtype)] * 3 + [jax.ShapeDtypeStruct((n, W_A), F32)] * 7
    out_specs = [blk] * 10
    if kv_seq is not None:
        nt = kv_seq // tm
        kv_blk = pl.BlockSpec((None, W_A, tm), lambda i: (i // nt, 0, i % nt))
        kv_shape = jax.ShapeDtypeStruct((n // kv_seq, W_A, kv_seq), F32)
        outs[3:5] = [kv_shape, kv_shape]
        out_specs[3:5] = [kv_blk, kv_blk]
    return pl.pallas_call(
        functools.partial(_inproj_even_kernel, kv_token_minor=kv_seq is not None),
        grid=(n // tm,),
        in_specs=[pl.BlockSpec((tm, d), row), pl.BlockSpec((1, d), const),
                  pl.BlockSpec(w.shape, const), pl.BlockSpec((1, W_A), const),
                  pl.BlockSpec((1, W_A), const), pl.BlockSpec((W_A, W_A), const)],
        out_specs=out_specs,
        out_shape=outs,
        compiler_params=_params(("parallel",)),
        name="inproj_even",
    )(x, ng, w, qg, kg, grp)


def _sb_block(z, pv, carry, ntri, mask):
    incl = _sb_incl(_sb_nl(z, mask), ntri)
    w = jnp.exp2(z + (carry + incl))
    if mask is not None:
        w = jnp.where(mask, w, 0.0)
    return pv(w), carry + incl[:, 0:1]


def _sb_nl(z, mask):
    neg_abs = lax.bitcast_convert_type(lax.bitcast_convert_type(z, jnp.uint32) | jnp.uint32(0x80000000), F32)
    nl = jnp.maximum(z, 0.0) + jnp.log(1.0 + jnp.exp2(neg_abs)) * LOG2E
    if mask is not None:
        nl = jnp.where(mask, nl, 0.0)
    return nl


def _sb_incl(nl, ntri):
    if SB_SPLIT:
        hi, lo = _split2(nl)
        return _dot(hi, ntri) + _dot(lo, ntri)
    return _dot(nl.astype(BF16), ntri)


def _sb_prompt_kernel(bias_ref, q_ref, k_ref, v_ref, ga_ref, ntri_ref, o_ref, qs_ref, acc_ref, carry_ref,
                      z_ref, zi_ref, col_ref, *, tq, nq):
    hp = pl.program_id(1)
    rows = 2 * tq
    lane = lax.broadcasted_iota(jnp.int32, (tq, LANES), 1)
    b0 = bias_ref[2 * hp] * LOG2E
    b1 = bias_ref[2 * hp + 1] * LOG2E

    def blk(i):
        return pl.ds(pl.multiple_of(i * tq, tq), tq)

    def logits(qs, kb):
        zr = _dot_nt(qs, k_ref[blk(kb), :])
        return jnp.concatenate([zr[:tq] + b0, zr[tq:] + b1], axis=0)

    r2 = lax.broadcasted_iota(jnp.int32, (rows, tq), 0)
    c2 = lax.broadcasted_iota(jnp.int32, (rows, tq), 1)
    mask = c2 < jnp.where(r2 >= tq, r2 - tq, r2)

    def diag(qi, _):
        q = q_ref[blk(qi), :].astype(F32)
        qs = jnp.concatenate([jnp.where(lane < HD_A, q, 0.0), jnp.where(lane >= HD_A, q, 0.0)], axis=0).astype(BF16)
        qs_ref[qi] = qs
        contrib, carry = _sb_block(logits(qs, qi), lambda w: _dot(w.astype(BF16), v_ref[blk(qi), :]),
                                   jnp.zeros((rows, 1), F32), ntri_ref[...], mask)
        acc_ref[qi] = contrib
        carry_ref[qi] = carry
        return 0

    lax.fori_loop(0, nq, diag, 0)

    def stage_a(p, slot):
        z_ref[slot] = logits(qs_ref[p[0]], p[1])

    def stage_b(slot):
        z = z_ref[slot]
        incl = _sb_incl(_sb_nl(z, None), ntri_ref[...])
        zi_ref[slot] = z + incl
        col_ref[slot] = incl[:, 0:1]

    def stage_c(p, slot):
        carry = carry_ref[p[0]]
        w = jnp.exp2(zi_ref[slot] + carry)
        acc_ref[p[0]] += _dot(w.astype(BF16), v_ref[blk(p[1]), :])
        carry_ref[p[0]] = carry + col_ref[slot]

    def nxt(p):
        wrap = p[1] == 0
        return jnp.where(wrap, p[0] + 1, p[0]), jnp.where(wrap, p[0], p[1] - 1)

    def step(t, pc, pa):
        stage_c(pc, t % SB_RING)
        stage_b((t + 1) % SB_RING)
        stage_a(pa, (t + 2) % SB_RING)
        return nxt(pc), nxt(pa)

    n_pairs = nq * (nq - 1) // 2
    p0 = (jnp.int32(1), jnp.int32(0))
    if n_pairs == 1:
        stage_a(p0, 0)
        stage_b(0)
        stage_c(p0, 0)
    elif n_pairs >= 2:
        p1 = nxt(p0)
        stage_a(p0, 0)
        stage_b(0)
        stage_a(p1, 1)
        n_steady = n_pairs - 2
        n_trips = n_steady // SB_RING

        def body(_, st):
            pc, pa = st[:2], st[2:]
            for u in range(SB_RING):
                pc, pa = step(u, pc, pa)
            return (*pc, *pa)

        st = lax.fori_loop(0, n_trips, body, (*p0, *nxt(p1)))
        pc, pa = st[:2], st[2:]
        for t in range(n_trips * SB_RING, n_steady):
            pc, pa = step(t, pc, pa)
        stage_c(pc, n_steady % SB_RING)
        stage_b((n_steady + 1) % SB_RING)
        stage_c(nxt(pc), (n_steady + 1) % SB_RING)

    def finish(qi, _):
        acc = acc_ref[qi]
        o = jnp.where(lane < HD_A, acc[:tq], acc[tq:]) * ga_ref[blk(qi), :]
        o_ref[blk(qi), :] = o.astype(o_ref.dtype)
        return 0

    lax.fori_loop(0, nq, finish, 0)


def _sb_prompt(q, k, v, ga, bias, ntri, batch, seq, tq):
    n = q.shape[0]
    nq = seq // tq
    seqmap = lambda b, hp: (b, hp)
    blk = pl.BlockSpec((seq, LANES), seqmap)
    return pl.pallas_call(
        functools.partial(_sb_prompt_kernel, tq=tq, nq=nq),
        grid=(batch, H_A // 2),
        in_specs=[pl.BlockSpec(memory_space=pltpu.SMEM), blk, blk, blk, blk,
                  pl.BlockSpec((tq, tq), lambda b, hp: (0, 0))],
        out_specs=blk,
        out_shape=jax.ShapeDtypeStruct((n, W_A), BF16),
        scratch_shapes=[pltpu.VMEM((nq, 2 * tq, LANES), BF16), pltpu.VMEM((nq, 2 * tq, LANES), F32),
                        pltpu.VMEM((nq, 2 * tq, 1), F32),
                        pltpu.VMEM((SB_RING, 2 * tq, tq), F32), pltpu.VMEM((SB_RING, 2 * tq, tq), F32),
                        pltpu.VMEM((SB_RING, 2 * tq, 1), F32)],
        compiler_params=_params(("parallel", "parallel")),
        name="sb_prompt",
    )(bias, q, k, v, ga, ntri)


def _sb_sample_kernel(pt_ref, q_ref, kn_ref, vn_ref, ga_ref, bias_ref, ntri_ref, *rest, ts, pages):
    page_refs = rest[:2 * pages]
    o_ref, acc_ref, carry_ref = rest[2 * pages:]
    s = pl.program_id(1)
    rows = H_A * ts
    ps = ntri_ref.shape[0]
    q = q_ref[...]
    qpair = [jnp.concatenate([q[:, (2 * p) * HD_A:(2 * p + 1) * HD_A], q[:, (2 * p + 1) * HD_A:(2 * p + 2) * HD_A]],
                             axis=0).astype(BF16) for p in range(H_A // 2)]
    bias = bias_ref[...]
    ntri = ntri_ref[...]

    def own_rows(x, h):
        return x[(h % 2) * ts:(h % 2 + 1) * ts]

    def sweep(n, qk, pv, carry, mask):
        z = jnp.concatenate(
            [jnp.concatenate([own_rows(qk(j, h, qpair[h // 2]), h) for h in range(H_A)], axis=0)
             for j in range(n)], axis=1) + bias
        nl = _sb_nl(z, mask)
        incls = [_sb_incl(nl[:, j * ps:(j + 1) * ps], ntri) for j in range(n)]
        sums = []
        for j in range(n):
            sums.append(carry + incls[j])
            carry = carry + incls[j][:, 0:1]
        w = jnp.exp2(z + jnp.concatenate(sums, axis=1))
        if mask is not None:
            w = jnp.where(mask, w, 0.0)
        w = w.astype(BF16)
        parts = [jnp.concatenate([own_rows(pv(j, h, w[(h // 2) * 2 * ts:(h // 2 + 1) * 2 * ts, j * ps:(j + 1) * ps]), h)
                                  for h in range(H_A)], axis=0) for j in range(n)]
        contrib = parts[0]
        for c in parts[1:]:
            contrib = contrib + c
        return contrib, carry

    @pl.when(s == 0)
    def _():
        pad = jnp.zeros((ps - ts, HD_A), F32)
        kn = kn_ref[...]
        vn = vn_ref[...]
        new_k = lambda h: jnp.concatenate([kn[:, h * HD_A:(h + 1) * HD_A], pad], axis=0).astype(BF16)
        new_v = lambda h: jnp.concatenate([vn[:, h * HD_A:(h + 1) * HD_A], pad], axis=0).astype(BF16)
        r2 = lax.broadcasted_iota(jnp.int32, (rows, ps), 0)
        c2 = lax.broadcasted_iota(jnp.int32, (rows, ps), 1)
        mask = c2 < lax.rem(r2, ts)
        contrib, carry = sweep(1, lambda j, h, q16: _dot_nt(q16, new_k(h)), lambda j, h, w16: _dot(w16, new_v(h)),
                               jnp.zeros((rows, 1), F32), mask)
        acc_ref[...] = contrib
        carry_ref[...] = carry

    contrib, carry = sweep(pages,
                           lambda j, h, q16: _dot(q16, page_refs[2 * j][h].astype(BF16)),
                           lambda j, h, w16: _dot_nt(w16, page_refs[2 * j + 1][h].astype(BF16)),
                           carry_ref[...], None)
    acc_ref[...] += contrib
    carry_ref[...] = carry

    @pl.when(s == pl.num_programs(1) - 1)
    def _():
        acc = acc_ref[...]
        o = jnp.concatenate([acc[h * ts:(h + 1) * ts] for h in range(H_A)], axis=-1)
        o_ref[...] = (o * ga_ref[...]).astype(o_ref.dtype)


def _sb_sample(q, kn, vn, ga, bias_rows, ntri, cache_k, cache_v, page_table, ts, pages):
    n = q.shape[0]
    nb, n_pages = page_table.shape
    ps = cache_k.shape[3]
    steps = n_pages // pages
    tok = lambda b, s, pt: (b, 0)
    const = lambda b, s, pt: (0, 0)

    def page_map(j):
        return lambda b, s, pt: (pt[b, n_pages - 1 - (s * pages + j)], 0, 0, 0)

    page_specs, page_args = [], []
    for j in range(pages):
        page_specs += [pl.BlockSpec((None, H_A, HD_A, ps), page_map(j))] * 2
        page_args += [cache_k, cache_v]
    rows = H_A * ts
    grid_spec = pltpu.PrefetchScalarGridSpec(
        num_scalar_prefetch=1,
        grid=(nb, steps),
        in_specs=[pl.BlockSpec((ts, W_A), tok)] * 4
                 + [pl.BlockSpec((rows, 1), const), pl.BlockSpec((ps, ps), const)] + page_specs,
        out_specs=pl.BlockSpec((ts, W_A), tok),
        scratch_shapes=[pltpu.VMEM((rows, HD_A), F32), pltpu.VMEM((rows, 1), F32)],
    )
    return pl.pallas_call(
        functools.partial(_sb_sample_kernel, ts=ts, pages=pages),
        grid_spec=grid_spec,
        out_shape=jax.ShapeDtypeStruct((n, W_A), F32),
        compiler_params=_params(("parallel", "arbitrary")),
        name="sb_sample",
    )(page_table, q, kn, vn, ga, bias_rows, ntri, *page_args)


def _hgrn_tables(c, n_valid):
    levels = int(math.log2(c))
    a = np.zeros((levels + 2, c, c), np.float32)
    lm = np.zeros((levels, c, c), np.float32)
    r = np.arange(c)
    for l in range(levels):
        m = 2 ** l
        mid = (r // (2 * m)) * (2 * m) + m - 1
        second = (r // m) % 2 == 1
        for t in range(c):
            if second[t]:
                a[l, t, mid[t] + 1:t + 1] = 1.0
            else:
                a[l, t, t + 1:mid[t] + 1] = 1.0
        same = (r[:, None] // (2 * m)) == (r[None, :] // (2 * m))
        lm[l] = (same & second[:, None] & (~second)[None, :]).astype(np.float32)
    a[levels] = np.tril(np.ones((c, c), np.float32))
    a[levels + 1] = np.triu(np.ones((c, c), np.float32), 1)
    a[:, :, n_valid:] = 0.0
    return a.reshape((levels + 2) * c, c), lm


def _hgrn_kernel(q_ref, f_ref, i_ref, g_ref, s0_ref, lbp_ref, og_ref, a_ref, lm_ref,
                 o_ref, sout_ref, st_ref, *, c, n_chunks, n_valid, layer, nb):
    t = pl.program_id(1)
    levels = lm_ref.shape[0]
    chains = [(bi, h) for bi in range(nb) for h in range(H_B)]

    @pl.when(t == 0)
    def _():
        for bi, h in chains:
            st_ref[bi, h] = s0_ref[bi, h].T

    lbp = lbp_ref[...]
    p = jnp.exp(lbp - jnp.max(lbp, axis=0, keepdims=True))
    lb = jnp.sum(p[:layer + 1], axis=0, keepdims=True) / jnp.sum(p, axis=0, keepdims=True)
    og = og_ref[...]
    rowi = lax.broadcasted_iota(jnp.int32, (c, 1), 0)
    valid = rowi < n_valid

    def load(ref, bi, ci):
        if n_valid < c:
            x = ref[bi]
            return jnp.concatenate([x, jnp.zeros((c - x.shape[0], x.shape[1]), F32)], axis=0)
        return ref[bi, pl.ds(pl.multiple_of(ci * c, c), c), :]

    def chunk(ci, _):
        gs, kks, qqs, vvs = [], [], [], []
        for bi in range(nb):
            xf = load(f_ref, bi, ci)
            e = jnp.exp(-jnp.abs(xf))
            r = 1.0 / (1.0 + e)
            sig_pos = jnp.where(xf >= 0, r, e * r)
            sig_neg = jnp.where(xf >= 0, e * r, r)
            g = jnp.log(lb + (1.0 - lb) * sig_pos)
            kk = (1.0 - lb) * sig_neg
            if n_valid < c:
                g = jnp.where(valid, g, 0.0)
                kk = jnp.where(valid, kk, 0.0)
            gs.append(g)
            kks.append(kk)
            qqs.append(_silu(load(q_ref, bi, ci)))
            vvs.append(load(i_ref, bi, ci))
        ghi, gmid, glo = _split3(jnp.concatenate(gs, axis=1))
        a = a_ref[...]
        ee = _dot(a, ghi) + _dot(a, gmid) + _dot(a, glo)
        outs = [[] for _ in range(nb)]
        for bi, h in chains:
            hl = slice(h * LANES, (h + 1) * LANES)
            el = slice((bi * H_B + h) * LANES, (bi * H_B + h + 1) * LANES)
            qq, kk, vv = qqs[bi][:, hl], kks[bi][:, hl], vvs[bi][:, hl]
            vv16 = vv.astype(BF16)
            bcum = ee[levels * c:(levels + 1) * c, el]
            rem = ee[(levels + 1) * c:, el]
            st = st_ref[bi, h]
            o = _dot_nt((qq * jnp.exp(bcum)).astype(BF16), st.astype(BF16))
            scores = jnp.zeros((c, c), F32)
            for l in range(levels):
                x = jnp.exp(ee[l * c:(l + 1) * c, el])
                second = jnp.bitwise_and(jnp.right_shift(rowi, l), 1) == 1
                qt = jnp.where(second, qq * x, 0.0).astype(BF16)
                kt = jnp.where(second, 0.0, kk * x).astype(BF16)
                scores = scores + _dot_nt(qt, kt) * lm_ref[l]
            o = o + _dot(scores.astype(BF16), vv16)
            o = o + jnp.sum(qq * kk, axis=-1, keepdims=True) * vv
            dec_last = jnp.exp(bcum[c - 1:c, :])
            st_ref[bi, h] = dec_last * st + _dot_tn(vv16, (kk * jnp.exp(rem)).astype(BF16))
            outs[bi].append(_rms(o) * og)
        for bi in range(nb):
            ob = jnp.concatenate(outs[bi], axis=1) * load(g_ref, bi, ci)
            if n_valid < c:
                o_ref[bi] = ob[:n_valid].astype(o_ref.dtype)
            else:
                o_ref[bi, pl.ds(pl.multiple_of(ci * c, c), c), :] = ob.astype(o_ref.dtype)
        return 0

    lax.fori_loop(0, n_chunks, chunk, 0)

    @pl.when(t == pl.num_programs(1) - 1)
    def _():
        for bi, h in chains:
            sout_ref[bi, h] = st_ref[bi, h].T


def _hgrn(qb, fb, ib, gb, s0, lbp, og, batch, seq, layer, out_dtype, tb, nb):
    n = qb.shape[0]
    c = HGRN_CHUNK
    if seq >= c:
        n_valid, rows_blk, nt, n_chunks = c, tb, seq // tb, tb // c
    else:
        n_valid, rows_blk, nt, n_chunks = seq, seq, 1, 1
    a_np, lm_np = _hgrn_tables(c, n_valid)
    a = jnp.asarray(a_np, BF16)
    lm = jnp.asarray(lm_np, F32)
    tok = lambda b, t: (b, t, 0)
    st = lambda b, t: (b, 0, 0, 0)
    blk = pl.BlockSpec((nb, rows_blk, W_B), tok)
    st_blk = pl.BlockSpec((nb, H_B, DK_B, DV_B), st)
    r3 = lambda x: x.reshape(batch, seq, W_B)
    mix, s_out = pl.pallas_call(
        functools.partial(_hgrn_kernel, c=c, n_chunks=n_chunks, n_valid=n_valid, layer=layer, nb=nb),
        grid=(batch // nb, nt),
        in_specs=[blk, blk, blk, blk, st_blk,
                  pl.BlockSpec(lbp.shape, lambda b, t: (0, 0)),
                  pl.BlockSpec((1, LANES), lambda b, t: (0, 0)),
                  pl.BlockSpec(a.shape, lambda b, t: (0, 0)),
                  pl.BlockSpec(lm.shape, lambda b, t: (0, 0, 0))],
        out_specs=[blk, st_blk],
        out_shape=[jax.ShapeDtypeStruct((batch, seq, W_B), out_dtype),
                   jax.ShapeDtypeStruct((batch, H_B, DK_B, DV_B), F32)],
        scratch_shapes=[pltpu.VMEM((nb, H_B, DV_B, DK_B), F32)],
        compiler_params=_params(("parallel", "arbitrary")),
        name="hgrn",
    )(r3(qb), r3(fb), r3(ib), r3(gb), s0, lbp, og, a, lm)
    return mix.reshape(n, W_B), s_out


def _outproj_even_kernel(x_ref, ma_ref, mb_ref, w_ref, y_ref):
    y = x_ref[...] + _dot(ma_ref[...].astype(BF16), w_ref[:W_A, :]) + _dot(mb_ref[...].astype(BF16), w_ref[W_A:, :])
    y_ref[...] = y


def _outproj_even(x, ma, mb, w, tm):
    n, d = x.shape
    row = lambda i: (i, 0)
    return pl.pallas_call(
        _outproj_even_kernel,
        grid=(n // tm,),
        in_specs=[pl.BlockSpec((tm, d), row), pl.BlockSpec((tm, W_A), row), pl.BlockSpec((tm, W_B), row),
                  pl.BlockSpec(w.shape, lambda i: (0, 0))],
        out_specs=pl.BlockSpec((tm, d), row),
        out_shape=jax.ShapeDtypeStruct((n, d), F32),
        compiler_params=_params(("parallel",)),
        name="outproj_even",
    )(x, ma, mb, w)


def _inproj_odd_kernel(x_ref, ng_ref, w_ref, cos_ref, sin_ref, q_ref, k_ref, v_ref, g_ref):
    x = x_ref[...]
    h = (_rms(x) * ng_ref[...]).astype(BF16)
    cos = cos_ref[...]
    sin = sin_ref[...]
    half = DK_C // 2

    def rot(a, scale):
        outs = []
        for hd in range(H_C):
            x1 = a[:, hd * DK_C:hd * DK_C + half]
            x2 = a[:, hd * DK_C + half:(hd + 1) * DK_C]
            outs += [(x1 * cos - x2 * sin) * scale, (x2 * cos + x1 * sin) * scale]
        return jnp.concatenate(outs, axis=-1)

    q_ref[...] = rot(_dot(h, w_ref[:, :QK_C]), 1.0).astype(q_ref.dtype)
    k_ref[...] = rot(_dot(h, w_ref[:, QK_C:2 * QK_C]), DK_C ** -0.5)
    for j in range(2):
        lo = 2 * QK_C + j * QK_C
        v_ref[:, j * QK_C:(j + 1) * QK_C] = _dot(h, w_ref[:, lo:lo + QK_C]).astype(v_ref.dtype)
    for j in range(2):
        lo = 2 * QK_C + W_C + j * QK_C
        g_ref[:, j * QK_C:(j + 1) * QK_C] = _silu(_dot(h, w_ref[:, lo:lo + QK_C]))


def _inproj_odd(x, ng, w, cos, sin, act_dtype, tm):
    n, d = x.shape
    npos = cos.shape[0] // tm
    row = lambda i: (i, 0)
    const = lambda i: (0, 0)
    pos = lambda i: (i % npos, 0)
    return pl.pallas_call(
        _inproj_odd_kernel,
        grid=(n // tm,),
        in_specs=[pl.BlockSpec((tm, d), row), pl.BlockSpec((1, d), const), pl.BlockSpec(w.shape, const),
                  pl.BlockSpec((tm, DK_C // 2), pos), pl.BlockSpec((tm, DK_C // 2), pos)],
        out_specs=[pl.BlockSpec((tm, QK_C), row), pl.BlockSpec((tm, QK_C), row),
                   pl.BlockSpec((tm, W_C), row), pl.BlockSpec((tm, W_C), row)],
        out_shape=[jax.ShapeDtypeStruct((n, QK_C), act_dtype), jax.ShapeDtypeStruct((n, QK_C), F32),
                   jax.ShapeDtypeStruct((n, W_C), act_dtype), jax.ShapeDtypeStruct((n, W_C), F32)],
        compiler_params=_params(("parallel",)),
        name="inproj_odd",
    )(x, ng, w, cos, sin)


def _ret_kernel(q_ref, k_ref, v_ref, g_ref, s0_ref, idec_ref, qdec_ref, kdec_ref, cdec_ref,
                o_ref, sout_ref, s_ref, *, c, n_chunks, n_valid, nb, hpg):
    t = pl.program_id(2)
    chains = [(bi, hh) for bi in range(nb) for hh in range(hpg)]

    @pl.when(t == 0)
    def _():
        s_ref[...] = s0_ref[...]

    def load(ref, bi, hh, width, ci):
        cols = slice(hh * width, (hh + 1) * width)
        if n_valid < c:
            x = ref[bi, :, cols].astype(F32)
            return jnp.concatenate([x, jnp.zeros((c - x.shape[0], x.shape[1]), F32)], axis=0)
        return ref[bi, pl.ds(pl.multiple_of(ci * c, c), c), cols]

    def chunk(ci, _):
        for bi, hh in chains:
            qc = load(q_ref, bi, hh, DK_C, ci).astype(BF16)
            kc = load(k_ref, bi, hh, DK_C, ci)
            vc = load(v_ref, bi, hh, DV_C, ci).astype(BF16)
            s = s_ref[bi, hh]
            scores = _dot_nt(qc, kc.astype(BF16)) * idec_ref[hh]
            o = _dot(scores.astype(BF16), vc) + _dot(qc, s.astype(BF16)) * qdec_ref[hh, :, 0:1]
            s_ref[bi, hh] = cdec_ref[hh, 0:1, 0:1] * s + _dot_tn((kc * kdec_ref[hh, :, 0:1]).astype(BF16), vc)
            ob = _rms(o) * load(g_ref, bi, hh, DV_C, ci)
            cols = slice(hh * DV_C, (hh + 1) * DV_C)
            if n_valid < c:
                o_ref[bi, :, cols] = ob[:n_valid].astype(o_ref.dtype)
            else:
                o_ref[bi, pl.ds(pl.multiple_of(ci * c, c), c), cols] = ob.astype(o_ref.dtype)
        return 0

    lax.fori_loop(0, n_chunks, chunk, 0)

    @pl.when(t == pl.num_programs(2) - 1)
    def _():
        sout_ref[...] = s_ref[...]


def _ret_tables(c, chunk):
    f32 = jnp.float32
    log_gamma = jnp.log1p(-jnp.exp2(-5.0 - jnp.arange(H_C, dtype=f32)))
    idx = jnp.arange(c, dtype=f32)
    real = idx < chunk
    rel = idx[:, None] - idx[None, :]
    ok = (rel >= 0) & real[:, None] & real[None, :]
    idec = jnp.exp(jnp.where(ok[None], rel[None] * log_gamma[:, None, None], -jnp.inf))
    qdec = jnp.where(real[None, :], jnp.exp((idx[None, :] + 1.0) * log_gamma[:, None]), 0.0)
    kdec = jnp.where(real[None, :], jnp.exp((chunk - 1.0 - idx[None, :]) * log_gamma[:, None]), 0.0)
    cdec = jnp.exp(chunk * log_gamma)
    bc = lambda x: jnp.broadcast_to(x[:, :, None], (H_C, c, LANES))
    return idec, bc(qdec), bc(kdec), jnp.broadcast_to(cdec[:, None, None], (H_C, 8, LANES))


def _ret(q, k, v, g, s0, batch, seq, out_dtype, tb, nb, hpg):
    n = q.shape[0]
    c = RET_CHUNK
    if seq >= c:
        n_valid, rows_blk, nt, n_chunks = c, tb, seq // tb, tb // c
    else:
        n_valid, rows_blk, nt, n_chunks = seq, seq, 1, 1
    idec, qdec, kdec, cdec = _ret_tables(c, n_valid)
    tok = lambda hg, b, t: (b, t, hg)
    st = lambda hg, b, t: (b, hg, 0, 0)
    hd = lambda hg, b, t: (hg, 0, 0)
    qk_blk = pl.BlockSpec((nb, rows_blk, hpg * DK_C), tok)
    vg_blk = pl.BlockSpec((nb, rows_blk, hpg * DV_C), tok)
    st_blk = pl.BlockSpec((nb, hpg, DK_C, DV_C), st)
    r3 = lambda x: x.reshape(batch, seq, x.shape[-1])
    mix, s_out = pl.pallas_call(
        functools.partial(_ret_kernel, c=c, n_chunks=n_chunks, n_valid=n_valid, nb=nb, hpg=hpg),
        grid=(H_C // hpg, batch // nb, nt),
        in_specs=[qk_blk, qk_blk, vg_blk, vg_blk, st_blk,
                  pl.BlockSpec((hpg, c, c), hd), pl.BlockSpec((hpg, c, LANES), hd),
                  pl.BlockSpec((hpg, c, LANES), hd), pl.BlockSpec((hpg, 8, LANES), hd)],
        out_specs=[vg_blk, st_blk],
        out_shape=[jax.ShapeDtypeStruct((batch, seq, W_C), out_dtype),
                   jax.ShapeDtypeStruct((batch, H_C, DK_C, DV_C), F32)],
        scratch_shapes=[pltpu.VMEM((nb, hpg, DK_C, DV_C), F32)],
        compiler_params=_params(("parallel", "parallel", "arbitrary")),
        name="retention",
    )(r3(q), r3(k), r3(v), r3(g), s0, idec, qdec, kdec, cdec)
    return mix.reshape(n, W_C), s_out


def _outproj_odd_kernel(x_ref, m_ref, w_ref, y_ref):
    y_ref[...] = x_ref[...] + _dot(m_ref[...].astype(BF16), w_ref[...])


def _outproj_odd(x, m, w, tm):
    n, d = x.shape
    row = lambda i: (i, 0)
    return pl.pallas_call(
        _outproj_odd_kernel,
        grid=(n // tm,),
        in_specs=[pl.BlockSpec((tm, d), row), pl.BlockSpec((tm, W_C), row), pl.BlockSpec(w.shape, lambda i: (0, 0))],
        out_specs=pl.BlockSpec((tm, d), row),
        out_shape=jax.ShapeDtypeStruct((n, d), F32),
        compiler_params=_params(("parallel",)),
        name="outproj_odd",
    )(x, m, w)


def _rope_tables(pos):
    half = DK_C // 2
    inv = 1.0 / (ROPE_BASE ** jnp.linspace(0.0, 1.0, half, dtype=F32))
    ang = pos[:, None] * inv[None, :]
    return jnp.cos(ang), jnp.sin(ang)


def _ntri(n):
    return jnp.asarray(-np.tril(np.ones((n, n), np.float32)), BF16)


def kernel(x_prompt, x_sample, cache_k, cache_v, page_table, state_hgrn, state_ret, norm_g,
           w_in_even, w_out_even, q_norm_g, k_norm_g, sb_logit_bias, hgrn_lower_bounds,
           hgrn_out_norm_g, w_in_odd, w_out_odd):
    bp, tp, d = x_prompt.shape
    bs, ts, _ = x_sample.shape
    n_pool, page_size = cache_k.shape[1], cache_k.shape[2]
    past_len = page_table.shape[1] * page_size
    depth = norm_g.shape[0]
    tm_p = 256
    tm_s = bs * ts

    yp = x_prompt.reshape(bp * tp, d)
    ys = x_sample.reshape(bs * ts, d)
    grp = jnp.asarray(np.kron(np.eye(H_A, dtype=np.float32), np.ones((HD_A, HD_A), np.float32)), BF16)
    cos_p, sin_p = _rope_tables(jnp.arange(tp, dtype=F32))
    cos_s, sin_s = _rope_tables(past_len + jnp.arange(ts, dtype=F32))
    cos_s, sin_s = jnp.tile(cos_s, (bs, 1)), jnp.tile(sin_s, (bs, 1))

    k_p, v_p, k_s, v_s, hg_p, hg_s, rt_p, rt_s = [], [], [], [], [], [], [], []
    for layer in range(depth):
        e = layer // 2
        ng = norm_g[layer].reshape(1, d)
        if layer % 2 == 0:
            w_in = w_in_even[e].astype(BF16)
            w_out = w_out_even[e].astype(BF16)
            qg = jnp.tile(q_norm_g[e], H_A).reshape(1, W_A)
            kg = jnp.tile(k_norm_g[e], H_A).reshape(1, W_A)
            og = hgrn_out_norm_g[e].reshape(1, DV_B)
            bias = sb_logit_bias[e].astype(F32)
            q, k, v, k32, v32, ga, qb, fb, ib, gb = _inproj_even(yp, ng, w_in, qg, kg, grp, BF16, tm_p, kv_seq=tp)
            ma = _sb_prompt(q, k, v, ga, bias, _ntri(SB_TQ), bp, tp, SB_TQ)
            mb, hs = _hgrn(qb, fb, ib, gb, jnp.zeros((bp, H_B, DK_B, DV_B), F32), hgrn_lower_bounds,
                           og, bp, tp, e, BF16, 512, HGRN_NB)
            yp = _outproj_even(yp, ma, mb, w_out, tm_p)
            k_p.append(jnp.transpose(k32.reshape(bp, H_A, HD_A, tp), (0, 3, 1, 2)))
            v_p.append(jnp.transpose(v32.reshape(bp, H_A, HD_A, tp), (0, 3, 1, 2)))
            hg_p.append(hs)
            q, k, v, k32, v32, ga, qb, fb, ib, gb = _inproj_even(ys, ng, w_in, qg, kg, grp, F32, tm_s)
            ma = _sb_sample(q, k, v, ga, jnp.repeat(bias * LOG2E, ts).reshape(H_A * ts, 1), _ntri(page_size),
                            jnp.transpose(cache_k[e], (0, 2, 3, 1)), jnp.transpose(cache_v[e], (0, 2, 3, 1)),
                            page_table, ts, SB_PAGES_PER_STEP)
            mb, hs = _hgrn(qb, fb, ib, gb, state_hgrn[e].astype(F32), hgrn_lower_bounds, og, bs, ts, e, F32, ts,
                           HGRN_NB)
            ys = _outproj_even(ys, ma, mb, w_out, tm_s)
            k_s.append(k32.reshape(bs, ts, H_A, HD_A))
            v_s.append(v32.reshape(bs, ts, H_A, HD_A))
            hg_s.append(hs)
        else:
            w_in = w_in_odd[e].astype(BF16)
            w_out = w_out_odd[e].astype(BF16)
            q, k, v, g = _inproj_odd(yp, ng, w_in, cos_p, sin_p, BF16, tm_p)
            m, s = _ret(q, k, v, g, jnp.zeros((bp, H_C, DK_C, DV_C), F32), bp, tp, BF16, 512, RET_NB, RET_HPG)
            yp = _outproj_odd(yp, m, w_out, tm_p)
            rt_p.append(s)
            q, k, v, g = _inproj_odd(ys, ng, w_in, cos_s, sin_s, F32, tm_s)
            m, s = _ret(q, k, v, g, state_ret[e].astype(F32), bs, ts, F32, ts, RET_NB, RET_HPG)
            ys = _outproj_odd(ys, m, w_out, tm_s)
            rt_s.append(s)
    return (yp.reshape(bp, tp, d), ys.reshape(bs, ts, d), jnp.stack(k_p), jnp.stack(v_p), jnp.stack(k_s),
            jnp.stack(v_s), jnp.stack(hg_p), jnp.stack(hg_s), jnp.stack(rt_p), jnp.stack(rt_s))
```

```python
import functools
import math

import numpy as np
import jax
import jax.numpy as jnp
from jax import lax
from jax.experimental import pallas as pl
from jax.experimental.pallas import tpu as pltpu

F32 = jnp.float32
BF16 = jnp.bfloat16

H_A, HD_A = 8, 64
W_A = H_A * HD_A
H_B, DK_B, DV_B = 4, 128, 128
F_B = H_B * DK_B
W_B = H_B * DV_B
H_C, DK_C, DV_C = 4, 256, 512
QK_C = H_C * DK_C
W_C = H_C * DV_C
HGRN_CHUNK = 64
RET_CHUNK = 128
ROPE_BASE = 10000.0
NORM_EPS = 1e-6

LANES = 128
VMEM_LIMIT = 56 * 1024 * 1024
SB_TQ = 256
RET_NB, RET_HPG = 2, 2
HGRN_NB = 2
SB_RING = 3
SB_UNROLL = 4 * SB_RING
SB_PAGES_PER_STEP = 16
LOG2E = 1.4426950408889634


def _dot(a, b):
    return jnp.dot(a, b, preferred_element_type=F32)


def _dot_nt(a, b):
    return lax.dot_general(a, b, (((1,), (1,)), ((), ())), preferred_element_type=F32)


def _dot_tn(a, b):
    return lax.dot_general(a, b, (((0,), (0,)), ((), ())), preferred_element_type=F32)


def _split2(x):
    hi = x.astype(BF16)
    lo = (x - hi.astype(F32)).astype(BF16)
    return hi, lo


def _split3(x):
    hi = x.astype(BF16)
    r = x - hi.astype(F32)
    mid = r.astype(BF16)
    lo = (r - mid.astype(F32)).astype(BF16)
    return hi, mid, lo


def _sigmoid(x):
    return 1.0 / (1.0 + jnp.exp(-x))


def _silu(x):
    return x * _sigmoid(x)


def _rms(x, eps=NORM_EPS):
    return x * lax.rsqrt(jnp.mean(x * x, axis=-1, keepdims=True) + eps)


def _params(sem):
    return pltpu.CompilerParams(dimension_semantics=sem, vmem_limit_bytes=VMEM_LIMIT)


def _inproj_even_kernel(x_ref, ng_ref, w_ref, qg_ref, kg_ref, grp_ref,
                        q_ref, k_ref, v_ref, k32_ref, v32_ref, ga_ref,
                        qb_ref, fb_ref, ib_ref, gb_ref, *, kv_token_minor):
    x = x_ref[...]
    h = (_rms(x) * ng_ref[...]).astype(BF16)

    def proj(c):
        return _dot(h, w_ref[:, c * W_A:(c + 1) * W_A])

    def head_norm(a, gain):
        hi, lo = _split2(a * a)
        ssum = _dot(hi, grp_ref[...]) + _dot(lo, grp_ref[...])
        return a * lax.rsqrt(ssum * (1.0 / HD_A) + NORM_EPS) * gain

    qa = head_norm(proj(0), qg_ref[...])
    q_ref[...] = (qa * (HD_A ** -0.5 * LOG2E)).astype(q_ref.dtype)
    ka = head_norm(proj(1), kg_ref[...])
    k32_ref[...] = ka.T if kv_token_minor else ka
    k_ref[...] = ka.astype(k_ref.dtype)
    va = proj(2)
    v32_ref[...] = va.T if kv_token_minor else va
    v_ref[...] = va.astype(v_ref.dtype)
    ga_ref[...] = _silu(proj(3))
    qb_ref[...] = proj(4)
    fb_ref[...] = proj(5)
    ib_ref[...] = proj(6)
    gb_ref[...] = _silu(proj(7))


def _inproj_even(x, ng, w, qg, kg, grp, act_dtype, tm, kv_seq=None):
    n, d = x.shape
    row = lambda i: (i, 0)
    const = lambda i: (0, 0)
    blk = pl.BlockSpec((tm, W_A), row)
    outs = [jax.ShapeDtypeStruct((n, W_A), act_dtype)] * 3 + [jax.ShapeDtypeStruct((n, W_A), F32)] * 7
    out_specs = [blk] * 10
    if kv_seq is not None:
        nt = kv_seq // tm
        kv_blk = pl.BlockSpec((None, W_A, tm), lambda i: (i // nt, 0, i % nt))
        kv_shape = jax.ShapeDtypeStruct((n // kv_seq, W_A, kv_seq), F32)
        outs[3:5] = [kv_shape, kv_shape]
        out_specs[3:5] = [kv_blk, kv_blk]
    return pl.pallas_call(
        functools.partial(_inproj_even_kernel, kv_token_minor=kv_seq is not None),
        grid=(n // tm,),
        in_specs=[pl.BlockSpec((tm, d), row), pl.BlockSpec((1, d), const),
                  pl.BlockSpec(w.shape, const, pipeline_mode=pl.Buffered(1)), pl.BlockSpec((1, W_A), const),
                  pl.BlockSpec((1, W_A), const), pl.BlockSpec((W_A, W_A), const)],
        out_specs=out_specs,
        out_shape=outs,
        compiler_params=_params(("parallel",)),
        name="inproj_even",
    )(x, ng, w, qg, kg, grp)


def _sb_nl(z, mask):
    neg_abs = lax.bitcast_convert_type(lax.bitcast_convert_type(z, jnp.uint32) | jnp.uint32(0x80000000), F32)
    nl = jnp.maximum(z, 0.0) + jnp.log(1.0 + jnp.exp2(neg_abs)) * LOG2E
    if mask is not None:
        nl = jnp.where(mask, nl, 0.0)
    return nl


def _sb_incl(nl, ntri):
    return _dot(nl.astype(BF16), ntri)


def _sb_prompt_kernel(bias_ref, q_ref, k_ref, v_ref, ga_ref, ntri_ref, o_ref, qs_ref, acc_ref, carry_ref,
                      z_ref, zi_ref, col_ref, *, tq, nq):
    hp = pl.program_id(1)
    rows = 2 * tq
    lane = lax.broadcasted_iota(jnp.int32, (tq, LANES), 1)
    b0 = bias_ref[2 * hp] * LOG2E
    b1 = bias_ref[2 * hp + 1] * LOG2E

    def blk(i):
        return pl.ds(pl.multiple_of(i * tq, tq), tq)

    def logits(qs, kb):
        zr = _dot_nt(qs, k_ref[blk(kb), :])
        return jnp.concatenate([zr[:tq] + b0, zr[tq:] + b1], axis=0)

    r2 = lax.broadcasted_iota(jnp.int32, (rows, tq), 0)
    c2 = lax.broadcasted_iota(jnp.int32, (rows, tq), 1)
    mask = c2 < jnp.where(r2 >= tq, r2 - tq, r2)

    def stage_a(p, slot, diag):
        if diag:
            q = q_ref[blk(p[0]), :].astype(F32)
            qs = jnp.concatenate([jnp.where(lane < HD_A, q, 0.0), jnp.where(lane >= HD_A, q, 0.0)],
                                 axis=0).astype(BF16)
            qs_ref[p[0]] = qs
        else:
            qs = qs_ref[p[0]]
        z_ref[slot] = logits(qs, p[1])

    def stage_b(slot, diag):
        z = z_ref[slot]
        incl = _sb_incl(_sb_nl(z, mask if diag else None), ntri_ref[...])
        zi_ref[slot] = z + incl
        col_ref[slot] = incl[:, 0:1]

    def stage_c(p, slot, diag):
        if diag:
            w = jnp.where(mask, jnp.exp2(zi_ref[slot]), 0.0)
            acc_ref[p[0]] = _dot(w.astype(BF16), v_ref[blk(p[1]), :])
            carry_ref[p[0]] = col_ref[slot]
        else:
            carry = carry_ref[p[0]]
            w = jnp.exp2(zi_ref[slot] + carry)
            acc_ref[p[0]] += _dot(w.astype(BF16), v_ref[blk(p[1]), :])
            carry_ref[p[0]] = carry + col_ref[slot]

    def pipeline(n_steps, p0, nxt, diag):
        if n_steps == 0:
            return
        stage_a(p0, 0, diag)
        stage_b(0, diag)
        if n_steps == 1:
            stage_c(p0, 0, diag)
            return
        stage_a(nxt(p0), 1, diag)
        n_steady = n_steps - 2

        def step(t, pc, pa):
            stage_c(pc, t % SB_RING, diag)
            stage_b((t + 1) % SB_RING, diag)
            stage_a(pa, (t + 2) % SB_RING, diag)
            return nxt(pc), nxt(pa)

        def body(_, st):
            pc, pa = st[:2], st[2:]
            for u in range(SB_UNROLL):
                pc, pa = step(u, pc, pa)
            return (*pc, *pa)

        n_trips = n_steady // SB_UNROLL
        st = lax.fori_loop(0, n_trips, body, (*p0, *nxt(nxt(p0))))
        pc, pa = st[:2], st[2:]
        for t in range(n_trips * SB_UNROLL, n_steady):
            pc, pa = step(t, pc, pa)
        stage_c(pc, n_steady % SB_RING, diag)
        stage_b((n_steady + 1) % SB_RING, diag)
        stage_c(nxt(pc), (n_steady + 1) % SB_RING, diag)

    def next_off_diagonal(p):
        wrap = p[1] == 0
        return jnp.where(wrap, p[0] + 1, p[0]), jnp.where(wrap, p[0], p[1] - 1)

    pipeline(nq, (jnp.int32(0), jnp.int32(0)), lambda p: (p[0] + 1, p[1] + 1), True)
    pipeline(nq * (nq - 1) // 2, (jnp.int32(1), jnp.int32(0)), next_off_diagonal, False)

    def finish(qi, _):
        acc = acc_ref[qi]
        o = jnp.where(lane < HD_A, acc[:tq], acc[tq:]) * ga_ref[blk(qi), :]
        o_ref[blk(qi), :] = o.astype(o_ref.dtype)
        return 0

    lax.fori_loop(0, nq, finish, 0)


def _sb_prompt(q, k, v, ga, bias, ntri, batch, seq, tq):
    n = q.shape[0]
    nq = seq // tq
    seqmap = lambda b, hp: (b, hp)
    blk = pl.BlockSpec((seq, LANES), seqmap)
    return pl.pallas_call(
        functools.partial(_sb_prompt_kernel, tq=tq, nq=nq),
        grid=(batch, H_A // 2),
        in_specs=[pl.BlockSpec(memory_space=pltpu.SMEM), blk, blk, blk, blk,
                  pl.BlockSpec((tq, tq), lambda b, hp: (0, 0))],
        out_specs=blk,
        out_shape=jax.ShapeDtypeStruct((n, W_A), BF16),
        scratch_shapes=[pltpu.VMEM((nq, 2 * tq, LANES), BF16), pltpu.VMEM((nq, 2 * tq, LANES), F32),
                        pltpu.VMEM((nq, 2 * tq, 1), F32),
                        pltpu.VMEM((SB_RING, 2 * tq, tq), F32), pltpu.VMEM((SB_RING, 2 * tq, tq), F32),
                        pltpu.VMEM((SB_RING, 2 * tq, 1), F32)],
        compiler_params=_params(("parallel", "parallel")),
        name="sb_prompt",
    )(bias, q, k, v, ga, ntri)


def _sb_sample_kernel(pt_ref, q_ref, kn_ref, vn_ref, ga_ref, bias_ref, ntri_ref, *rest, ts, pages):
    page_refs = rest[:2 * pages]
    o_ref, acc_ref, carry_ref = rest[2 * pages:]
    s = pl.program_id(1)
    rows = H_A * ts
    ps = ntri_ref.shape[0]
    q = q_ref[...]
    qpair = [jnp.concatenate([q[:, (2 * p) * HD_A:(2 * p + 1) * HD_A], q[:, (2 * p + 1) * HD_A:(2 * p + 2) * HD_A]],
                             axis=0).astype(BF16) for p in range(H_A // 2)]
    bias = bias_ref[...]
    ntri = ntri_ref[...]

    def own_rows(x, h):
        return x[(h % 2) * ts:(h % 2 + 1) * ts]

    def sweep(n, qk, pv, carry, mask):
        z = jnp.concatenate(
            [jnp.concatenate([own_rows(qk(j, h, qpair[h // 2]), h) for h in range(H_A)], axis=0)
             for j in range(n)], axis=1) + bias
        nl = _sb_nl(z, mask)
        incls = [_sb_incl(nl[:, j * ps:(j + 1) * ps], ntri) for j in range(n)]
        sums = []
        for j in range(n):
            sums.append(carry + incls[j])
            carry = carry + incls[j][:, 0:1]
        w = jnp.exp2(z + jnp.concatenate(sums, axis=1))
        if mask is not None:
            w = jnp.where(mask, w, 0.0)
        w = w.astype(BF16)
        parts = [jnp.concatenate([own_rows(pv(j, h, w[(h // 2) * 2 * ts:(h // 2 + 1) * 2 * ts, j * ps:(j + 1) * ps]), h)
                                  for h in range(H_A)], axis=0) for j in range(n)]
        contrib = parts[0]
        for c in parts[1:]:
            contrib = contrib + c
        return contrib, carry

    @pl.when(s == 0)
    def _():
        pad = jnp.zeros((ps - ts, HD_A), F32)
        kn = kn_ref[...]
        vn = vn_ref[...]
        new_k = lambda h: jnp.concatenate([kn[:, h * HD_A:(h + 1) * HD_A], pad], axis=0).astype(BF16)
        new_v = lambda h: jnp.concatenate([vn[:, h * HD_A:(h + 1) * HD_A], pad], axis=0).astype(BF16)
        r2 = lax.broadcasted_iota(jnp.int32, (rows, ps), 0)
        c2 = lax.broadcasted_iota(jnp.int32, (rows, ps), 1)
        mask = c2 < lax.rem(r2, ts)
        contrib, carry = sweep(1, lambda j, h, q16: _dot_nt(q16, new_k(h)), lambda j, h, w16: _dot(w16, new_v(h)),
                               jnp.zeros((rows, 1), F32), mask)
        acc_ref[...] = contrib
        carry_ref[...] = carry

    contrib, carry = sweep(pages,
                           lambda j, h, q16: _dot(q16, page_refs[2 * j][h].astype(BF16)),
                           lambda j, h, w16: _dot_nt(w16, page_refs[2 * j + 1][h].astype(BF16)),
                           carry_ref[...], None)
    acc_ref[...] += contrib
    carry_ref[...] = carry

    @pl.when(s == pl.num_programs(1) - 1)
    def _():
        acc = acc_ref[...]
        o = jnp.concatenate([acc[h * ts:(h + 1) * ts] for h in range(H_A)], axis=-1)
        o_ref[...] = (o * ga_ref[...]).astype(o_ref.dtype)


def _sb_sample(q, kn, vn, ga, bias_rows, ntri, cache_k, cache_v, page_table, ts, pages):
    n = q.shape[0]
    nb, n_pages = page_table.shape
    ps = cache_k.shape[3]
    steps = n_pages // pages
    tok = lambda b, s, pt: (b, 0)
    const = lambda b, s, pt: (0, 0)

    def page_map(j):
        return lambda b, s, pt: (pt[b, n_pages - 1 - (s * pages + j)], 0, 0, 0)

    page_specs, page_args = [], []
    for j in range(pages):
        page_specs += [pl.BlockSpec((None, H_A, HD_A, ps), page_map(j))] * 2
        page_args += [cache_k, cache_v]
    rows = H_A * ts
    grid_spec = pltpu.PrefetchScalarGridSpec(
        num_scalar_prefetch=1,
        grid=(nb, steps),
        in_specs=[pl.BlockSpec((ts, W_A), tok)] * 4
                 + [pl.BlockSpec((rows, 1), const), pl.BlockSpec((ps, ps), const)] + page_specs,
        out_specs=pl.BlockSpec((ts, W_A), tok),
        scratch_shapes=[pltpu.VMEM((rows, HD_A), F32), pltpu.VMEM((rows, 1), F32)],
    )
    return pl.pallas_call(
        functools.partial(_sb_sample_kernel, ts=ts, pages=pages),
        grid_spec=grid_spec,
        out_shape=jax.ShapeDtypeStruct((n, W_A), F32),
        compiler_params=_params(("parallel", "arbitrary")),
        name="sb_sample",
    )(page_table, q, kn, vn, ga, bias_rows, ntri, *page_args)


def _hgrn_tables(c, n_valid):
    levels = int(math.log2(c))
    lm = np.zeros((levels, c, c), np.float32)
    r = np.arange(c)
    for l in range(levels):
        m = 2 ** l
        second = (r // m) % 2 == 1
        same = (r[:, None] // (2 * m)) == (r[None, :] // (2 * m))
        lm[l] = (same & second[:, None] & (~second)[None, :]).astype(np.float32)
    a = np.tril(np.ones((c, c), np.float32))
    a[:, n_valid:] = 0.0
    return a, lm


def _hold_mid(b, m, rowi):
    c = b.shape[0]
    if m == 1:
        return jnp.where(jnp.bitwise_and(rowi, 1) == 1, pltpu.roll(b, 1, 0), b)
    if m == 2:
        lo = jnp.concatenate([jnp.broadcast_to(b[g + 1:g + 2], (8, LANES)) for g in range(0, c, 8)], axis=0)
        hi = jnp.concatenate([jnp.broadcast_to(b[g + 5:g + 6], (8, LANES)) for g in range(0, c, 8)], axis=0)
        return jnp.where(jnp.bitwise_and(rowi, 4) == 0, lo, hi)
    return jnp.concatenate([jnp.broadcast_to(b[g + m - 1:g + m], (2 * m, LANES)) for g in range(0, c, 2 * m)], axis=0)


def _hgrn_kernel(q_ref, f_ref, i_ref, g_ref, s0_ref, lbp_ref, og_ref, a_ref, lm_ref,
                 o_ref, sout_ref, st_ref, *, c, n_chunks, n_valid, layer, nb):
    t = pl.program_id(1)
    levels = lm_ref.shape[0]
    chains = [(bi, h) for bi in range(nb) for h in range(H_B)]

    @pl.when(t == 0)
    def _():
        for bi, h in chains:
            st_ref[bi, h] = s0_ref[bi, h].T

    lbp = lbp_ref[...]
    p = jnp.exp(lbp - jnp.max(lbp, axis=0, keepdims=True))
    lb = jnp.sum(p[:layer + 1], axis=0, keepdims=True) / jnp.sum(p, axis=0, keepdims=True)
    og = og_ref[...]
    rowi = lax.broadcasted_iota(jnp.int32, (c, 1), 0)
    valid = rowi < n_valid

    def load(ref, bi, ci):
        if n_valid < c:
            x = ref[bi]
            return jnp.concatenate([x, jnp.zeros((c - x.shape[0], x.shape[1]), F32)], axis=0)
        return ref[bi, pl.ds(pl.multiple_of(ci * c, c), c), :]

    def chunk(ci, _):
        gs, kks, qqs, vvs = [], [], [], []
        for bi in range(nb):
            xf = load(f_ref, bi, ci)
            e = jnp.exp(-jnp.abs(xf))
            r = 1.0 / (1.0 + e)
            sig_pos = jnp.where(xf >= 0, r, e * r)
            sig_neg = jnp.where(xf >= 0, e * r, r)
            g = jnp.log(lb + (1.0 - lb) * sig_pos)
            kk = (1.0 - lb) * sig_neg
            if n_valid < c:
                g = jnp.where(valid, g, 0.0)
                kk = jnp.where(valid, kk, 0.0)
            gs.append(g)
            kks.append(kk)
            qqs.append(_silu(load(q_ref, bi, ci)))
            vvs.append(load(i_ref, bi, ci))
        ghi, gmid, glo = _split3(jnp.concatenate(gs, axis=1))
        a = a_ref[...]
        bcum_all = _dot(a, ghi) + _dot(a, gmid) + _dot(a, glo)
        outs = [[] for _ in range(nb)]
        for bi, h in chains:
            hl = slice(h * LANES, (h + 1) * LANES)
            el = slice((bi * H_B + h) * LANES, (bi * H_B + h + 1) * LANES)
            qq, kk, vv = qqs[bi][:, hl], kks[bi][:, hl], vvs[bi][:, hl]
            vv16 = vv.astype(BF16)
            bcum = bcum_all[:, el]
            rem = bcum[c - 1:c, :] - bcum
            st = st_ref[bi, h]
            o = _dot_nt((qq * jnp.exp(bcum)).astype(BF16), st.astype(BF16))
            scores = jnp.zeros((c, c), F32)
            for l in range(levels):
                second = jnp.bitwise_and(jnp.right_shift(rowi, l), 1) == 1
                d = bcum - _hold_mid(bcum, 2 ** l, rowi)
                x = jnp.exp(jnp.where(second, d, -d))
                qt = jnp.where(second, qq * x, 0.0).astype(BF16)
                kt = jnp.where(second, 0.0, kk * x).astype(BF16)
                scores = scores + _dot_nt(qt, kt) * lm_ref[l]
            o = o + _dot(scores.astype(BF16), vv16)
            o = o + jnp.sum(qq * kk, axis=-1, keepdims=True) * vv
            dec_last = jnp.exp(bcum[c - 1:c, :])
            st_ref[bi, h] = dec_last * st + _dot_tn(vv16, (kk * jnp.exp(rem)).astype(BF16))
            outs[bi].append(_rms(o) * og)
        for bi in range(nb):
            ob = jnp.concatenate(outs[bi], axis=1) * load(g_ref, bi, ci)
            if n_valid < c:
                o_ref[bi] = ob[:n_valid].astype(o_ref.dtype)
            else:
                o_ref[bi, pl.ds(pl.multiple_of(ci * c, c), c), :] = ob.astype(o_ref.dtype)
        return 0

    lax.fori_loop(0, n_chunks, chunk, 0)

    @pl.when(t == pl.num_programs(1) - 1)
    def _():
        for bi, h in chains:
            sout_ref[bi, h] = st_ref[bi, h].T


def _hgrn(qb, fb, ib, gb, s0, lbp, og, batch, seq, layer, out_dtype, tb, nb):
    n = qb.shape[0]
    c = HGRN_CHUNK
    if seq >= c:
        n_valid, rows_blk, nt, n_chunks = c, tb, seq // tb, tb // c
    else:
        n_valid, rows_blk, nt, n_chunks = seq, seq, 1, 1
    a_np, lm_np = _hgrn_tables(c, n_valid)
    a = jnp.asarray(a_np, BF16)
    lm = jnp.asarray(lm_np, F32)
    tok = lambda b, t: (b, t, 0)
    st = lambda b, t: (b, 0, 0, 0)
    blk = pl.BlockSpec((nb, rows_blk, W_B), tok)
    st_blk = pl.BlockSpec((nb, H_B, DK_B, DV_B), st)
    r3 = lambda x: x.reshape(batch, seq, W_B)
    mix, s_out = pl.pallas_call(
        functools.partial(_hgrn_kernel, c=c, n_chunks=n_chunks, n_valid=n_valid, layer=layer, nb=nb),
        grid=(batch // nb, nt),
        in_specs=[blk, blk, blk, blk, st_blk,
                  pl.BlockSpec(lbp.shape, lambda b, t: (0, 0)),
                  pl.BlockSpec((1, LANES), lambda b, t: (0, 0)),
                  pl.BlockSpec(a.shape, lambda b, t: (0, 0)),
                  pl.BlockSpec(lm.shape, lambda b, t: (0, 0, 0))],
        out_specs=[blk, st_blk],
        out_shape=[jax.ShapeDtypeStruct((batch, seq, W_B), out_dtype),
                   jax.ShapeDtypeStruct((batch, H_B, DK_B, DV_B), F32)],
        scratch_shapes=[pltpu.VMEM((nb, H_B, DV_B, DK_B), F32)],
        compiler_params=_params(("parallel", "arbitrary")),
        name="hgrn",
    )(r3(qb), r3(fb), r3(ib), r3(gb), s0, lbp, og, a, lm)
    return mix.reshape(n, W_B), s_out


def _outproj_even_kernel(x_ref, ma_ref, mb_ref, w_ref, y_ref):
    y = x_ref[...] + _dot(ma_ref[...].astype(BF16), w_ref[:W_A, :]) + _dot(mb_ref[...].astype(BF16), w_ref[W_A:, :])
    y_ref[...] = y


def _outproj_even(x, ma, mb, w, tm):
    n, d = x.shape
    row = lambda i: (i, 0)
    return pl.pallas_call(
        _outproj_even_kernel,
        grid=(n // tm,),
        in_specs=[pl.BlockSpec((tm, d), row), pl.BlockSpec((tm, W_A), row), pl.BlockSpec((tm, W_B), row),
                  pl.BlockSpec(w.shape, lambda i: (0, 0))],
        out_specs=pl.BlockSpec((tm, d), row),
        out_shape=jax.ShapeDtypeStruct((n, d), F32),
        compiler_params=_params(("parallel",)),
        name="outproj_even",
    )(x, ma, mb, w)


def _inproj_odd_kernel(x_ref, ng_ref, w_ref, cos_ref, sin_ref, q_ref, k_ref, v_ref, g_ref):
    x = x_ref[...]
    h = (_rms(x) * ng_ref[...]).astype(BF16)
    cos = cos_ref[...]
    sin = sin_ref[...]
    half = DK_C // 2

    def rot(a, scale):
        outs = []
        for hd in range(H_C):
            x1 = a[:, hd * DK_C:hd * DK_C + half]
            x2 = a[:, hd * DK_C + half:(hd + 1) * DK_C]
            outs += [(x1 * cos - x2 * sin) * scale, (x2 * cos + x1 * sin) * scale]
        return jnp.concatenate(outs, axis=-1)

    q_ref[...] = rot(_dot(h, w_ref[:, :QK_C]), 1.0).astype(q_ref.dtype)
    k_ref[...] = rot(_dot(h, w_ref[:, QK_C:2 * QK_C]), DK_C ** -0.5)
    for j in range(2):
        lo = 2 * QK_C + j * QK_C
        v_ref[:, j * QK_C:(j + 1) * QK_C] = _dot(h, w_ref[:, lo:lo + QK_C]).astype(v_ref.dtype)
    for j in range(2):
        lo = 2 * QK_C + W_C + j * QK_C
        g_ref[:, j * QK_C:(j + 1) * QK_C] = _silu(_dot(h, w_ref[:, lo:lo + QK_C]))


def _inproj_odd(x, ng, w, cos, sin, act_dtype, tm):
    n, d = x.shape
    npos = cos.shape[0] // tm
    row = lambda i: (i, 0)
    const = lambda i: (0, 0)
    pos = lambda i: (i % npos, 0)
    return pl.pallas_call(
        _inproj_odd_kernel,
        grid=(n // tm,),
        in_specs=[pl.BlockSpec((tm, d), row), pl.BlockSpec((1, d), const),
                  pl.BlockSpec(w.shape, const, pipeline_mode=pl.Buffered(1)),
                  pl.BlockSpec((tm, DK_C // 2), pos), pl.BlockSpec((tm, DK_C // 2), pos)],
        out_specs=[pl.BlockSpec((tm, QK_C), row), pl.BlockSpec((tm, QK_C), row),
                   pl.BlockSpec((tm, W_C), row), pl.BlockSpec((tm, W_C), row)],
        out_shape=[jax.ShapeDtypeStruct((n, QK_C), act_dtype), jax.ShapeDtypeStruct((n, QK_C), F32),
                   jax.ShapeDtypeStruct((n, W_C), act_dtype), jax.ShapeDtypeStruct((n, W_C), F32)],
        compiler_params=_params(("parallel",)),
        name="inproj_odd",
    )(x, ng, w, cos, sin)


def _ret_kernel(q_ref, k_ref, v_ref, g_ref, s0_ref, idec_ref, qdec_ref, kdec_ref, cdec_ref,
                o_ref, sout_ref, s_ref, *, c, n_chunks, n_valid, nb, hpg):
    t = pl.program_id(2)
    chains = [(bi, hh) for bi in range(nb) for hh in range(hpg)]

    @pl.when(t == 0)
    def _():
        s_ref[...] = s0_ref[...]

    def load(ref, bi, hh, width, ci):
        cols = slice(hh * width, (hh + 1) * width)
        if n_valid < c:
            x = ref[bi, :, cols].astype(F32)
            return jnp.concatenate([x, jnp.zeros((c - x.shape[0], x.shape[1]), F32)], axis=0)
        return ref[bi, pl.ds(pl.multiple_of(ci * c, c), c), cols]

    def chunk(ci, _):
        for bi, hh in chains:
            qc = load(q_ref, bi, hh, DK_C, ci).astype(BF16)
            kc = load(k_ref, bi, hh, DK_C, ci)
            vc = load(v_ref, bi, hh, DV_C, ci).astype(BF16)
            s = s_ref[bi, hh]
            scores = _dot_nt(qc, kc.astype(BF16)) * idec_ref[hh]
            o = _dot(scores.astype(BF16), vc) + _dot(qc, s.astype(BF16)) * qdec_ref[hh, :, 0:1]
            s_ref[bi, hh] = cdec_ref[hh, 0:1, 0:1] * s + _dot_tn((kc * kdec_ref[hh, :, 0:1]).astype(BF16), vc)
            ob = _rms(o) * load(g_ref, bi, hh, DV_C, ci)
            cols = slice(hh * DV_C, (hh + 1) * DV_C)
            if n_valid < c:
                o_ref[bi, :, cols] = ob[:n_valid].astype(o_ref.dtype)
            else:
                o_ref[bi, pl.ds(pl.multiple_of(ci * c, c), c), cols] = ob.astype(o_ref.dtype)
        return 0

    lax.fori_loop(0, n_chunks, chunk, 0)

    @pl.when(t == pl.num_programs(2) - 1)
    def _():
        sout_ref[...] = s_ref[...]


def _ret_tables(c, chunk):
    f32 = jnp.float32
    log_gamma = jnp.log1p(-jnp.exp2(-5.0 - jnp.arange(H_C, dtype=f32)))
    idx = jnp.arange(c, dtype=f32)
    real = idx < chunk
    rel = idx[:, None] - idx[None, :]
    ok = (rel >= 0) & real[:, None] & real[None, :]
    idec = jnp.exp(jnp.where(ok[None], rel[None] * log_gamma[:, None, None], -jnp.inf))
    qdec = jnp.where(real[None, :], jnp.exp((idx[None, :] + 1.0) * log_gamma[:, None]), 0.0)
    kdec = jnp.where(real[None, :], jnp.exp((chunk - 1.0 - idx[None, :]) * log_gamma[:, None]), 0.0)
    cdec = jnp.exp(chunk * log_gamma)
    bc = lambda x: jnp.broadcast_to(x[:, :, None], (H_C, c, LANES))
    return idec, bc(qdec), bc(kdec), jnp.broadcast_to(cdec[:, None, None], (H_C, 8, LANES))


def _ret(q, k, v, g, s0, batch, seq, out_dtype, tb, nb, hpg):
    n = q.shape[0]
    c = RET_CHUNK
    if seq >= c:
        n_valid, rows_blk, nt, n_chunks = c, tb, seq // tb, tb // c
    else:
        n_valid, rows_blk, nt, n_chunks = seq, seq, 1, 1
    idec, qdec, kdec, cdec = _ret_tables(c, n_valid)
    tok = lambda hg, b, t: (b, t, hg)
    st = lambda hg, b, t: (b, hg, 0, 0)
    hd = lambda hg, b, t: (hg, 0, 0)
    qk_blk = pl.BlockSpec((nb, rows_blk, hpg * DK_C), tok)
    vg_blk = pl.BlockSpec((nb, rows_blk, hpg * DV_C), tok)
    st_blk = pl.BlockSpec((nb, hpg, DK_C, DV_C), st)
    r3 = lambda x: x.reshape(batch, seq, x.shape[-1])
    mix, s_out = pl.pallas_call(
        functools.partial(_ret_kernel, c=c, n_chunks=n_chunks, n_valid=n_valid, nb=nb, hpg=hpg),
        grid=(H_C // hpg, batch // nb, nt),
        in_specs=[qk_blk, qk_blk, vg_blk, vg_blk, st_blk,
                  pl.BlockSpec((hpg, c, c), hd), pl.BlockSpec((hpg, c, LANES), hd),
                  pl.BlockSpec((hpg, c, LANES), hd), pl.BlockSpec((hpg, 8, LANES), hd)],
        out_specs=[vg_blk, st_blk],
        out_shape=[jax.ShapeDtypeStruct((batch, seq, W_C), out_dtype),
                   jax.ShapeDtypeStruct((batch, H_C, DK_C, DV_C), F32)],
        scratch_shapes=[pltpu.VMEM((nb, hpg, DK_C, DV_C), F32)],
        compiler_params=_params(("parallel", "parallel", "arbitrary")),
        name="retention",
    )(r3(q), r3(k), r3(v), r3(g), s0, idec, qdec, kdec, cdec)
    return mix.reshape(n, W_C), s_out


def _outproj_odd_kernel(x_ref, m_ref, w_ref, y_ref):
    y_ref[...] = x_ref[...] + _dot(m_ref[...].astype(BF16), w_ref[...])


def _outproj_odd(x, m, w, tm):
    n, d = x.shape
    row = lambda i: (i, 0)
    return pl.pallas_call(
        _outproj_odd_kernel,
        grid=(n // tm,),
        in_specs=[pl.BlockSpec((tm, d), row), pl.BlockSpec((tm, W_C), row), pl.BlockSpec(w.shape, lambda i: (0, 0))],
        out_specs=pl.BlockSpec((tm, d), row),
        out_shape=jax.ShapeDtypeStruct((n, d), F32),
        compiler_params=_params(("parallel",)),
        name="outproj_odd",
    )(x, m, w)


def _rope_tables(pos):
    half = DK_C // 2
    inv = 1.0 / (ROPE_BASE ** jnp.linspace(0.0, 1.0, half, dtype=F32))
    ang = pos[:, None] * inv[None, :]
    return jnp.cos(ang), jnp.sin(ang)


def _ntri(n):
    return jnp.asarray(-np.tril(np.ones((n, n), np.float32)), BF16)


def kernel(x_prompt, x_sample, cache_k, cache_v, page_table, state_hgrn, state_ret, norm_g,
           w_in_even, w_out_even, q_norm_g, k_norm_g, sb_logit_bias, hgrn_lower_bounds,
           hgrn_out_norm_g, w_in_odd, w_out_odd):
    bp, tp, d = x_prompt.shape
    bs, ts, _ = x_sample.shape
    n_pool, page_size = cache_k.shape[1], cache_k.shape[2]
    past_len = page_table.shape[1] * page_size
    depth = norm_g.shape[0]
    tm_p = 512
    tm_o = 512
    tm_s = bs * ts

    yp = x_prompt.reshape(bp * tp, d)
    ys = x_sample.reshape(bs * ts, d)
    grp = jnp.asarray(np.kron(np.eye(H_A, dtype=np.float32), np.ones((HD_A, HD_A), np.float32)), BF16)
    cos_p, sin_p = _rope_tables(jnp.arange(tp, dtype=F32))
    cos_s, sin_s = _rope_tables(past_len + jnp.arange(ts, dtype=F32))
    cos_s, sin_s = jnp.tile(cos_s, (bs, 1)), jnp.tile(sin_s, (bs, 1))

    k_p, v_p, k_s, v_s, hg_p, hg_s, rt_p, rt_s = [], [], [], [], [], [], [], []
    for layer in range(depth):
        e = layer // 2
        ng = norm_g[layer].reshape(1, d)
        if layer % 2 == 0:
            w_in = w_in_even[e].astype(BF16)
            w_out = w_out_even[e].astype(BF16)
            qg = jnp.tile(q_norm_g[e], H_A).reshape(1, W_A)
            kg = jnp.tile(k_norm_g[e], H_A).reshape(1, W_A)
            og = hgrn_out_norm_g[e].reshape(1, DV_B)
            bias = sb_logit_bias[e].astype(F32)
            q, k, v, k32, v32, ga, qb, fb, ib, gb = _inproj_even(yp, ng, w_in, qg, kg, grp, BF16, tm_p, kv_seq=tp)
            ma = _sb_prompt(q, k, v, ga, bias, _ntri(SB_TQ), bp, tp, SB_TQ)
            mb, hs = _hgrn(qb, fb, ib, gb, jnp.zeros((bp, H_B, DK_B, DV_B), F32), hgrn_lower_bounds,
                           og, bp, tp, e, BF16, 512, HGRN_NB)
            yp = _outproj_even(yp, ma, mb, w_out, tm_o)
            k_p.append(jnp.transpose(k32.reshape(bp, H_A, HD_A, tp), (0, 3, 1, 2)))
            v_p.append(jnp.transpose(v32.reshape(bp, H_A, HD_A, tp), (0, 3, 1, 2)))
            hg_p.append(hs)
            q, k, v, k32, v32, ga, qb, fb, ib, gb = _inproj_even(ys, ng, w_in, qg, kg, grp, F32, tm_s)
            ma = _sb_sample(q, k, v, ga, jnp.repeat(bias * LOG2E, ts).reshape(H_A * ts, 1), _ntri(page_size),
                            jnp.transpose(cache_k[e], (0, 2, 3, 1)), jnp.transpose(cache_v[e], (0, 2, 3, 1)),
                            page_table, ts, SB_PAGES_PER_STEP)
            mb, hs = _hgrn(qb, fb, ib, gb, state_hgrn[e].astype(F32), hgrn_lower_bounds, og, bs, ts, e, F32, ts,
                           HGRN_NB)
            ys = _outproj_even(ys, ma, mb, w_out, tm_s)
            k_s.append(k32.reshape(bs, ts, H_A, HD_A))
            v_s.append(v32.reshape(bs, ts, H_A, HD_A))
            hg_s.append(hs)
        else:
            w_in = w_in_odd[e].astype(BF16)
            w_out = w_out_odd[e].astype(BF16)
            q, k, v, g = _inproj_odd(yp, ng, w_in, cos_p, sin_p, BF16, tm_p)
            m, s = _ret(q, k, v, g, jnp.zeros((bp, H_C, DK_C, DV_C), F32), bp, tp, BF16, 512, RET_NB, RET_HPG)
            yp = _outproj_odd(yp, m, w_out, tm_o)
            rt_p.append(s)
            q, k, v, g = _inproj_odd(ys, ng, w_in, cos_s, sin_s, F32, tm_s)
            m, s = _ret(q, k, v, g, state_ret[e].astype(F32), bs, ts, F32, ts, RET_NB, RET_HPG)
            ys = _outproj_odd(ys, m, w_out, tm_s)
            rt_s.append(s)
    return (yp.reshape(bp, tp, d), ys.reshape(bs, ts, d), jnp.stack(k_p), jnp.stack(v_p), jnp.stack(k_s),
            jnp.stack(v_s), jnp.stack(hg_p), jnp.stack(hg_s), jnp.stack(rt_p), jnp.stack(rt_s))
```

```python
import functools
import math

import numpy as np
import jax
import jax.numpy as jnp
from jax import lax
from jax.experimental import pallas as pl
from jax.experimental.pallas import tpu as pltpu

F32 = jnp.float32
BF16 = jnp.bfloat16

H_A, HD_A = 8, 64
W_A = H_A * HD_A
H_B, DK_B, DV_B = 4, 128, 128
F_B = H_B * DK_B
W_B = H_B * DV_B
H_C, DK_C, DV_C = 4, 256, 512
QK_C = H_C * DK_C
W_C = H_C * DV_C
HGRN_CHUNK = 64
RET_CHUNK = 256
RET_PAD_CHUNK = 128
ROPE_BASE = 10000.0
NORM_EPS = 1e-6

LANES = 128
VMEM_LIMIT = 56 * 1024 * 1024
SB_TQ = 256
RET_NB, RET_HPG = 2, 2
HGRN_NB = 2
SB_RING = 3
SB_UNROLL = 4 * SB_RING
SB_PAGES_PER_STEP = 16
LOG2E = 1.4426950408889634


def _dot(a, b):
    return jnp.dot(a, b, preferred_element_type=F32)


def _dot_nt(a, b):
    return lax.dot_general(a, b, (((1,), (1,)), ((), ())), preferred_element_type=F32)


def _dot_tn(a, b):
    return lax.dot_general(a, b, (((0,), (0,)), ((), ())), preferred_element_type=F32)


def _split2(x):
    hi = x.astype(BF16)
    lo = (x - hi.astype(F32)).astype(BF16)
    return hi, lo


def _split3(x):
    hi = x.astype(BF16)
    r = x - hi.astype(F32)
    mid = r.astype(BF16)
    lo = (r - mid.astype(F32)).astype(BF16)
    return hi, mid, lo


def _sigmoid(x):
    return 1.0 / (1.0 + jnp.exp(-x))


def _silu(x):
    return x * _sigmoid(x)


def _rms(x, eps=NORM_EPS):
    return x * lax.rsqrt(jnp.mean(x * x, axis=-1, keepdims=True) + eps)


def _params(sem):
    return pltpu.CompilerParams(dimension_semantics=sem, vmem_limit_bytes=VMEM_LIMIT)


def _inproj_even_kernel(x_ref, ng_ref, w_ref, qg_ref, kg_ref, grp_ref,
                        q_ref, k_ref, v_ref, k32_ref, v32_ref, ga_ref,
                        qb_ref, fb_ref, ib_ref, gb_ref, *, kv_token_minor):
    x = x_ref[...]
    h = (_rms(x) * ng_ref[...]).astype(BF16)

    def proj(c):
        return _dot(h, w_ref[:, c * W_A:(c + 1) * W_A])

    def head_norm(a, gain):
        hi, lo = _split2(a * a)
        ssum = _dot(hi, grp_ref[...]) + _dot(lo, grp_ref[...])
        return a * lax.rsqrt(ssum * (1.0 / HD_A) + NORM_EPS) * gain

    qa = head_norm(proj(0), qg_ref[...])
    q_ref[...] = (qa * (HD_A ** -0.5 * LOG2E)).astype(q_ref.dtype)
    ka = head_norm(proj(1), kg_ref[...])
    k32_ref[...] = ka.T if kv_token_minor else ka
    k_ref[...] = ka.astype(k_ref.dtype)
    va = proj(2)
    v32_ref[...] = va.T if kv_token_minor else va
    v_ref[...] = va.astype(v_ref.dtype)
    ga_ref[...] = _silu(proj(3))
    qb_ref[...] = proj(4)
    fb_ref[...] = proj(5)
    ib_ref[...] = proj(6)
    gb_ref[...] = _silu(proj(7))


def _inproj_even(x, ng, w, qg, kg, grp, act_dtype, tm, kv_seq=None):
    n, d = x.shape
    row = lambda i: (i, 0)
    const = lambda i: (0, 0)
    blk = pl.BlockSpec((tm, W_A), row)
    outs = [jax.ShapeDtypeStruct((n, W_A), act_dtype)] * 3 + [jax.ShapeDtypeStruct((n, W_A), F32)] * 7
    out_specs = [blk] * 10
    if kv_seq is not None:
        nt = kv_seq // tm
        kv_blk = pl.BlockSpec((None, W_A, tm), lambda i: (i // nt, 0, i % nt))
        kv_shape = jax.ShapeDtypeStruct((n // kv_seq, W_A, kv_seq), F32)
        outs[3:5] = [kv_shape, kv_shape]
        out_specs[3:5] = [kv_blk, kv_blk]
    return pl.pallas_call(
        functools.partial(_inproj_even_kernel, kv_token_minor=kv_seq is not None),
        grid=(n // tm,),
        in_specs=[pl.BlockSpec((tm, d), row), pl.BlockSpec((1, d), const),
                  pl.BlockSpec(w.shape, const, pipeline_mode=pl.Buffered(1)), pl.BlockSpec((1, W_A), const),
                  pl.BlockSpec((1, W_A), const), pl.BlockSpec((W_A, W_A), const)],
        out_specs=out_specs,
        out_shape=outs,
        compiler_params=_params(("parallel",)),
        name="inproj_even",
    )(x, ng, w, qg, kg, grp)


def _sb_nl(z, mask):
    neg_abs = lax.bitcast_convert_type(lax.bitcast_convert_type(z, jnp.uint32) | jnp.uint32(0x80000000), F32)
    nl = jnp.maximum(z, 0.0) + jnp.log(1.0 + jnp.exp2(neg_abs)) * LOG2E
    if mask is not None:
        nl = jnp.where(mask, nl, 0.0)
    return nl


def _sb_incl(nl, ntri):
    return _dot(nl.astype(BF16), ntri)


def _sb_prompt_kernel(bias_ref, q_ref, k_ref, v_ref, ga_ref, ntri_ref, o_ref, qs_ref, acc_ref, carry_ref,
                      z_ref, zi_ref, col_ref, *, tq, nq):
    hp = pl.program_id(1)
    rows = 2 * tq
    lane = lax.broadcasted_iota(jnp.int32, (tq, LANES), 1)

    b0 = bias_ref[2 * hp] * LOG2E
    b1 = bias_ref[2 * hp + 1] * LOG2E

    def blk(i):
        return pl.ds(pl.multiple_of(i * tq, tq), tq)

    def logits(qs, kb):
        zr = _dot_nt(qs, k_ref[blk(kb), :])
        return jnp.concatenate([zr[:tq] + b0, zr[tq:] + b1], axis=0)

    r2 = lax.broadcasted_iota(jnp.int32, (rows, tq), 0)
    c2 = lax.broadcasted_iota(jnp.int32, (rows, tq), 1)
    mask = c2 < jnp.where(r2 >= tq, r2 - tq, r2)

    def stage_a(p, slot, diag):
        if diag:
            q = q_ref[blk(p[0]), :].astype(F32)
            qs = jnp.concatenate([jnp.where(lane < HD_A, q, 0.0), jnp.where(lane >= HD_A, q, 0.0)],
                                 axis=0).astype(BF16)
            qs_ref[p[0]] = qs
        else:
            qs = qs_ref[p[0]]
        z_ref[slot] = logits(qs, p[1])

    def stage_b(slot, diag):
        z = z_ref[slot]
        incl = _sb_incl(_sb_nl(z, mask if diag else None), ntri_ref[...])
        zi_ref[slot] = z + incl
        col_ref[slot] = incl[:, 0:1]

    def stage_c(p, slot, diag):
        def own_lanes(x):
            return jnp.where(lane < HD_A, x[:tq], x[tq:])

        if diag:
            w = jnp.where(mask, jnp.exp2(zi_ref[slot]), 0.0)
            acc_ref[p[0]] = own_lanes(_dot(w.astype(BF16), v_ref[blk(p[1]), :]))
            carry_ref[p[0]] = col_ref[slot]
        else:
            carry = carry_ref[p[0]]
            w = jnp.exp2(zi_ref[slot] + carry)
            acc_ref[p[0]] += own_lanes(_dot(w.astype(BF16), v_ref[blk(p[1]), :]))
            carry_ref[p[0]] = carry + col_ref[slot]

    def pipeline(n_steps, p0, nxt, diag):
        if n_steps == 0:
            return
        stage_a(p0, 0, diag)
        stage_b(0, diag)
        if n_steps == 1:
            stage_c(p0, 0, diag)
            return
        stage_a(nxt(p0), 1, diag)
        n_steady = n_steps - 2

        def step(t, pc, pa):
            stage_c(pc, t % SB_RING, diag)
            stage_b((t + 1) % SB_RING, diag)
            stage_a(pa, (t + 2) % SB_RING, diag)
            return nxt(pc), nxt(pa)

        def body(_, st):
            pc, pa = st[:2], st[2:]
            for u in range(SB_UNROLL):
                pc, pa = step(u, pc, pa)
            return (*pc, *pa)

        n_trips = n_steady // SB_UNROLL
        st = lax.fori_loop(0, n_trips, body, (*p0, *nxt(nxt(p0))))
        pc, pa = st[:2], st[2:]
        for t in range(n_trips * SB_UNROLL, n_steady):
            pc, pa = step(t, pc, pa)
        stage_c(pc, n_steady % SB_RING, diag)
        stage_b((n_steady + 1) % SB_RING, diag)
        stage_c(nxt(pc), (n_steady + 1) % SB_RING, diag)

    def next_off_diagonal(p):
        wrap = p[1] == 0
        return jnp.where(wrap, p[0] + 1, p[0]), jnp.where(wrap, p[0], p[1] - 1)

    pipeline(nq, (jnp.int32(0), jnp.int32(0)), lambda p: (p[0] + 1, p[1] + 1), True)
    pipeline(nq * (nq - 1) // 2, (jnp.int32(1), jnp.int32(0)), next_off_diagonal, False)

    def finish(qi, _):
        o_ref[blk(qi), :] = (acc_ref[qi] * ga_ref[blk(qi), :]).astype(o_ref.dtype)
        return 0

    lax.fori_loop(0, nq, finish, 0)


def _sb_prompt(q, k, v, ga, bias, ntri, batch, seq, tq):
    n = q.shape[0]
    nq = seq // tq
    seqmap = lambda b, hp: (b, hp)
    blk = pl.BlockSpec((seq, LANES), seqmap)
    return pl.pallas_call(
        functools.partial(_sb_prompt_kernel, tq=tq, nq=nq),
        grid=(batch, H_A // 2),
        in_specs=[pl.BlockSpec(memory_space=pltpu.SMEM), blk, blk, blk, blk,
                  pl.BlockSpec((tq, tq), lambda b, hp: (0, 0))],
        out_specs=blk,
        out_shape=jax.ShapeDtypeStruct((n, W_A), BF16),
        scratch_shapes=[pltpu.VMEM((nq, 2 * tq, LANES), BF16), pltpu.VMEM((nq, tq, LANES), F32),
                        pltpu.VMEM((nq, 2 * tq, 1), F32),
                        pltpu.VMEM((SB_RING, 2 * tq, tq), F32), pltpu.VMEM((SB_RING, 2 * tq, tq), F32),
                        pltpu.VMEM((SB_RING, 2 * tq, 1), F32)],
        compiler_params=_params(("parallel", "parallel")),
        name="sb_prompt",
    )(bias, q, k, v, ga, ntri)


def _sb_sample_kernel(pt_ref, q_ref, kn_ref, vn_ref, ga_ref, bias_ref, ntri_ref, *rest, ts, pages):
    page_refs = rest[:2 * pages]
    o_ref, acc_ref, carry_ref = rest[2 * pages:]
    s = pl.program_id(1)
    rows = H_A * ts
    ps = ntri_ref.shape[0]
    q = q_ref[...]
    qpair = [jnp.concatenate([q[:, (2 * p) * HD_A:(2 * p + 1) * HD_A], q[:, (2 * p + 1) * HD_A:(2 * p + 2) * HD_A]],
                             axis=0).astype(BF16) for p in range(H_A // 2)]
    bias = bias_ref[...]
    ntri = ntri_ref[...]

    def own_rows(x, h):
        return x[(h % 2) * ts:(h % 2 + 1) * ts]

    def sweep(n, qk, pv, carry, mask):
        z = jnp.concatenate(
            [jnp.concatenate([own_rows(qk(j, h, qpair[h // 2]), h) for h in range(H_A)], axis=0)
             for j in range(n)], axis=1) + bias
        nl = _sb_nl(z, mask)
        incls = [_sb_incl(nl[:, j * ps:(j + 1) * ps], ntri) for j in range(n)]
        sums = []
        for j in range(n):
            sums.append(carry + incls[j])
            carry = carry + incls[j][:, 0:1]
        w = jnp.exp2(z + jnp.concatenate(sums, axis=1))
        if mask is not None:
            w = jnp.where(mask, w, 0.0)
        w = w.astype(BF16)
        parts = [jnp.concatenate([own_rows(pv(j, h, w[(h // 2) * 2 * ts:(h // 2 + 1) * 2 * ts, j * ps:(j + 1) * ps]), h)
                                  for h in range(H_A)], axis=0) for j in range(n)]
        contrib = parts[0]
        for c in parts[1:]:
            contrib = contrib + c
        return contrib, carry

    @pl.when(s == 0)
    def _():
        pad = jnp.zeros((ps - ts, HD_A), F32)
        kn = kn_ref[...]
        vn = vn_ref[...]
        new_k = lambda h: jnp.concatenate([kn[:, h * HD_A:(h + 1) * HD_A], pad], axis=0).astype(BF16)
        new_v = lambda h: jnp.concatenate([vn[:, h * HD_A:(h + 1) * HD_A], pad], axis=0).astype(BF16)
        r2 = lax.broadcasted_iota(jnp.int32, (rows, ps), 0)
        c2 = lax.broadcasted_iota(jnp.int32, (rows, ps), 1)
        mask = c2 < lax.rem(r2, ts)
        contrib, carry = sweep(1, lambda j, h, q16: _dot_nt(q16, new_k(h)), lambda j, h, w16: _dot(w16, new_v(h)),
                               jnp.zeros((rows, 1), F32), mask)
        acc_ref[...] = contrib
        carry_ref[...] = carry

    contrib, carry = sweep(pages,
                           lambda j, h, q16: _dot(q16, page_refs[2 * j][h].astype(BF16)),
                           lambda j, h, w16: _dot_nt(w16, page_refs[2 * j + 1][h].astype(BF16)),
                           carry_ref[...], None)
    acc_ref[...] += contrib
    carry_ref[...] = carry

    @pl.when(s == pl.num_programs(1) - 1)
    def _():
        acc = acc_ref[...]
        o = jnp.concatenate([acc[h * ts:(h + 1) * ts] for h in range(H_A)], axis=-1)
        o_ref[...] = (o * ga_ref[...]).astype(o_ref.dtype)


def _sb_sample(q, kn, vn, ga, bias_rows, ntri, cache_k, cache_v, page_table, ts, pages):
    n = q.shape[0]
    nb, n_pages = page_table.shape
    ps = cache_k.shape[3]
    steps = n_pages // pages
    tok = lambda b, s, pt: (b, 0)
    const = lambda b, s, pt: (0, 0)

    def page_map(j):
        return lambda b, s, pt: (pt[b, n_pages - 1 - (s * pages + j)], 0, 0, 0)

    page_specs, page_args = [], []
    for j in range(pages):
        page_specs += [pl.BlockSpec((None, H_A, HD_A, ps), page_map(j))] * 2
        page_args += [cache_k, cache_v]
    rows = H_A * ts
    grid_spec = pltpu.PrefetchScalarGridSpec(
        num_scalar_prefetch=1,
        grid=(nb, steps),
        in_specs=[pl.BlockSpec((ts, W_A), tok)] * 4
                 + [pl.BlockSpec((rows, 1), const), pl.BlockSpec((ps, ps), const)] + page_specs,
        out_specs=pl.BlockSpec((ts, W_A), tok),
        scratch_shapes=[pltpu.VMEM((rows, HD_A), F32), pltpu.VMEM((rows, 1), F32)],
    )
    return pl.pallas_call(
        functools.partial(_sb_sample_kernel, ts=ts, pages=pages),
        grid_spec=grid_spec,
        out_shape=jax.ShapeDtypeStruct((n, W_A), F32),
        compiler_params=_params(("parallel", "arbitrary")),
        name="sb_sample",
    )(page_table, q, kn, vn, ga, bias_rows, ntri, *page_args)


def _hgrn_tables(c, n_valid):
    levels = int(math.log2(c))
    lm = np.zeros((levels, c, c), np.float32)
    r = np.arange(c)
    for l in range(levels):
        m = 2 ** l
        second = (r // m) % 2 == 1
        same = (r[:, None] // (2 * m)) == (r[None, :] // (2 * m))
        lm[l] = (same & second[:, None] & (~second)[None, :]).astype(np.float32)
    a = np.tril(np.ones((c, c), np.float32))
    a[:, n_valid:] = 0.0
    return a, lm


def _hold_mid(b, m, rowi):
    c = b.shape[0]
    if m == 1:
        return jnp.where(jnp.bitwise_and(rowi, 1) == 1, pltpu.roll(b, 1, 0), b)
    if m == 2:
        lo = jnp.concatenate([jnp.broadcast_to(b[g + 1:g + 2], (8, LANES)) for g in range(0, c, 8)], axis=0)
        hi = jnp.concatenate([jnp.broadcast_to(b[g + 5:g + 6], (8, LANES)) for g in range(0, c, 8)], axis=0)
        return jnp.where(jnp.bitwise_and(rowi, 4) == 0, lo, hi)
    return jnp.concatenate([jnp.broadcast_to(b[g + m - 1:g + m], (2 * m, LANES)) for g in range(0, c, 2 * m)], axis=0)


def _hgrn_kernel(q_ref, f_ref, i_ref, g_ref, s0_ref, lbp_ref, og_ref, a_ref, lm_ref,
                 o_ref, sout_ref, st_ref, *, c, n_chunks, n_valid, layer, nb):
    t = pl.program_id(1)
    levels = lm_ref.shape[0]
    chains = [(bi, h) for bi in range(nb) for h in range(H_B)]

    @pl.when(t == 0)
    def _():
        for bi, h in chains:
            st_ref[bi, h] = s0_ref[bi, h].T

    lbp = lbp_ref[...]
    p = jnp.exp(lbp - jnp.max(lbp, axis=0, keepdims=True))
    lb = jnp.sum(p[:layer + 1], axis=0, keepdims=True) / jnp.sum(p, axis=0, keepdims=True)
    og = og_ref[...]
    rowi = lax.broadcasted_iota(jnp.int32, (c, 1), 0)
    valid = rowi < n_valid

    def load(ref, bi, ci):
        if n_valid < c:
            x = ref[bi]
            return jnp.concatenate([x, jnp.zeros((c - x.shape[0], x.shape[1]), F32)], axis=0)
        return ref[bi, pl.ds(pl.multiple_of(ci * c, c), c), :]

    def chunk(ci, _):
        gs, kks, qqs, vvs = [], [], [], []
        for bi in range(nb):
            xf = load(f_ref, bi, ci)
            e = jnp.exp(-jnp.abs(xf))
            r = 1.0 / (1.0 + e)
            sig_pos = jnp.where(xf >= 0, r, e * r)
            sig_neg = jnp.where(xf >= 0, e * r, r)
            g = jnp.log(lb + (1.0 - lb) * sig_pos)
            kk = (1.0 - lb) * sig_neg
            if n_valid < c:
                g = jnp.where(valid, g, 0.0)
                kk = jnp.where(valid, kk, 0.0)
            gs.append(g)
            kks.append(kk)
            qqs.append(_silu(load(q_ref, bi, ci)))
            vvs.append(load(i_ref, bi, ci))
        ghi, gmid, glo = _split3(jnp.concatenate(gs, axis=1))
        a = a_ref[...]
        bcum_all = _dot(a, ghi) + _dot(a, gmid) + _dot(a, glo)
        outs = [[] for _ in range(nb)]
        for bi, h in chains:
            hl = slice(h * LANES, (h + 1) * LANES)
            el = slice((bi * H_B + h) * LANES, (bi * H_B + h + 1) * LANES)
            qq, kk, vv = qqs[bi][:, hl], kks[bi][:, hl], vvs[bi][:, hl]
            vv16 = vv.astype(BF16)
            bcum = bcum_all[:, el]
            rem = bcum[c - 1:c, :] - bcum
            st = st_ref[bi, h]
            o = _dot_nt((qq * jnp.exp(bcum)).astype(BF16), st.astype(BF16))
            scores = jnp.zeros((c, c), F32)
            for l in range(levels):
                second = jnp.bitwise_and(jnp.right_shift(rowi, l), 1) == 1
                d = bcum - _hold_mid(bcum, 2 ** l, rowi)
                x = jnp.exp(jnp.where(second, d, -d))
                scores = scores + _dot_nt((qq * x).astype(BF16), (kk * x).astype(BF16)) * lm_ref[l]
            o = o + _dot(scores.astype(BF16), vv16)
            o = o + jnp.sum(qq * kk, axis=-1, keepdims=True) * vv
            dec_last = jnp.exp(bcum[c - 1:c, :])
            st_ref[bi, h] = dec_last * st + _dot_tn(vv16, (kk * jnp.exp(rem)).astype(BF16))
            outs[bi].append(_rms(o) * og)
        for bi in range(nb):
            ob = jnp.concatenate(outs[bi], axis=1) * load(g_ref, bi, ci)
            if n_valid < c:
                o_ref[bi] = ob[:n_valid].astype(o_ref.dtype)
            else:
                o_ref[bi, pl.ds(pl.multiple_of(ci * c, c), c), :] = ob.astype(o_ref.dtype)
        return 0

    lax.fori_loop(0, n_chunks, chunk, 0)

    @pl.when(t == pl.num_programs(1) - 1)
    def _():
        for bi, h in chains:
            sout_ref[bi, h] = st_ref[bi, h].T


def _hgrn(qb, fb, ib, gb, s0, lbp, og, batch, seq, layer, out_dtype, tb, nb):
    n = qb.shape[0]
    c = HGRN_CHUNK
    if seq >= c:
        n_valid, rows_blk, nt, n_chunks = c, tb, seq // tb, tb // c
    else:
        n_valid, rows_blk, nt, n_chunks = seq, seq, 1, 1
    a_np, lm_np = _hgrn_tables(c, n_valid)
    a = jnp.asarray(a_np, BF16)
    lm = jnp.asarray(lm_np, F32)
    tok = lambda b, t: (b, t, 0)
    st = lambda b, t: (b, 0, 0, 0)
    blk = pl.BlockSpec((nb, rows_blk, W_B), tok)
    st_blk = pl.BlockSpec((nb, H_B, DK_B, DV_B), st)
    r3 = lambda x: x.reshape(batch, seq, W_B)
    mix, s_out = pl.pallas_call(
        functools.partial(_hgrn_kernel, c=c, n_chunks=n_chunks, n_valid=n_valid, layer=layer, nb=nb),
        grid=(batch // nb, nt),
        in_specs=[blk, blk, blk, blk, st_blk,
                  pl.BlockSpec(lbp.shape, lambda b, t: (0, 0)),
                  pl.BlockSpec((1, LANES), lambda b, t: (0, 0)),
                  pl.BlockSpec(a.shape, lambda b, t: (0, 0)),
                  pl.BlockSpec(lm.shape, lambda b, t: (0, 0, 0))],
        out_specs=[blk, st_blk],
        out_shape=[jax.ShapeDtypeStruct((batch, seq, W_B), out_dtype),
                   jax.ShapeDtypeStruct((batch, H_B, DK_B, DV_B), F32)],
        scratch_shapes=[pltpu.VMEM((nb, H_B, DV_B, DK_B), F32)],
        compiler_params=_params(("parallel", "arbitrary")),
        name="hgrn",
    )(r3(qb), r3(fb), r3(ib), r3(gb), s0, lbp, og, a, lm)
    return mix.reshape(n, W_B), s_out


def _outproj_even_kernel(x_ref, ma_ref, mb_ref, w_ref, y_ref):
    y = x_ref[...] + _dot(ma_ref[...].astype(BF16), w_ref[:W_A, :]) + _dot(mb_ref[...].astype(BF16), w_ref[W_A:, :])
    y_ref[...] = y


def _outproj_even(x, ma, mb, w, tm):
    n, d = x.shape
    row = lambda i: (i, 0)
    return pl.pallas_call(
        _outproj_even_kernel,
        grid=(n // tm,),
        in_specs=[pl.BlockSpec((tm, d), row), pl.BlockSpec((tm, W_A), row), pl.BlockSpec((tm, W_B), row),
                  pl.BlockSpec(w.shape, lambda i: (0, 0))],
        out_specs=pl.BlockSpec((tm, d), row),
        out_shape=jax.ShapeDtypeStruct((n, d), F32),
        compiler_params=_params(("parallel",)),
        name="outproj_even",
    )(x, ma, mb, w)


def _inproj_odd_kernel(x_ref, ng_ref, w_ref, cos_ref, sin_ref, q_ref, k_ref, v_ref, g_ref):
    x = x_ref[...]
    h = (_rms(x) * ng_ref[...]).astype(BF16)
    cos = cos_ref[...]
    sin = sin_ref[...]
    half = DK_C // 2

    def rot(a, scale):
        outs = []
        for hd in range(H_C):
            x1 = a[:, hd * DK_C:hd * DK_C + half]
            x2 = a[:, hd * DK_C + half:(hd + 1) * DK_C]
            outs += [(x1 * cos - x2 * sin) * scale, (x2 * cos + x1 * sin) * scale]
        return jnp.concatenate(outs, axis=-1)

    q_ref[...] = rot(_dot(h, w_ref[:, :QK_C]), 1.0).astype(q_ref.dtype)
    k_ref[...] = rot(_dot(h, w_ref[:, QK_C:2 * QK_C]), DK_C ** -0.5)
    for j in range(2):
        lo = 2 * QK_C + j * QK_C
        v_ref[:, j * QK_C:(j + 1) * QK_C] = _dot(h, w_ref[:, lo:lo + QK_C]).astype(v_ref.dtype)
    for j in range(2):
        lo = 2 * QK_C + W_C + j * QK_C
        g_ref[:, j * QK_C:(j + 1) * QK_C] = _silu(_dot(h, w_ref[:, lo:lo + QK_C]))


def _inproj_odd(x, ng, w, cos, sin, act_dtype, tm):
    n, d = x.shape
    npos = cos.shape[0] // tm
    row = lambda i: (i, 0)
    const = lambda i: (0, 0)
    pos = lambda i: (i % npos, 0)
    return pl.pallas_call(
        _inproj_odd_kernel,
        grid=(n // tm,),
        in_specs=[pl.BlockSpec((tm, d), row), pl.BlockSpec((1, d), const),
                  pl.BlockSpec(w.shape, const, pipeline_mode=pl.Buffered(1)),
                  pl.BlockSpec((tm, DK_C // 2), pos), pl.BlockSpec((tm, DK_C // 2), pos)],
        out_specs=[pl.BlockSpec((tm, QK_C), row), pl.BlockSpec((tm, QK_C), row),
                   pl.BlockSpec((tm, W_C), row), pl.BlockSpec((tm, W_C), row)],
        out_shape=[jax.ShapeDtypeStruct((n, QK_C), act_dtype), jax.ShapeDtypeStruct((n, QK_C), F32),
                   jax.ShapeDtypeStruct((n, W_C), act_dtype), jax.ShapeDtypeStruct((n, W_C), F32)],
        compiler_params=_params(("parallel",)),
        name="inproj_odd",
    )(x, ng, w, cos, sin)


def _ret_kernel(q_ref, k_ref, v_ref, g_ref, s0_ref, idec_ref, qdec_ref, kdec_ref, cdec_ref,
                o_ref, sout_ref, s_ref, *, c, n_chunks, n_valid, nb, hpg):
    t = pl.program_id(2)
    chains = [(bi, hh) for bi in range(nb) for hh in range(hpg)]

    @pl.when(t == 0)
    def _():
        s_ref[...] = s0_ref[...]

    def load(ref, bi, hh, width, ci):
        cols = slice(hh * width, (hh + 1) * width)
        if n_valid < c:
            x = ref[bi, :, cols].astype(F32)
            return jnp.concatenate([x, jnp.zeros((c - x.shape[0], x.shape[1]), F32)], axis=0)
        return ref[bi, pl.ds(pl.multiple_of(ci * c, c), c), cols]

    def chunk(ci, _):
        for bi, hh in chains:
            qc = load(q_ref, bi, hh, DK_C, ci).astype(BF16)
            kc = load(k_ref, bi, hh, DK_C, ci)
            vc = load(v_ref, bi, hh, DV_C, ci).astype(BF16)
            s = s_ref[bi, hh]
            scores = _dot_nt(qc, kc.astype(BF16)) * idec_ref[hh]
            o = _dot(scores.astype(BF16), vc) + _dot(qc, s.astype(BF16)) * qdec_ref[hh, :, 0:1]
            s_ref[bi, hh] = cdec_ref[hh, 0:1, 0:1] * s + _dot_tn((kc * kdec_ref[hh, :, 0:1]).astype(BF16), vc)
            ob = _rms(o) * load(g_ref, bi, hh, DV_C, ci)
            cols = slice(hh * DV_C, (hh + 1) * DV_C)
            if n_valid < c:
                o_ref[bi, :, cols] = ob[:n_valid].astype(o_ref.dtype)
            else:
                o_ref[bi, pl.ds(pl.multiple_of(ci * c, c), c), cols] = ob.astype(o_ref.dtype)
        return 0

    lax.fori_loop(0, n_chunks, chunk, 0)

    @pl.when(t == pl.num_programs(2) - 1)
    def _():
        sout_ref[...] = s_ref[...]


def _ret_tables(c, chunk):
    f32 = jnp.float32
    log_gamma = jnp.log1p(-jnp.exp2(-5.0 - jnp.arange(H_C, dtype=f32)))
    idx = jnp.arange(c, dtype=f32)
    real = idx < chunk
    rel = idx[:, None] - idx[None, :]
    ok = (rel >= 0) & real[:, None] & real[None, :]
    idec = jnp.exp(jnp.where(ok[None], rel[None] * log_gamma[:, None, None], -jnp.inf))
    qdec = jnp.where(real[None, :], jnp.exp((idx[None, :] + 1.0) * log_gamma[:, None]), 0.0)
    kdec = jnp.where(real[None, :], jnp.exp((chunk - 1.0 - idx[None, :]) * log_gamma[:, None]), 0.0)
    cdec = jnp.exp(chunk * log_gamma)
    bc = lambda x: jnp.broadcast_to(x[:, :, None], (H_C, c, LANES))
    return idec, bc(qdec), bc(kdec), jnp.broadcast_to(cdec[:, None, None], (H_C, 8, LANES))


def _ret(q, k, v, g, s0, batch, seq, out_dtype, tb, nb, hpg):
    n = q.shape[0]
    if seq >= RET_CHUNK:
        c = RET_CHUNK
        n_valid, rows_blk, nt, n_chunks = c, tb, seq // tb, tb // c
    else:
        c = RET_PAD_CHUNK
        n_valid, rows_blk, nt, n_chunks = seq, seq, 1, 1
    idec, qdec, kdec, cdec = _ret_tables(c, n_valid)
    tok = lambda hg, b, t: (b, t, hg)
    st = lambda hg, b, t: (b, hg, 0, 0)
    hd = lambda hg, b, t: (hg, 0, 0)
    qk_blk = pl.BlockSpec((nb, rows_blk, hpg * DK_C), tok)
    vg_blk = pl.BlockSpec((nb, rows_blk, hpg * DV_C), tok)
    st_blk = pl.BlockSpec((nb, hpg, DK_C, DV_C), st)
    r3 = lambda x: x.reshape(batch, seq, x.shape[-1])
    mix, s_out = pl.pallas_call(
        functools.partial(_ret_kernel, c=c, n_chunks=n_chunks, n_valid=n_valid, nb=nb, hpg=hpg),
        grid=(H_C // hpg, batch // nb, nt),
        in_specs=[qk_blk, qk_blk, vg_blk, vg_blk, st_blk,
                  pl.BlockSpec((hpg, c, c), hd), pl.BlockSpec((hpg, c, LANES), hd),
                  pl.BlockSpec((hpg, c, LANES), hd), pl.BlockSpec((hpg, 8, LANES), hd)],
        out_specs=[vg_blk, st_blk],
        out_shape=[jax.ShapeDtypeStruct((batch, seq, W_C), out_dtype),
                   jax.ShapeDtypeStruct((batch, H_C, DK_C, DV_C), F32)],
        scratch_shapes=[pltpu.VMEM((nb, hpg, DK_C, DV_C), F32)],
        compiler_params=_params(("parallel", "parallel", "arbitrary")),
        name="retention",
    )(r3(q), r3(k), r3(v), r3(g), s0, idec, qdec, kdec, cdec)
    return mix.reshape(n, W_C), s_out


def _outproj_odd_kernel(x_ref, m_ref, w_ref, y_ref):
    y_ref[...] = x_ref[...] + _dot(m_ref[...].astype(BF16), w_ref[...])


def _outproj_odd(x, m, w, tm):
    n, d = x.shape
    row = lambda i: (i, 0)
    return pl.pallas_call(
        _outproj_odd_kernel,
        grid=(n // tm,),
        in_specs=[pl.BlockSpec((tm, d), row), pl.BlockSpec((tm, W_C), row), pl.BlockSpec(w.shape, lambda i: (0, 0))],
        out_specs=pl.BlockSpec((tm, d), row),
        out_shape=jax.ShapeDtypeStruct((n, d), F32),
        compiler_params=_params(("parallel",)),
        name="outproj_odd",
    )(x, m, w)


def _rope_tables(pos):
    half = DK_C // 2
    inv = 1.0 / (ROPE_BASE ** jnp.linspace(0.0, 1.0, half, dtype=F32))
    ang = pos[:, None] * inv[None, :]
    return jnp.cos(ang), jnp.sin(ang)


def _ntri(n):
    return jnp.asarray(-np.tril(np.ones((n, n), np.float32)), BF16)


def kernel(x_prompt, x_sample, cache_k, cache_v, page_table, state_hgrn, state_ret, norm_g,
           w_in_even, w_out_even, q_norm_g, k_norm_g, sb_logit_bias, hgrn_lower_bounds,
           hgrn_out_norm_g, w_in_odd, w_out_odd):
    bp, tp, d = x_prompt.shape
    bs, ts, _ = x_sample.shape
    n_pool, page_size = cache_k.shape[1], cache_k.shape[2]
    past_len = page_table.shape[1] * page_size
    depth = norm_g.shape[0]
    tm_p = 512
    tm_o = 512
    tm_s = bs * ts

    yp = x_prompt.reshape(bp * tp, d)
    ys = x_sample.reshape(bs * ts, d)
    grp = jnp.asarray(np.kron(np.eye(H_A, dtype=np.float32), np.ones((HD_A, HD_A), np.float32)), BF16)
    cos_p, sin_p = _rope_tables(jnp.arange(tp, dtype=F32))
    cos_s, sin_s = _rope_tables(past_len + jnp.arange(ts, dtype=F32))
    cos_s, sin_s = jnp.tile(cos_s, (bs, 1)), jnp.tile(sin_s, (bs, 1))

    k_p, v_p, k_s, v_s, hg_p, hg_s, rt_p, rt_s = [], [], [], [], [], [], [], []
    for layer in range(depth):
        e = layer // 2
        ng = norm_g[layer].reshape(1, d)
        if layer % 2 == 0:
            w_in = w_in_even[e].astype(BF16)
            w_out = w_out_even[e].astype(BF16)
            qg = jnp.tile(q_norm_g[e], H_A).reshape(1, W_A)
            kg = jnp.tile(k_norm_g[e], H_A).reshape(1, W_A)
            og = hgrn_out_norm_g[e].reshape(1, DV_B)
            bias = sb_logit_bias[e].astype(F32)
            q, k, v, k32, v32, ga, qb, fb, ib, gb = _inproj_even(yp, ng, w_in, qg, kg, grp, BF16, tm_p, kv_seq=tp)
            ma = _sb_prompt(q, k, v, ga, bias, _ntri(SB_TQ), bp, tp, SB_TQ)
            mb, hs = _hgrn(qb, fb, ib, gb, jnp.zeros((bp, H_B, DK_B, DV_B), F32), hgrn_lower_bounds,
                           og, bp, tp, e, BF16, 512, HGRN_NB)
            yp = _outproj_even(yp, ma, mb, w_out, tm_o)
            k_p.append(jnp.transpose(k32.reshape(bp, H_A, HD_A, tp), (0, 3, 1, 2)))
            v_p.append(jnp.transpose(v32.reshape(bp, H_A, HD_A, tp), (0, 3, 1, 2)))
            hg_p.append(hs)
            q, k, v, k32, v32, ga, qb, fb, ib, gb = _inproj_even(ys, ng, w_in, qg, kg, grp, F32, tm_s)
            ma = _sb_sample(q, k, v, ga, jnp.repeat(bias * LOG2E, ts).reshape(H_A * ts, 1), _ntri(page_size),
                            jnp.transpose(cache_k[e], (0, 2, 3, 1)), jnp.transpose(cache_v[e], (0, 2, 3, 1)),
                            page_table, ts, SB_PAGES_PER_STEP)
            mb, hs = _hgrn(qb, fb, ib, gb, state_hgrn[e].astype(F32), hgrn_lower_bounds, og, bs, ts, e, F32, ts,
                           HGRN_NB)
            ys = _outproj_even(ys, ma, mb, w_out, tm_s)
            k_s.append(k32.reshape(bs, ts, H_A, HD_A))
            v_s.append(v32.reshape(bs, ts, H_A, HD_A))
            hg_s.append(hs)
        else:
            w_in = w_in_odd[e].astype(BF16)
            w_out = w_out_odd[e].astype(BF16)
            q, k, v, g = _inproj_odd(yp, ng, w_in, cos_p, sin_p, BF16, tm_p)
            m, s = _ret(q, k, v, g, jnp.zeros((bp, H_C, DK_C, DV_C), F32), bp, tp, BF16, 512, RET_NB, RET_HPG)
            yp = _outproj_odd(yp, m, w_out, tm_o)
            rt_p.append(s)
            q, k, v, g = _inproj_odd(ys, ng, w_in, cos_s, sin_s, F32, tm_s)
            m, s = _ret(q, k, v, g, state_ret[e].astype(F32), bs, ts, F32, ts, RET_NB, RET_HPG)
            ys = _outproj_odd(ys, m, w_out, tm_s)
            rt_s.append(s)
    return (yp.reshape(bp, tp, d), ys.reshape(bs, ts, d), jnp.stack(k_p), jnp.stack(v_p), jnp.stack(k_s),
            jnp.stack(v_s), jnp.stack(hg_p), jnp.stack(hg_s), jnp.stack(rt_p), jnp.stack(rt_s))
```

```python
import functools
import math

import numpy as np
import jax
import jax.numpy as jnp
from jax import lax
from jax.experimental import pallas as pl
from jax.experimental.pallas import tpu as pltpu

F32 = jnp.float32
BF16 = jnp.bfloat16

H_A, HD_A = 8, 64
W_A = H_A * HD_A
H_B, DK_B, DV_B = 4, 128, 128
F_B = H_B * DK_B
W_B = H_B * DV_B
H_C, DK_C, DV_C = 4, 256, 512
QK_C = H_C * DK_C
W_C = H_C * DV_C
HGRN_CHUNK = 64
RET_CHUNK = 256
RET_PAD_CHUNK = 128
ROPE_BASE = 10000.0
NORM_EPS = 1e-6

LANES = 128
VMEM_LIMIT = 56 * 1024 * 1024
SB_TQ = 256
RET_NB, RET_HPG = 2, 2
HGRN_NB = 2
SB_RING = 3
SB_UNROLL = 8 * SB_RING
SB_PAGES_PER_STEP = 16
LOG2E = 1.4426950408889634


def _dot(a, b):
    return jnp.dot(a, b, preferred_element_type=F32)


def _dot_nt(a, b):
    return lax.dot_general(a, b, (((1,), (1,)), ((), ())), preferred_element_type=F32)


def _dot_tn(a, b):
    return lax.dot_general(a, b, (((0,), (0,)), ((), ())), preferred_element_type=F32)


def _split2(x):
    hi = x.astype(BF16)
    lo = (x - hi.astype(F32)).astype(BF16)
    return hi, lo


def _split3(x):
    hi = x.astype(BF16)
    r = x - hi.astype(F32)
    mid = r.astype(BF16)
    lo = (r - mid.astype(F32)).astype(BF16)
    return hi, mid, lo


def _sigmoid(x):
    return 1.0 / (1.0 + jnp.exp(-x))


def _silu(x):
    return x * _sigmoid(x)


def _rms(x, eps=NORM_EPS):
    return x * lax.rsqrt(jnp.mean(x * x, axis=-1, keepdims=True) + eps)


def _params(sem):
    return pltpu.CompilerParams(dimension_semantics=sem, vmem_limit_bytes=VMEM_LIMIT)


def _inproj_even_kernel(x_ref, ng_ref, w_ref, qg_ref, kg_ref, grp_ref,
                        q_ref, k_ref, v_ref, k32_ref, v32_ref, ga_ref,
                        qb_ref, fb_ref, ib_ref, gb_ref, *, kv_token_minor):
    x = x_ref[...]
    h = (_rms(x) * ng_ref[...]).astype(BF16)

    def proj(c):
        return _dot(h, w_ref[:, c * W_A:(c + 1) * W_A].astype(BF16))

    def head_norm(a, gain):
        hi, lo = _split2(a * a)
        ssum = _dot(hi, grp_ref[...]) + _dot(lo, grp_ref[...])
        return a * lax.rsqrt(ssum * (1.0 / HD_A) + NORM_EPS) * gain

    qa = head_norm(proj(0), qg_ref[...])
    q_ref[...] = (qa * (HD_A ** -0.5 * LOG2E)).astype(q_ref.dtype)
    ka = head_norm(proj(1), kg_ref[...])
    k32_ref[...] = ka.T if kv_token_minor else ka
    k_ref[...] = ka.astype(k_ref.dtype)
    va = proj(2)
    v32_ref[...] = va.T if kv_token_minor else va
    v_ref[...] = va.astype(v_ref.dtype)
    ga_ref[...] = _silu(proj(3))
    qb_ref[...] = proj(4)
    fb_ref[...] = proj(5)
    ib_ref[...] = proj(6)
    gb_ref[...] = _silu(proj(7))


def _inproj_even(x, ng, w, qg, kg, grp, act_dtype, tm, kv_seq=None):
    n, d = x.shape
    row = lambda i: (i, 0)
    const = lambda i: (0, 0)
    blk = pl.BlockSpec((tm, W_A), row)
    outs = [jax.ShapeDtypeStruct((n, W_A), act_dtype)] * 3 + [jax.ShapeDtypeStruct((n, W_A), F32)] * 7
    out_specs = [blk] * 10
    if kv_seq is not None:
        nt = kv_seq // tm
        kv_blk = pl.BlockSpec((None, W_A, tm), lambda i: (i // nt, 0, i % nt))
        kv_shape = jax.ShapeDtypeStruct((n // kv_seq, W_A, kv_seq), F32)
        outs[3:5] = [kv_shape, kv_shape]
        out_specs[3:5] = [kv_blk, kv_blk]
    return pl.pallas_call(
        functools.partial(_inproj_even_kernel, kv_token_minor=kv_seq is not None),
        grid=(n // tm,),
        in_specs=[pl.BlockSpec((tm, d), row), pl.BlockSpec((1, d), const),
                  pl.BlockSpec(w.shape, const, pipeline_mode=pl.Buffered(1)), pl.BlockSpec((1, W_A), const),
                  pl.BlockSpec((1, W_A), const), pl.BlockSpec((W_A, W_A), const)],
        out_specs=out_specs,
        out_shape=outs,
        compiler_params=_params(("parallel",)),
        name="inproj_even",
    )(x, ng, w, qg, kg, grp)


def _sb_nl(z, mask):
    neg_abs = lax.bitcast_convert_type(lax.bitcast_convert_type(z, jnp.uint32) | jnp.uint32(0x80000000), F32)
    nl = jnp.maximum(z, 0.0) + jnp.log(1.0 + jnp.exp2(neg_abs)) * LOG2E
    if mask is not None:
        nl = jnp.where(mask, nl, 0.0)
    return nl


def _sb_incl(nl, ntri):
    return _dot(nl.astype(BF16), ntri)


def _sb_prompt_kernel(bias_ref, q_ref, k_ref, v_ref, ga_ref, ntri_ref, o_ref, qs_ref, acc_ref, carry_ref,
                      z_ref, zi_ref, col_ref, *, tq, nq):
    hp = pl.program_id(1)
    rows = 2 * tq
    lane = lax.broadcasted_iota(jnp.int32, (tq, LANES), 1)

    b0 = bias_ref[2 * hp] * LOG2E
    b1 = bias_ref[2 * hp + 1] * LOG2E

    def blk(i):
        return pl.ds(pl.multiple_of(i * tq, tq), tq)

    def logits(qs, kb):
        zr = _dot_nt(qs, k_ref[blk(kb), :])
        return jnp.concatenate([zr[:tq] + b0, zr[tq:] + b1], axis=0)

    r2 = lax.broadcasted_iota(jnp.int32, (rows, tq), 0)
    c2 = lax.broadcasted_iota(jnp.int32, (rows, tq), 1)
    mask = c2 < jnp.where(r2 >= tq, r2 - tq, r2)

    def stage_a(p, slot, diag):
        if diag:
            q = q_ref[blk(p[0]), :].astype(F32)
            qs = jnp.concatenate([jnp.where(lane < HD_A, q, 0.0), jnp.where(lane >= HD_A, q, 0.0)],
                                 axis=0).astype(BF16)
            qs_ref[p[0]] = qs
        else:
            qs = qs_ref[p[0]]
        z_ref[slot] = logits(qs, p[1])

    def stage_b(slot, diag):
        z = z_ref[slot]
        incl = _sb_incl(_sb_nl(z, mask if diag else None), ntri_ref[...])
        zi_ref[slot] = z + incl
        col_ref[slot] = incl[:, 0:1]

    def stage_c(p, slot, diag):
        def own_lanes(x):
            return jnp.where(lane < HD_A, x[:tq], x[tq:])

        if diag:
            w = jnp.where(mask, jnp.exp2(zi_ref[slot]), 0.0)
            acc_ref[p[0]] = own_lanes(_dot(w.astype(BF16), v_ref[blk(p[1]), :]))
            carry_ref[p[0]] = col_ref[slot]
        else:
            carry = carry_ref[p[0]]
            w = jnp.exp2(zi_ref[slot] + carry)
            acc_ref[p[0]] += own_lanes(_dot(w.astype(BF16), v_ref[blk(p[1]), :]))
            carry_ref[p[0]] = carry + col_ref[slot]

    def pipeline(n_steps, p0, nxt, diag):
        if n_steps == 0:
            return
        stage_a(p0, 0, diag)
        stage_b(0, diag)
        if n_steps == 1:
            stage_c(p0, 0, diag)
            return
        stage_a(nxt(p0), 1, diag)
        n_steady = n_steps - 2

        def step(t, pc, pa):
            stage_c(pc, t % SB_RING, diag)
            stage_b((t + 1) % SB_RING, diag)
            stage_a(pa, (t + 2) % SB_RING, diag)
            return nxt(pc), nxt(pa)

        def body(_, st):
            pc, pa = st[:2], st[2:]
            for u in range(SB_UNROLL):
                pc, pa = step(u, pc, pa)
            return (*pc, *pa)

        n_trips = n_steady // SB_UNROLL
        st = lax.fori_loop(0, n_trips, body, (*p0, *nxt(nxt(p0))))
        pc, pa = st[:2], st[2:]
        for t in range(n_trips * SB_UNROLL, n_steady):
            pc, pa = step(t, pc, pa)
        stage_c(pc, n_steady % SB_RING, diag)
        stage_b((n_steady + 1) % SB_RING, diag)
        stage_c(nxt(pc), (n_steady + 1) % SB_RING, diag)

    def next_off_diagonal(p):
        wrap = p[1] == 0
        return jnp.where(wrap, p[0] + 1, p[0]), jnp.where(wrap, p[0], p[1] - 1)

    pipeline(nq, (jnp.int32(0), jnp.int32(0)), lambda p: (p[0] + 1, p[1] + 1), True)
    pipeline(nq * (nq - 1) // 2, (jnp.int32(1), jnp.int32(0)), next_off_diagonal, False)

    def finish(qi, _):
        o_ref[blk(qi), :] = (acc_ref[qi] * ga_ref[blk(qi), :]).astype(o_ref.dtype)
        return 0

    lax.fori_loop(0, nq, finish, 0)


def _sb_prompt(q, k, v, ga, bias, ntri, batch, seq, tq):
    n = q.shape[0]
    nq = seq // tq
    seqmap = lambda b, hp: (b, hp)
    blk = pl.BlockSpec((seq, LANES), seqmap)
    return pl.pallas_call(
        functools.partial(_sb_prompt_kernel, tq=tq, nq=nq),
        grid=(batch, H_A // 2),
        in_specs=[pl.BlockSpec(memory_space=pltpu.SMEM), blk, blk, blk, blk,
                  pl.BlockSpec((tq, tq), lambda b, hp: (0, 0))],
        out_specs=blk,
        out_shape=jax.ShapeDtypeStruct((n, W_A), BF16),
        scratch_shapes=[pltpu.VMEM((nq, 2 * tq, LANES), BF16), pltpu.VMEM((nq, tq, LANES), F32),
                        pltpu.VMEM((nq, 2 * tq, 1), F32),
                        pltpu.VMEM((SB_RING, 2 * tq, tq), F32), pltpu.VMEM((SB_RING, 2 * tq, tq), F32),
                        pltpu.VMEM((SB_RING, 2 * tq, 1), F32)],
        compiler_params=_params(("parallel", "parallel")),
        name="sb_prompt",
    )(bias, q, k, v, ga, ntri)


def _sb_sample_kernel(pt_ref, q_ref, kn_ref, vn_ref, ga_ref, bias_ref, ntri_ref, *rest, ts, pages):
    page_refs = rest[:2 * pages]
    o_ref, acc_ref, carry_ref = rest[2 * pages:]
    s = pl.program_id(1)
    rows = H_A * ts
    ps = ntri_ref.shape[0]
    q = q_ref[...]
    qpair = [jnp.concatenate([q[:, (2 * p) * HD_A:(2 * p + 1) * HD_A], q[:, (2 * p + 1) * HD_A:(2 * p + 2) * HD_A]],
                             axis=0).astype(BF16) for p in range(H_A // 2)]
    bias = bias_ref[...]
    ntri = ntri_ref[...]

    def own_rows(x, h):
        return x[(h % 2) * ts:(h % 2 + 1) * ts]

    def sweep(n, qk, pv, carry, mask):
        z = jnp.concatenate(
            [jnp.concatenate([own_rows(qk(j, h, qpair[h // 2]), h) for h in range(H_A)], axis=0)
             for j in range(n)], axis=1) + bias
        nl = _sb_nl(z, mask)
        incls = [_sb_incl(nl[:, j * ps:(j + 1) * ps], ntri) for j in range(n)]
        sums = []
        for j in range(n):
            sums.append(carry + incls[j])
            carry = carry + incls[j][:, 0:1]
        w = jnp.exp2(z + jnp.concatenate(sums, axis=1))
        if mask is not None:
            w = jnp.where(mask, w, 0.0)
        w = w.astype(BF16)
        parts = [jnp.concatenate([own_rows(pv(j, h, w[(h // 2) * 2 * ts:(h // 2 + 1) * 2 * ts, j * ps:(j + 1) * ps]), h)
                                  for h in range(H_A)], axis=0) for j in range(n)]
        contrib = parts[0]
        for c in parts[1:]:
            contrib = contrib + c
        return contrib, carry

    @pl.when(s == 0)
    def _():
        pad = jnp.zeros((ps - ts, HD_A), F32)
        kn = kn_ref[...]
        vn = vn_ref[...]
        new_k = lambda h: jnp.concatenate([kn[:, h * HD_A:(h + 1) * HD_A], pad], axis=0).astype(BF16)
        new_v = lambda h: jnp.concatenate([vn[:, h * HD_A:(h + 1) * HD_A], pad], axis=0).astype(BF16)
        r2 = lax.broadcasted_iota(jnp.int32, (rows, ps), 0)
        c2 = lax.broadcasted_iota(jnp.int32, (rows, ps), 1)
        mask = c2 < lax.rem(r2, ts)
        contrib, carry = sweep(1, lambda j, h, q16: _dot_nt(q16, new_k(h)), lambda j, h, w16: _dot(w16, new_v(h)),
                               jnp.zeros((rows, 1), F32), mask)
        acc_ref[...] = contrib
        carry_ref[...] = carry

    contrib, carry = sweep(pages,
                           lambda j, h, q16: _dot(q16, page_refs[2 * j][h].astype(BF16)),
                           lambda j, h, w16: _dot_nt(w16, page_refs[2 * j + 1][h].astype(BF16)),
                           carry_ref[...], None)
    acc_ref[...] += contrib
    carry_ref[...] = carry

    @pl.when(s == pl.num_programs(1) - 1)
    def _():
        acc = acc_ref[...]
        o = jnp.concatenate([acc[h * ts:(h + 1) * ts] for h in range(H_A)], axis=-1)
        o_ref[...] = (o * ga_ref[...]).astype(o_ref.dtype)


def _sb_sample(q, kn, vn, ga, bias_rows, ntri, cache_k, cache_v, page_table, ts, pages):
    n = q.shape[0]
    nb, n_pages = page_table.shape
    ps = cache_k.shape[3]
    steps = n_pages // pages
    tok = lambda b, s, pt: (b, 0)
    const = lambda b, s, pt: (0, 0)

    def page_map(j):
        return lambda b, s, pt: (pt[b, n_pages - 1 - (s * pages + j)], 0, 0, 0)

    page_specs, page_args = [], []
    for j in range(pages):
        page_specs += [pl.BlockSpec((None, H_A, HD_A, ps), page_map(j))] * 2
        page_args += [cache_k, cache_v]
    rows = H_A * ts
    grid_spec = pltpu.PrefetchScalarGridSpec(
        num_scalar_prefetch=1,
        grid=(nb, steps),
        in_specs=[pl.BlockSpec((ts, W_A), tok)] * 4
                 + [pl.BlockSpec((rows, 1), const), pl.BlockSpec((ps, ps), const)] + page_specs,
        out_specs=pl.BlockSpec((ts, W_A), tok),
        scratch_shapes=[pltpu.VMEM((rows, HD_A), F32), pltpu.VMEM((rows, 1), F32)],
    )
    return pl.pallas_call(
        functools.partial(_sb_sample_kernel, ts=ts, pages=pages),
        grid_spec=grid_spec,
        out_shape=jax.ShapeDtypeStruct((n, W_A), F32),
        compiler_params=_params(("parallel", "arbitrary")),
        name="sb_sample",
    )(page_table, q, kn, vn, ga, bias_rows, ntri, *page_args)


def _hgrn_tables(c, n_valid):
    levels = int(math.log2(c))
    lm = np.zeros((levels, c, c), np.float32)
    r = np.arange(c)
    for l in range(levels):
        m = 2 ** l
        second = (r // m) % 2 == 1
        same = (r[:, None] // (2 * m)) == (r[None, :] // (2 * m))
        lm[l] = (same & second[:, None] & (~second)[None, :]).astype(np.float32)
    a = np.tril(np.ones((c, c), np.float32))
    a[:, n_valid:] = 0.0
    return a, lm


def _hold_mid(b, m, rowi):
    c = b.shape[0]
    if m == 1:
        return jnp.where(jnp.bitwise_and(rowi, 1) == 1, pltpu.roll(b, 1, 0), b)
    if m == 2:
        lo = jnp.concatenate([jnp.broadcast_to(b[g + 1:g + 2], (8, LANES)) for g in range(0, c, 8)], axis=0)
        hi = jnp.concatenate([jnp.broadcast_to(b[g + 5:g + 6], (8, LANES)) for g in range(0, c, 8)], axis=0)
        return jnp.where(jnp.bitwise_and(rowi, 4) == 0, lo, hi)
    return jnp.concatenate([jnp.broadcast_to(b[g + m - 1:g + m], (2 * m, LANES)) for g in range(0, c, 2 * m)], axis=0)


def _hgrn_kernel(q_ref, f_ref, i_ref, g_ref, s0_ref, lbp_ref, og_ref, a_ref, lm_ref,
                 o_ref, sout_ref, st_ref, *, c, n_chunks, n_valid, layer, nb):
    t = pl.program_id(1)
    levels = lm_ref.shape[0]
    chains = [(bi, h) for bi in range(nb) for h in range(H_B)]

    @pl.when(t == 0)
    def _():
        for bi, h in chains:
            st_ref[bi, h] = s0_ref[bi, h].T

    lbp = lbp_ref[...]
    p = jnp.exp(lbp - jnp.max(lbp, axis=0, keepdims=True))
    lb = jnp.sum(p[:layer + 1], axis=0, keepdims=True) / jnp.sum(p, axis=0, keepdims=True)
    og = og_ref[...]
    rowi = lax.broadcasted_iota(jnp.int32, (c, 1), 0)
    valid = rowi < n_valid

    def load(ref, bi, ci):
        if n_valid < c:
            x = ref[bi]
            return jnp.concatenate([x, jnp.zeros((c - x.shape[0], x.shape[1]), F32)], axis=0)
        return ref[bi, pl.ds(pl.multiple_of(ci * c, c), c), :]

    def chunk(ci, _):
        gs, kks, qqs, vvs = [], [], [], []
        for bi in range(nb):
            xf = load(f_ref, bi, ci)
            e = jnp.exp(-jnp.abs(xf))
            r = 1.0 / (1.0 + e)
            sig_pos = jnp.where(xf >= 0, r, e * r)
            sig_neg = jnp.where(xf >= 0, e * r, r)
            g = jnp.log(lb + (1.0 - lb) * sig_pos)
            kk = (1.0 - lb) * sig_neg
            if n_valid < c:
                g = jnp.where(valid, g, 0.0)
                kk = jnp.where(valid, kk, 0.0)
            gs.append(g)
            kks.append(kk)
            qqs.append(_silu(load(q_ref, bi, ci)))
            vvs.append(load(i_ref, bi, ci))
        ghi, gmid, glo = _split3(jnp.concatenate(gs, axis=1))
        a = a_ref[...]
        bcum_all = _dot(a, ghi) + _dot(a, gmid) + _dot(a, glo)
        outs = [[] for _ in range(nb)]
        for bi, h in chains:
            hl = slice(h * LANES, (h + 1) * LANES)
            el = slice((bi * H_B + h) * LANES, (bi * H_B + h + 1) * LANES)
            qq, kk, vv = qqs[bi][:, hl], kks[bi][:, hl], vvs[bi][:, hl]
            vv16 = vv.astype(BF16)
            bcum = bcum_all[:, el]
            rem = bcum[c - 1:c, :] - bcum
            st = st_ref[bi, h]
            o = _dot_nt((qq * jnp.exp(bcum)).astype(BF16), st.astype(BF16))
            scores = jnp.zeros((c, c), F32)
            for l in range(levels):
                second = jnp.bitwise_and(jnp.right_shift(rowi, l), 1) == 1
                d = bcum - _hold_mid(bcum, 2 ** l, rowi)
                x = jnp.exp(jnp.where(second, d, -d))
                scores = scores + _dot_nt((qq * x).astype(BF16), (kk * x).astype(BF16)) * lm_ref[l]
            o = o + _dot(scores.astype(BF16), vv16)
            o = o + jnp.sum(qq * kk, axis=-1, keepdims=True) * vv
            dec_last = jnp.exp(bcum[c - 1:c, :])
            st_ref[bi, h] = dec_last * st + _dot_tn(vv16, (kk * jnp.exp(rem)).astype(BF16))
            outs[bi].append(_rms(o) * og)
        for bi in range(nb):
            ob = jnp.concatenate(outs[bi], axis=1) * load(g_ref, bi, ci)
            if n_valid < c:
                o_ref[bi] = ob[:n_valid].astype(o_ref.dtype)
            else:
                o_ref[bi, pl.ds(pl.multiple_of(ci * c, c), c), :] = ob.astype(o_ref.dtype)
        return 0

    lax.fori_loop(0, n_chunks, chunk, 0)

    @pl.when(t == pl.num_programs(1) - 1)
    def _():
        for bi, h in chains:
            sout_ref[bi, h] = st_ref[bi, h].T


def _hgrn(qb, fb, ib, gb, s0, lbp, og, batch, seq, layer, out_dtype, tb, nb):
    n = qb.shape[0]
    c = HGRN_CHUNK
    if seq >= c:
        n_valid, rows_blk, nt, n_chunks = c, tb, seq // tb, tb // c
    else:
        n_valid, rows_blk, nt, n_chunks = seq, seq, 1, 1
    a_np, lm_np = _hgrn_tables(c, n_valid)
    a = jnp.asarray(a_np, BF16)
    lm = jnp.asarray(lm_np, F32)
    tok = lambda b, t: (b, t, 0)
    st = lambda b, t: (b, 0, 0, 0)
    blk = pl.BlockSpec((nb, rows_blk, W_B), tok)
    st_blk = pl.BlockSpec((nb, H_B, DK_B, DV_B), st)
    r3 = lambda x: x.reshape(batch, seq, W_B)
    mix, s_out = pl.pallas_call(
        functools.partial(_hgrn_kernel, c=c, n_chunks=n_chunks, n_valid=n_valid, layer=layer, nb=nb),
        grid=(batch // nb, nt),
        in_specs=[blk, blk, blk, blk, st_blk,
                  pl.BlockSpec(lbp.shape, lambda b, t: (0, 0)),
                  pl.BlockSpec((1, LANES), lambda b, t: (0, 0)),
                  pl.BlockSpec(a.shape, lambda b, t: (0, 0)),
                  pl.BlockSpec(lm.shape, lambda b, t: (0, 0, 0))],
        out_specs=[blk, st_blk],
        out_shape=[jax.ShapeDtypeStruct((batch, seq, W_B), out_dtype),
                   jax.ShapeDtypeStruct((batch, H_B, DK_B, DV_B), F32)],
        scratch_shapes=[pltpu.VMEM((nb, H_B, DV_B, DK_B), F32)],
        compiler_params=_params(("parallel", "arbitrary")),
        name="hgrn",
    )(r3(qb), r3(fb), r3(ib), r3(gb), s0, lbp, og, a, lm)
    return mix.reshape(n, W_B), s_out


def _outproj_even_kernel(x_ref, ma_ref, mb_ref, w_ref, y_ref):
    y = x_ref[...] + _dot(ma_ref[...].astype(BF16), w_ref[:W_A, :]) + _dot(mb_ref[...].astype(BF16), w_ref[W_A:, :])
    y_ref[...] = y


def _outproj_even(x, ma, mb, w, tm):
    n, d = x.shape
    row = lambda i: (i, 0)
    return pl.pallas_call(
        _outproj_even_kernel,
        grid=(n // tm,),
        in_specs=[pl.BlockSpec((tm, d), row), pl.BlockSpec((tm, W_A), row), pl.BlockSpec((tm, W_B), row),
                  pl.BlockSpec(w.shape, lambda i: (0, 0))],
        out_specs=pl.BlockSpec((tm, d), row),
        out_shape=jax.ShapeDtypeStruct((n, d), F32),
        compiler_params=_params(("parallel",)),
        name="outproj_even",
    )(x, ma, mb, w)


def _outproj_even_inproj_odd_kernel(x_ref, ma_ref, mb_ref, wo_ref, ng_ref, w_ref, cos_ref, sin_ref,
                                    y_ref, q_ref, k_ref, v_ref, g_ref):
    y = (x_ref[...] + _dot(ma_ref[...].astype(BF16), wo_ref[:W_A, :])
         + _dot(mb_ref[...].astype(BF16), wo_ref[W_A:, :]))
    y_ref[...] = y
    _inproj_odd_body(y, ng_ref, w_ref, cos_ref, sin_ref, q_ref, k_ref, v_ref, g_ref)


def _inproj_odd_body(x, ng_ref, w_ref, cos_ref, sin_ref, q_ref, k_ref, v_ref, g_ref):
    h = (_rms(x) * ng_ref[...]).astype(BF16)
    cos = cos_ref[...]
    sin = sin_ref[...]
    half = DK_C // 2

    def rot(a, scale):
        outs = []
        for hd in range(H_C):
            x1 = a[:, hd * DK_C:hd * DK_C + half]
            x2 = a[:, hd * DK_C + half:(hd + 1) * DK_C]
            outs += [(x1 * cos - x2 * sin) * scale, (x2 * cos + x1 * sin) * scale]
        return jnp.concatenate(outs, axis=-1)

    q_ref[...] = rot(_dot(h, w_ref[:, :QK_C]), 1.0).astype(q_ref.dtype)
    k_ref[...] = rot(_dot(h, w_ref[:, QK_C:2 * QK_C]), DK_C ** -0.5)
    for j in range(2):
        lo = 2 * QK_C + j * QK_C
        v_ref[:, j * QK_C:(j + 1) * QK_C] = _dot(h, w_ref[:, lo:lo + QK_C]).astype(v_ref.dtype)
    for j in range(2):
        lo = 2 * QK_C + W_C + j * QK_C
        g_ref[:, j * QK_C:(j + 1) * QK_C] = _silu(_dot(h, w_ref[:, lo:lo + QK_C]))


def _outproj_even_inproj_odd(x, ma, mb, wo, ng, w, cos, sin, act_dtype, tm):
    n, d = x.shape
    npos = cos.shape[0] // tm
    row = lambda i: (i, 0)
    const = lambda i: (0, 0)
    pos = lambda i: (i % npos, 0)
    return pl.pallas_call(
        _outproj_even_inproj_odd_kernel,
        grid=(n // tm,),
        in_specs=[pl.BlockSpec((tm, d), row), pl.BlockSpec((tm, W_A), row), pl.BlockSpec((tm, W_B), row),
                  pl.BlockSpec(wo.shape, const, pipeline_mode=pl.Buffered(1)), pl.BlockSpec((1, d), const),
                  pl.BlockSpec(w.shape, const, pipeline_mode=pl.Buffered(1)),
                  pl.BlockSpec((tm, DK_C // 2), pos), pl.BlockSpec((tm, DK_C // 2), pos)],
        out_specs=[pl.BlockSpec((tm, d), row), pl.BlockSpec((tm, QK_C), row), pl.BlockSpec((tm, QK_C), row),
                   pl.BlockSpec((tm, W_C), row), pl.BlockSpec((tm, W_C), row)],
        out_shape=[jax.ShapeDtypeStruct((n, d), F32),
                   jax.ShapeDtypeStruct((n, QK_C), act_dtype), jax.ShapeDtypeStruct((n, QK_C), F32),
                   jax.ShapeDtypeStruct((n, W_C), act_dtype), jax.ShapeDtypeStruct((n, W_C), F32)],
        compiler_params=_params(("parallel",)),
        name="outproj_even_inproj_odd",
    )(x, ma, mb, wo, ng, w, cos, sin)


def _ret_kernel(q_ref, k_ref, v_ref, g_ref, s0_ref, idec_ref, qdec_ref, kdec_ref, cdec_ref,
                o_ref, sout_ref, s_ref, *, c, n_chunks, n_valid, nb, hpg):
    t = pl.program_id(2)
    chains = [(bi, hh) for bi in range(nb) for hh in range(hpg)]

    @pl.when(t == 0)
    def _():
        s_ref[...] = s0_ref[...]

    def load(ref, bi, hh, width, ci):
        cols = slice(hh * width, (hh + 1) * width)
        if n_valid < c:
            x = ref[bi, :, cols].astype(F32)
            return jnp.concatenate([x, jnp.zeros((c - x.shape[0], x.shape[1]), F32)], axis=0)
        return ref[bi, pl.ds(pl.multiple_of(ci * c, c), c), cols]

    def chunk(ci, _):
        for bi, hh in chains:
            qc = load(q_ref, bi, hh, DK_C, ci).astype(BF16)
            kc = load(k_ref, bi, hh, DK_C, ci)
            vc = load(v_ref, bi, hh, DV_C, ci).astype(BF16)
            s = s_ref[bi, hh]
            scores = _dot_nt(qc, kc.astype(BF16)) * idec_ref[hh]
            o = _dot(scores.astype(BF16), vc) + _dot(qc, s.astype(BF16)) * qdec_ref[hh, :, 0:1]
            s_ref[bi, hh] = cdec_ref[hh, 0:1, 0:1] * s + _dot_tn((kc * kdec_ref[hh, :, 0:1]).astype(BF16), vc)
            ob = _rms(o) * load(g_ref, bi, hh, DV_C, ci)
            cols = slice(hh * DV_C, (hh + 1) * DV_C)
            if n_valid < c:
                o_ref[bi, :, cols] = ob[:n_valid].astype(o_ref.dtype)
            else:
                o_ref[bi, pl.ds(pl.multiple_of(ci * c, c), c), cols] = ob.astype(o_ref.dtype)
        return 0

    lax.fori_loop(0, n_chunks, chunk, 0)

    @pl.when(t == pl.num_programs(2) - 1)
    def _():
        sout_ref[...] = s_ref[...]


def _ret_tables(c, chunk):
    f32 = jnp.float32
    log_gamma = jnp.log1p(-jnp.exp2(-5.0 - jnp.arange(H_C, dtype=f32)))
    idx = jnp.arange(c, dtype=f32)
    real = idx < chunk
    rel = idx[:, None] - idx[None, :]
    ok = (rel >= 0) & real[:, None] & real[None, :]
    idec = jnp.exp(jnp.where(ok[None], rel[None] * log_gamma[:, None, None], -jnp.inf))
    qdec = jnp.where(real[None, :], jnp.exp((idx[None, :] + 1.0) * log_gamma[:, None]), 0.0)
    kdec = jnp.where(real[None, :], jnp.exp((chunk - 1.0 - idx[None, :]) * log_gamma[:, None]), 0.0)
    cdec = jnp.exp(chunk * log_gamma)
    bc = lambda x: jnp.broadcast_to(x[:, :, None], (H_C, c, LANES))
    return idec, bc(qdec), bc(kdec), jnp.broadcast_to(cdec[:, None, None], (H_C, 8, LANES))


def _ret(q, k, v, g, s0, batch, seq, out_dtype, tb, nb, hpg):
    n = q.shape[0]
    if seq >= RET_CHUNK:
        c = RET_CHUNK
        n_valid, rows_blk, nt, n_chunks = c, tb, seq // tb, tb // c
    else:
        c = RET_PAD_CHUNK
        n_valid, rows_blk, nt, n_chunks = seq, seq, 1, 1
    idec, qdec, kdec, cdec = _ret_tables(c, n_valid)
    tok = lambda hg, b, t: (b, t, hg)
    st = lambda hg, b, t: (b, hg, 0, 0)
    hd = lambda hg, b, t: (hg, 0, 0)
    qk_blk = pl.BlockSpec((nb, rows_blk, hpg * DK_C), tok)
    vg_blk = pl.BlockSpec((nb, rows_blk, hpg * DV_C), tok)
    st_blk = pl.BlockSpec((nb, hpg, DK_C, DV_C), st)
    r3 = lambda x: x.reshape(batch, seq, x.shape[-1])
    mix, s_out = pl.pallas_call(
        functools.partial(_ret_kernel, c=c, n_chunks=n_chunks, n_valid=n_valid, nb=nb, hpg=hpg),
        grid=(H_C // hpg, batch // nb, nt),
        in_specs=[qk_blk, qk_blk, vg_blk, vg_blk, st_blk,
                  pl.BlockSpec((hpg, c, c), hd), pl.BlockSpec((hpg, c, LANES), hd),
                  pl.BlockSpec((hpg, c, LANES), hd), pl.BlockSpec((hpg, 8, LANES), hd)],
        out_specs=[vg_blk, st_blk],
        out_shape=[jax.ShapeDtypeStruct((batch, seq, W_C), out_dtype),
                   jax.ShapeDtypeStruct((batch, H_C, DK_C, DV_C), F32)],
        scratch_shapes=[pltpu.VMEM((nb, hpg, DK_C, DV_C), F32)],
        compiler_params=_params(("parallel", "parallel", "arbitrary")),
        name="retention",
    )(r3(q), r3(k), r3(v), r3(g), s0, idec, qdec, kdec, cdec)
    return mix.reshape(n, W_C), s_out


def _outproj_odd_kernel(x_ref, m_ref, w_ref, y_ref):
    y_ref[...] = x_ref[...] + _dot(m_ref[...].astype(BF16), w_ref[...].astype(BF16))


def _outproj_odd(x, m, w, tm):
    n, d = x.shape
    row = lambda i: (i, 0)
    return pl.pallas_call(
        _outproj_odd_kernel,
        grid=(n // tm,),
        in_specs=[pl.BlockSpec((tm, d), row), pl.BlockSpec((tm, W_C), row),
                  pl.BlockSpec(w.shape, lambda i: (0, 0), pipeline_mode=pl.Buffered(1))],
        out_specs=pl.BlockSpec((tm, d), row),
        out_shape=jax.ShapeDtypeStruct((n, d), F32),
        compiler_params=_params(("parallel",)),
        name="outproj_odd",
    )(x, m, w)


def _rope_tables(pos):
    half = DK_C // 2
    inv = 1.0 / (ROPE_BASE ** jnp.linspace(0.0, 1.0, half, dtype=F32))
    ang = pos[:, None] * inv[None, :]
    return jnp.cos(ang), jnp.sin(ang)


def _ntri(n):
    return jnp.asarray(-np.tril(np.ones((n, n), np.float32)), BF16)


def kernel(x_prompt, x_sample, cache_k, cache_v, page_table, state_hgrn, state_ret, norm_g,
           w_in_even, w_out_even, q_norm_g, k_norm_g, sb_logit_bias, hgrn_lower_bounds,
           hgrn_out_norm_g, w_in_odd, w_out_odd):
    bp, tp, d = x_prompt.shape
    bs, ts, _ = x_sample.shape
    n_pool, page_size = cache_k.shape[1], cache_k.shape[2]
    past_len = page_table.shape[1] * page_size
    depth = norm_g.shape[0]
    tm_p = 512
    tm_o = 512
    tm_s = bs * ts

    yp = x_prompt.reshape(bp * tp, d)
    ys = x_sample.reshape(bs * ts, d)
    grp = jnp.asarray(np.kron(np.eye(H_A, dtype=np.float32), np.ones((HD_A, HD_A), np.float32)), BF16)
    cos_p, sin_p = _rope_tables(jnp.arange(tp, dtype=F32))
    cos_s, sin_s = _rope_tables(past_len + jnp.arange(ts, dtype=F32))
    cos_s, sin_s = jnp.tile(cos_s, (bs, 1)), jnp.tile(sin_s, (bs, 1))

    k_p, v_p, k_s, v_s, hg_p, hg_s, rt_p, rt_s = [], [], [], [], [], [], [], []
    for layer in range(depth):
        e = layer // 2
        ng = norm_g[layer].reshape(1, d)
        if layer % 2 == 0:
            w_in = w_in_even[e]
            w_out = w_out_even[e].astype(BF16)
            qg = jnp.tile(q_norm_g[e], H_A).reshape(1, W_A)
            kg = jnp.tile(k_norm_g[e], H_A).reshape(1, W_A)
            og = hgrn_out_norm_g[e].reshape(1, DV_B)
            bias = sb_logit_bias[e].astype(F32)
            q, k, v, k32, v32, ga, qb, fb, ib, gb = _inproj_even(yp, ng, w_in, qg, kg, grp, BF16, tm_p, kv_seq=tp)
            ma = _sb_prompt(q, k, v, ga, bias, _ntri(SB_TQ), bp, tp, SB_TQ)
            mb, hs = _hgrn(qb, fb, ib, gb, jnp.zeros((bp, H_B, DK_B, DV_B), F32), hgrn_lower_bounds,
                           og, bp, tp, e, BF16, 512, HGRN_NB)
            fuse_next = layer + 1 < depth
            if fuse_next:
                ng_next = norm_g[layer + 1].reshape(1, d)
                w_in_next = w_in_odd[e].astype(BF16)
                yp, *odd_in_p = _outproj_even_inproj_odd(yp, ma, mb, w_out, ng_next, w_in_next, cos_p, sin_p, BF16, tm_p)
            else:
                yp = _outproj_even(yp, ma, mb, w_out, tm_o)
            k_p.append(jnp.transpose(k32.reshape(bp, H_A, HD_A, tp), (0, 3, 1, 2)))
            v_p.append(jnp.transpose(v32.reshape(bp, H_A, HD_A, tp), (0, 3, 1, 2)))
            hg_p.append(hs)
            q, k, v, k32, v32, ga, qb, fb, ib, gb = _inproj_even(ys, ng, w_in, qg, kg, grp, F32, tm_s)
            ma = _sb_sample(q, k, v, ga, jnp.repeat(bias * LOG2E, ts).reshape(H_A * ts, 1), _ntri(page_size),
                            jnp.transpose(cache_k[e], (0, 2, 3, 1)), jnp.transpose(cache_v[e], (0, 2, 3, 1)),
                            page_table, ts, SB_PAGES_PER_STEP)
            mb, hs = _hgrn(qb, fb, ib, gb, state_hgrn[e].astype(F32), hgrn_lower_bounds, og, bs, ts, e, F32, ts,
                           HGRN_NB)
            if fuse_next:
                ys, *odd_in_s = _outproj_even_inproj_odd(ys, ma, mb, w_out, ng_next, w_in_next, cos_s, sin_s, F32, tm_s)
            else:
                ys = _outproj_even(ys, ma, mb, w_out, tm_s)
            k_s.append(k32.reshape(bs, ts, H_A, HD_A))
            v_s.append(v32.reshape(bs, ts, H_A, HD_A))
            hg_s.append(hs)
        else:
            w_out = w_out_odd[e]
            q, k, v, g = odd_in_p
            m, s = _ret(q, k, v, g, jnp.zeros((bp, H_C, DK_C, DV_C), F32), bp, tp, BF16, 512, RET_NB, RET_HPG)
            yp = _outproj_odd(yp, m, w_out, tm_o)
            rt_p.append(s)
            q, k, v, g = odd_in_s
            m, s = _ret(q, k, v, g, state_ret[e].astype(F32), bs, ts, F32, ts, RET_NB, RET_HPG)
            ys = _outproj_odd(ys, m, w_out, tm_s)
            rt_s.append(s)
    return (yp.reshape(bp, tp, d), ys.reshape(bs, ts, d), jnp.stack(k_p), jnp.stack(v_p), jnp.stack(k_s),
            jnp.stack(v_s), jnp.stack(hg_p), jnp.stack(hg_s), jnp.stack(rt_p), jnp.stack(rt_s))
```

```python
import functools
import math

import numpy as np
import jax
import jax.numpy as jnp
from jax import lax
from jax.experimental import pallas as pl
from jax.experimental.pallas import tpu as pltpu

F32 = jnp.float32
BF16 = jnp.bfloat16

H_A, HD_A = 8, 64
W_A = H_A * HD_A
H_B, DK_B, DV_B = 4, 128, 128
F_B = H_B * DK_B
W_B = H_B * DV_B
H_C, DK_C, DV_C = 4, 256, 512
QK_C = H_C * DK_C
W_C = H_C * DV_C
HGRN_CHUNK = 64
RET_CHUNK = 256
RET_PAD_CHUNK = 128
ROPE_BASE = 10000.0
NORM_EPS = 1e-6

LANES = 128
VMEM_LIMIT = 56 * 1024 * 1024
SB_TQ = 256
RET_NB, RET_HPG = 2, 2
HGRN_NB = 2
SB_RING = 3
SB_UNROLL = 8 * SB_RING
SB_PAGES_PER_STEP = 16
LOG2E = 1.4426950408889634


def _dot(a, b):
    return jnp.dot(a, b, preferred_element_type=F32)


def _dot_nt(a, b):
    return lax.dot_general(a, b, (((1,), (1,)), ((), ())), preferred_element_type=F32)


def _dot_tn(a, b):
    return lax.dot_general(a, b, (((0,), (0,)), ((), ())), preferred_element_type=F32)


def _split2(x):
    hi = x.astype(BF16)
    lo = (x - hi.astype(F32)).astype(BF16)
    return hi, lo


def _split3(x):
    hi = x.astype(BF16)
    r = x - hi.astype(F32)
    mid = r.astype(BF16)
    lo = (r - mid.astype(F32)).astype(BF16)
    return hi, mid, lo


def _sigmoid(x):
    return 1.0 / (1.0 + jnp.exp(-x))


def _silu(x):
    return x * _sigmoid(x)


def _rms(x, eps=NORM_EPS):
    return x * lax.rsqrt(jnp.mean(x * x, axis=-1, keepdims=True) + eps)


def _params(sem):
    return pltpu.CompilerParams(dimension_semantics=sem, vmem_limit_bytes=VMEM_LIMIT)


def _inproj_even_kernel(x_ref, ng_ref, w_ref, qg_ref, kg_ref, grp_ref,
                        q_ref, k_ref, v_ref, k32_ref, v32_ref, ga_ref,
                        qb_ref, fb_ref, ib_ref, gb_ref, *, kv_token_minor):
    x = x_ref[...]
    h = (_rms(x) * ng_ref[...]).astype(BF16)

    def proj(c):
        return _dot(h, w_ref[:, c * W_A:(c + 1) * W_A].astype(BF16))

    def head_norm(a, gain):
        hi, lo = _split2(a * a)
        ssum = _dot(hi, grp_ref[...]) + _dot(lo, grp_ref[...])
        return a * lax.rsqrt(ssum * (1.0 / HD_A) + NORM_EPS) * gain

    qa = head_norm(proj(0), qg_ref[...])
    q_ref[...] = (qa * (HD_A ** -0.5 * LOG2E)).astype(q_ref.dtype)
    ka = head_norm(proj(1), kg_ref[...])
    k32_ref[...] = ka.T if kv_token_minor else ka
    k_ref[...] = ka.astype(k_ref.dtype)
    va = proj(2)
    v32_ref[...] = va.T if kv_token_minor else va
    v_ref[...] = va.astype(v_ref.dtype)
    ga_ref[...] = _silu(proj(3))
    qb_ref[...] = proj(4)
    fb_ref[...] = proj(5)
    ib_ref[...] = proj(6)
    gb_ref[...] = _silu(proj(7))


def _inproj_even(x, ng, w, qg, kg, grp, act_dtype, tm, kv_seq=None):
    n, d = x.shape
    row = lambda i: (i, 0)
    const = lambda i: (0, 0)
    blk = pl.BlockSpec((tm, W_A), row)
    outs = [jax.ShapeDtypeStruct((n, W_A), act_dtype)] * 3 + [jax.ShapeDtypeStruct((n, W_A), F32)] * 7
    out_specs = [blk] * 10
    if kv_seq is not None:
        nt = kv_seq // tm
        kv_blk = pl.BlockSpec((None, W_A, tm), lambda i: (i // nt, 0, i % nt))
        kv_shape = jax.ShapeDtypeStruct((n // kv_seq, W_A, kv_seq), F32)
        outs[3:5] = [kv_shape, kv_shape]
        out_specs[3:5] = [kv_blk, kv_blk]
    return pl.pallas_call(
        functools.partial(_inproj_even_kernel, kv_token_minor=kv_seq is not None),
        grid=(n // tm,),
        in_specs=[pl.BlockSpec((tm, d), row), pl.BlockSpec((1, d), const),
                  pl.BlockSpec(w.shape, const, pipeline_mode=pl.Buffered(1)), pl.BlockSpec((1, W_A), const),
                  pl.BlockSpec((1, W_A), const), pl.BlockSpec((W_A, W_A), const)],
        out_specs=out_specs,
        out_shape=outs,
        compiler_params=_params(("parallel",)),
        name="inproj_even",
    )(x, ng, w, qg, kg, grp)


def _sb_nl(z, mask):
    neg_abs = lax.bitcast_convert_type(lax.bitcast_convert_type(z, jnp.uint32) | jnp.uint32(0x80000000), F32)
    nl = jnp.maximum(z, 0.0) + jnp.log(1.0 + jnp.exp2(neg_abs)) * LOG2E
    if mask is not None:
        nl = jnp.where(mask, nl, 0.0)
    return nl


def _sb_later(nl, ntri):
    return _dot(nl.astype(BF16), ntri)


def _sb_prompt_kernel(bias_ref, q_ref, k_ref, v_ref, ga_ref, ntri_ref, o_ref, qs_ref, acc_ref, carry_ref,
                      z_ref, zi_ref, col_ref, *, tq, nq):
    hp = pl.program_id(1)
    rows = 2 * tq
    lane = lax.broadcasted_iota(jnp.int32, (tq, LANES), 1)

    b0 = bias_ref[2 * hp] * LOG2E
    b1 = bias_ref[2 * hp + 1] * LOG2E

    def blk(i):
        return pl.ds(pl.multiple_of(i * tq, tq), tq)

    def logits(qs, kb):
        zr = _dot_nt(qs, k_ref[blk(kb), :])
        return jnp.concatenate([zr[:tq] + b0, zr[tq:] + b1], axis=0)

    r2 = lax.broadcasted_iota(jnp.int32, (rows, tq), 0)
    c2 = lax.broadcasted_iota(jnp.int32, (rows, tq), 1)
    mask = c2 < jnp.where(r2 >= tq, r2 - tq, r2)

    def stage_a(p, slot, diag):
        if diag:
            q = q_ref[blk(p[0]), :].astype(F32)
            qs = jnp.concatenate([jnp.where(lane < HD_A, q, 0.0), jnp.where(lane >= HD_A, q, 0.0)],
                                 axis=0).astype(BF16)
            qs_ref[p[0]] = qs
        else:
            qs = qs_ref[p[0]]
        z_ref[slot] = logits(qs, p[1])

    def stage_b(slot, diag):
        z = z_ref[slot]
        nl = _sb_nl(z, mask if diag else None)
        zi_ref[slot] = z - nl
        later = _sb_later(nl, ntri_ref[...])
        zi_ref[slot] += later
        col_ref[slot] = later[:, 0:1] - nl[:, 0:1]

    def stage_c(p, slot, diag):
        def own_lanes(x):
            return jnp.where(lane < HD_A, x[:tq], x[tq:])

        if diag:
            w = jnp.where(mask, jnp.exp2(zi_ref[slot]), 0.0)
            acc_ref[p[0]] = own_lanes(_dot(w.astype(BF16), v_ref[blk(p[1]), :]))
            carry_ref[p[0]] = col_ref[slot]
        else:
            carry = carry_ref[p[0]]
            w = jnp.exp2(zi_ref[slot] + carry)
            acc_ref[p[0]] += own_lanes(_dot(w.astype(BF16), v_ref[blk(p[1]), :]))
            carry_ref[p[0]] = carry + col_ref[slot]

    def pipeline(n_steps, p0, nxt, diag):
        if n_steps == 0:
            return
        stage_a(p0, 0, diag)
        stage_b(0, diag)
        if n_steps == 1:
            stage_c(p0, 0, diag)
            return
        stage_a(nxt(p0), 1, diag)
        n_steady = n_steps - 2

        def step(t, pc, pa):
            stage_c(pc, t % SB_RING, diag)
            stage_b((t + 1) % SB_RING, diag)
            stage_a(pa, (t + 2) % SB_RING, diag)
            return nxt(pc), nxt(pa)

        def body(_, st):
            pc, pa = st[:2], st[2:]
            for u in range(SB_UNROLL):
                pc, pa = step(u, pc, pa)
            return (*pc, *pa)

        n_trips = n_steady // SB_UNROLL
        st = lax.fori_loop(0, n_trips, body, (*p0, *nxt(nxt(p0))))
        pc, pa = st[:2], st[2:]
        for t in range(n_trips * SB_UNROLL, n_steady):
            pc, pa = step(t, pc, pa)
        stage_c(pc, n_steady % SB_RING, diag)
        stage_b((n_steady + 1) % SB_RING, diag)
        stage_c(nxt(pc), (n_steady + 1) % SB_RING, diag)

    def next_off_diagonal(p):
        wrap = p[1] == 0
        return jnp.where(wrap, p[0] + 1, p[0]), jnp.where(wrap, p[0], p[1] - 1)

    pipeline(nq, (jnp.int32(0), jnp.int32(0)), lambda p: (p[0] + 1, p[1] + 1), True)
    pipeline(nq * (nq - 1) // 2, (jnp.int32(1), jnp.int32(0)), next_off_diagonal, False)

    def finish(qi, _):
        o_ref[blk(qi), :] = (acc_ref[qi] * ga_ref[blk(qi), :]).astype(o_ref.dtype)
        return 0

    lax.fori_loop(0, nq, finish, 0)


def _sb_prompt(q, k, v, ga, bias, ntri, batch, seq, tq):
    n = q.shape[0]
    nq = seq // tq
    seqmap = lambda b, hp: (b, hp)
    blk = pl.BlockSpec((seq, LANES), seqmap)
    return pl.pallas_call(
        functools.partial(_sb_prompt_kernel, tq=tq, nq=nq),
        grid=(batch, H_A // 2),
        in_specs=[pl.BlockSpec(memory_space=pltpu.SMEM), blk, blk, blk, blk,
                  pl.BlockSpec((tq, tq), lambda b, hp: (0, 0))],
        out_specs=blk,
        out_shape=jax.ShapeDtypeStruct((n, W_A), BF16),
        scratch_shapes=[pltpu.VMEM((nq, 2 * tq, LANES), BF16), pltpu.VMEM((nq, tq, LANES), F32),
                        pltpu.VMEM((nq, 2 * tq, 1), F32),
                        pltpu.VMEM((SB_RING, 2 * tq, tq), F32), pltpu.VMEM((SB_RING, 2 * tq, tq), F32),
                        pltpu.VMEM((SB_RING, 2 * tq, 1), F32)],
        compiler_params=_params(("parallel", "parallel")),
        name="sb_prompt",
    )(bias, q, k, v, ga, ntri)


def _sb_sample_kernel(pt_ref, q_ref, kn_ref, vn_ref, ga_ref, bias_ref, ntri_ref, *rest, ts, pages):
    page_refs = rest[:2 * pages]
    o_ref, acc_ref, carry_ref = rest[2 * pages:]
    s = pl.program_id(1)
    rows = H_A * ts
    ps = ntri_ref.shape[0]
    q = q_ref[...]
    qpair = [jnp.concatenate([q[:, (2 * p) * HD_A:(2 * p + 1) * HD_A], q[:, (2 * p + 1) * HD_A:(2 * p + 2) * HD_A]],
                             axis=0).astype(BF16) for p in range(H_A // 2)]
    bias = bias_ref[...]
    ntri = ntri_ref[...]

    def own_rows(x, h):
        return x[(h % 2) * ts:(h % 2 + 1) * ts]

    def sweep(n, qk, pv, carry, mask):
        z = jnp.concatenate(
            [jnp.concatenate([own_rows(qk(j, h, qpair[h // 2]), h) for h in range(H_A)], axis=0)
             for j in range(n)], axis=1) + bias
        nl = _sb_nl(z, mask)
        laters = [_sb_later(nl[:, j * ps:(j + 1) * ps], ntri) for j in range(n)]
        sums = []
        for j in range(n):
            sums.append(carry + laters[j])
            carry = carry + (laters[j][:, 0:1] - nl[:, j * ps:j * ps + 1])
        w = jnp.exp2((z - nl) + jnp.concatenate(sums, axis=1))
        if mask is not None:
            w = jnp.where(mask, w, 0.0)
        w = w.astype(BF16)
        parts = [jnp.concatenate([own_rows(pv(j, h, w[(h // 2) * 2 * ts:(h // 2 + 1) * 2 * ts, j * ps:(j + 1) * ps]), h)
                                  for h in range(H_A)], axis=0) for j in range(n)]
        contrib = parts[0]
        for c in parts[1:]:
            contrib = contrib + c
        return contrib, carry

    @pl.when(s == 0)
    def _():
        pad = jnp.zeros((ps - ts, HD_A), F32)
        kn = kn_ref[...]
        vn = vn_ref[...]
        new_k = lambda h: jnp.concatenate([kn[:, h * HD_A:(h + 1) * HD_A], pad], axis=0).astype(BF16)
        new_v = lambda h: jnp.concatenate([vn[:, h * HD_A:(h + 1) * HD_A], pad], axis=0).astype(BF16)
        r2 = lax.broadcasted_iota(jnp.int32, (rows, ps), 0)
        c2 = lax.broadcasted_iota(jnp.int32, (rows, ps), 1)
        mask = c2 < lax.rem(r2, ts)
        contrib, carry = sweep(1, lambda j, h, q16: _dot_nt(q16, new_k(h)), lambda j, h, w16: _dot(w16, new_v(h)),
                               jnp.zeros((rows, 1), F32), mask)
        acc_ref[...] = contrib
        carry_ref[...] = carry

    contrib, carry = sweep(pages,
                           lambda j, h, q16: _dot(q16, page_refs[2 * j][h].astype(BF16)),
                           lambda j, h, w16: _dot_nt(w16, page_refs[2 * j + 1][h].astype(BF16)),
                           carry_ref[...], None)
    acc_ref[...] += contrib
    carry_ref[...] = carry

    @pl.when(s == pl.num_programs(1) - 1)
    def _():
        acc = acc_ref[...]
        o = jnp.concatenate([acc[h * ts:(h + 1) * ts] for h in range(H_A)], axis=-1)
        o_ref[...] = (o * ga_ref[...]).astype(o_ref.dtype)


def _sb_sample(q, kn, vn, ga, bias_rows, ntri, cache_k, cache_v, page_table, ts, pages):
    n = q.shape[0]
    nb, n_pages = page_table.shape
    ps = cache_k.shape[3]
    steps = n_pages // pages
    tok = lambda b, s, pt: (b, 0)
    const = lambda b, s, pt: (0, 0)

    def page_map(j):
        return lambda b, s, pt: (pt[b, n_pages - 1 - (s * pages + j)], 0, 0, 0)

    page_specs, page_args = [], []
    for j in range(pages):
        page_specs += [pl.BlockSpec((None, H_A, HD_A, ps), page_map(j))] * 2
        page_args += [cache_k, cache_v]
    rows = H_A * ts
    grid_spec = pltpu.PrefetchScalarGridSpec(
        num_scalar_prefetch=1,
        grid=(nb, steps),
        in_specs=[pl.BlockSpec((ts, W_A), tok)] * 4
                 + [pl.BlockSpec((rows, 1), const), pl.BlockSpec((ps, ps), const)] + page_specs,
        out_specs=pl.BlockSpec((ts, W_A), tok),
        scratch_shapes=[pltpu.VMEM((rows, HD_A), F32), pltpu.VMEM((rows, 1), F32)],
    )
    return pl.pallas_call(
        functools.partial(_sb_sample_kernel, ts=ts, pages=pages),
        grid_spec=grid_spec,
        out_shape=jax.ShapeDtypeStruct((n, W_A), F32),
        compiler_params=_params(("parallel", "arbitrary")),
        name="sb_sample",
    )(page_table, q, kn, vn, ga, bias_rows, ntri, *page_args)


def _hgrn_tables(c, n_valid):
    levels = int(math.log2(c))
    lm = np.zeros((levels, c, c), np.float32)
    r = np.arange(c)
    for l in range(levels):
        m = 2 ** l
        second = (r // m) % 2 == 1
        same = (r[:, None] // (2 * m)) == (r[None, :] // (2 * m))
        lm[l] = (same & second[:, None] & (~second)[None, :]).astype(np.float32)
    a = np.tril(np.ones((c, c), np.float32))
    a[:, n_valid:] = 0.0
    return a, lm


def _hold_mid(b, m, rowi):
    c = b.shape[0]
    if m == 1:
        return jnp.where(jnp.bitwise_and(rowi, 1) == 1, pltpu.roll(b, 1, 0), b)
    if m == 2:
        lo = jnp.concatenate([jnp.broadcast_to(b[g + 1:g + 2], (8, LANES)) for g in range(0, c, 8)], axis=0)
        hi = jnp.concatenate([jnp.broadcast_to(b[g + 5:g + 6], (8, LANES)) for g in range(0, c, 8)], axis=0)
        return jnp.where(jnp.bitwise_and(rowi, 4) == 0, lo, hi)
    return jnp.concatenate([jnp.broadcast_to(b[g + m - 1:g + m], (2 * m, LANES)) for g in range(0, c, 2 * m)], axis=0)


def _hgrn_kernel(q_ref, f_ref, i_ref, g_ref, s0_ref, lbp_ref, og_ref, a_ref, lm_ref,
                 o_ref, sout_ref, st_ref, *, c, n_chunks, n_valid, layer, nb):
    t = pl.program_id(1)
    levels = lm_ref.shape[0]
    chains = [(bi, h) for bi in range(nb) for h in range(H_B)]

    @pl.when(t == 0)
    def _():
        for bi, h in chains:
            st_ref[bi, h] = s0_ref[bi, h].T

    lbp = lbp_ref[...]
    p = jnp.exp(lbp - jnp.max(lbp, axis=0, keepdims=True))
    lb = jnp.sum(p[:layer + 1], axis=0, keepdims=True) / jnp.sum(p, axis=0, keepdims=True)
    og = og_ref[...]
    rowi = lax.broadcasted_iota(jnp.int32, (c, 1), 0)
    valid = rowi < n_valid

    def load(ref, bi, ci):
        if n_valid < c:
            x = ref[bi]
            return jnp.concatenate([x, jnp.zeros((c - x.shape[0], x.shape[1]), F32)], axis=0)
        return ref[bi, pl.ds(pl.multiple_of(ci * c, c), c), :]

    def chunk(ci, _):
        gs, kks, qqs, vvs = [], [], [], []
        for bi in range(nb):
            xf = load(f_ref, bi, ci)
            e = jnp.exp(-jnp.abs(xf))
            r = 1.0 / (1.0 + e)
            sig_pos = jnp.where(xf >= 0, r, e * r)
            sig_neg = jnp.where(xf >= 0, e * r, r)
            g = jnp.log(lb + (1.0 - lb) * sig_pos)
            kk = (1.0 - lb) * sig_neg
            if n_valid < c:
                g = jnp.where(valid, g, 0.0)
                kk = jnp.where(valid, kk, 0.0)
            gs.append(g)
            kks.append(kk)
            qqs.append(_silu(load(q_ref, bi, ci)))
            vvs.append(load(i_ref, bi, ci))
        ghi, gmid, glo = _split3(jnp.concatenate(gs, axis=1))
        a = a_ref[...]
        bcum_all = _dot(a, ghi) + _dot(a, gmid) + _dot(a, glo)
        outs = [[] for _ in range(nb)]
        for bi, h in chains:
            hl = slice(h * LANES, (h + 1) * LANES)
            el = slice((bi * H_B + h) * LANES, (bi * H_B + h + 1) * LANES)
            qq, kk, vv = qqs[bi][:, hl], kks[bi][:, hl], vvs[bi][:, hl]
            vv16 = vv.astype(BF16)
            bcum = bcum_all[:, el]
            rem = bcum[c - 1:c, :] - bcum
            st = st_ref[bi, h]
            o = _dot_nt((qq * jnp.exp(bcum)).astype(BF16), st.astype(BF16))
            scores = jnp.zeros((c, c), F32)
            for l in range(levels):
                second = jnp.bitwise_and(jnp.right_shift(rowi, l), 1) == 1
                d = bcum - _hold_mid(bcum, 2 ** l, rowi)
                x = jnp.exp(jnp.where(second, d, -d))
                scores = scores + _dot_nt((qq * x).astype(BF16), (kk * x).astype(BF16)) * lm_ref[l]
            o = o + _dot(scores.astype(BF16), vv16)
            o = o + jnp.sum(qq * kk, axis=-1, keepdims=True) * vv
            dec_last = jnp.exp(bcum[c - 1:c, :])
            st_ref[bi, h] = dec_last * st + _dot_tn(vv16, (kk * jnp.exp(rem)).astype(BF16))
            outs[bi].append(_rms(o) * og)
        for bi in range(nb):
            ob = jnp.concatenate(outs[bi], axis=1) * load(g_ref, bi, ci)
            if n_valid < c:
                o_ref[bi] = ob[:n_valid].astype(o_ref.dtype)
            else:
                o_ref[bi, pl.ds(pl.multiple_of(ci * c, c), c), :] = ob.astype(o_ref.dtype)
        return 0

    per_trip = 2 if n_chunks % 2 == 0 else 1

    def trip(i, _):
        for u in range(per_trip):
            chunk(i * per_trip + u, 0)
        return 0

    lax.fori_loop(0, n_chunks // per_trip, trip, 0)

    @pl.when(t == pl.num_programs(1) - 1)
    def _():
        for bi, h in chains:
            sout_ref[bi, h] = st_ref[bi, h].T


def _hgrn(qb, fb, ib, gb, s0, lbp, og, batch, seq, layer, out_dtype, tb, nb):
    n = qb.shape[0]
    c = HGRN_CHUNK
    if seq >= c:
        n_valid, rows_blk, nt, n_chunks = c, tb, seq // tb, tb // c
    else:
        n_valid, rows_blk, nt, n_chunks = seq, seq, 1, 1
    a_np, lm_np = _hgrn_tables(c, n_valid)
    a = jnp.asarray(a_np, BF16)
    lm = jnp.asarray(lm_np, F32)
    tok = lambda b, t: (b, t, 0)
    st = lambda b, t: (b, 0, 0, 0)
    blk = pl.BlockSpec((nb, rows_blk, W_B), tok)
    st_blk = pl.BlockSpec((nb, H_B, DK_B, DV_B), st)
    r3 = lambda x: x.reshape(batch, seq, W_B)
    mix, s_out = pl.pallas_call(
        functools.partial(_hgrn_kernel, c=c, n_chunks=n_chunks, n_valid=n_valid, layer=layer, nb=nb),
        grid=(batch // nb, nt),
        in_specs=[blk, blk, blk, blk, st_blk,
                  pl.BlockSpec(lbp.shape, lambda b, t: (0, 0)),
                  pl.BlockSpec((1, LANES), lambda b, t: (0, 0)),
                  pl.BlockSpec(a.shape, lambda b, t: (0, 0)),
                  pl.BlockSpec(lm.shape, lambda b, t: (0, 0, 0))],
        out_specs=[blk, st_blk],
        out_shape=[jax.ShapeDtypeStruct((batch, seq, W_B), out_dtype),
                   jax.ShapeDtypeStruct((batch, H_B, DK_B, DV_B), F32)],
        scratch_shapes=[pltpu.VMEM((nb, H_B, DV_B, DK_B), F32)],
        compiler_params=_params(("parallel", "arbitrary")),
        name="hgrn",
    )(r3(qb), r3(fb), r3(ib), r3(gb), s0, lbp, og, a, lm)
    return mix.reshape(n, W_B), s_out


def _outproj_even_kernel(x_ref, ma_ref, mb_ref, w_ref, y_ref):
    y = x_ref[...] + _dot(ma_ref[...].astype(BF16), w_ref[:W_A, :]) + _dot(mb_ref[...].astype(BF16), w_ref[W_A:, :])
    y_ref[...] = y


def _outproj_even(x, ma, mb, w, tm):
    n, d = x.shape
    row = lambda i: (i, 0)
    return pl.pallas_call(
        _outproj_even_kernel,
        grid=(n // tm,),
        in_specs=[pl.BlockSpec((tm, d), row), pl.BlockSpec((tm, W_A), row), pl.BlockSpec((tm, W_B), row),
                  pl.BlockSpec(w.shape, lambda i: (0, 0))],
        out_specs=pl.BlockSpec((tm, d), row),
        out_shape=jax.ShapeDtypeStruct((n, d), F32),
        compiler_params=_params(("parallel",)),
        name="outproj_even",
    )(x, ma, mb, w)


def _outproj_even_inproj_odd_kernel(x_ref, ma_ref, mb_ref, wo_ref, ng_ref, w_ref, cos_ref, sin_ref,
                                    y_ref, q_ref, k_ref, v_ref, g_ref):
    y = (x_ref[...] + _dot(ma_ref[...].astype(BF16), wo_ref[:W_A, :])
         + _dot(mb_ref[...].astype(BF16), wo_ref[W_A:, :]))
    y_ref[...] = y
    _inproj_odd_body(y, ng_ref, w_ref, cos_ref, sin_ref, q_ref, k_ref, v_ref, g_ref)


def _inproj_odd_body(x, ng_ref, w_ref, cos_ref, sin_ref, q_ref, k_ref, v_ref, g_ref):
    h = (_rms(x) * ng_ref[...]).astype(BF16)
    cos = cos_ref[...]
    sin = sin_ref[...]
    half = DK_C // 2

    def rot(a, scale):
        outs = []
        for hd in range(H_C):
            x1 = a[:, hd * DK_C:hd * DK_C + half]
            x2 = a[:, hd * DK_C + half:(hd + 1) * DK_C]
            outs += [(x1 * cos - x2 * sin) * scale, (x2 * cos + x1 * sin) * scale]
        return jnp.concatenate(outs, axis=-1)

    q_ref[...] = rot(_dot(h, w_ref[:, :QK_C]), 1.0).astype(q_ref.dtype)
    k_ref[...] = rot(_dot(h, w_ref[:, QK_C:2 * QK_C]), DK_C ** -0.5)
    for j in range(2):
        lo = 2 * QK_C + j * QK_C
        v_ref[:, j * QK_C:(j + 1) * QK_C] = _dot(h, w_ref[:, lo:lo + QK_C]).astype(v_ref.dtype)
    for j in range(2):
        lo = 2 * QK_C + W_C + j * QK_C
        g_ref[:, j * QK_C:(j + 1) * QK_C] = _silu(_dot(h, w_ref[:, lo:lo + QK_C]))


def _outproj_even_inproj_odd(x, ma, mb, wo, ng, w, cos, sin, act_dtype, tm):
    n, d = x.shape
    npos = cos.shape[0] // tm
    row = lambda i: (i, 0)
    const = lambda i: (0, 0)
    pos = lambda i: (i % npos, 0)
    return pl.pallas_call(
        _outproj_even_inproj_odd_kernel,
        grid=(n // tm,),
        in_specs=[pl.BlockSpec((tm, d), row), pl.BlockSpec((tm, W_A), row), pl.BlockSpec((tm, W_B), row),
                  pl.BlockSpec(wo.shape, const, pipeline_mode=pl.Buffered(1)), pl.BlockSpec((1, d), const),
                  pl.BlockSpec(w.shape, const, pipeline_mode=pl.Buffered(1)),
                  pl.BlockSpec((tm, DK_C // 2), pos), pl.BlockSpec((tm, DK_C // 2), pos)],
        out_specs=[pl.BlockSpec((tm, d), row), pl.BlockSpec((tm, QK_C), row), pl.BlockSpec((tm, QK_C), row),
                   pl.BlockSpec((tm, W_C), row), pl.BlockSpec((tm, W_C), row)],
        out_shape=[jax.ShapeDtypeStruct((n, d), F32),
                   jax.ShapeDtypeStruct((n, QK_C), act_dtype), jax.ShapeDtypeStruct((n, QK_C), F32),
                   jax.ShapeDtypeStruct((n, W_C), act_dtype), jax.ShapeDtypeStruct((n, W_C), F32)],
        compiler_params=_params(("parallel",)),
        name="outproj_even_inproj_odd",
    )(x, ma, mb, wo, ng, w, cos, sin)


def _ret_kernel(q_ref, k_ref, v_ref, g_ref, s0_ref, idec_ref, qdec_ref, kdec_ref, cdec_ref,
                o_ref, sout_ref, s_ref, *, c, n_chunks, n_valid, nb, hpg):
    t = pl.program_id(2)
    chains = [(bi, hh) for bi in range(nb) for hh in range(hpg)]

    @pl.when(t == 0)
    def _():
        s_ref[...] = s0_ref[...]

    def load(ref, bi, hh, width, ci):
        cols = slice(hh * width, (hh + 1) * width)
        if n_valid < c:
            x = ref[bi, :, cols].astype(F32)
            return jnp.concatenate([x, jnp.zeros((c - x.shape[0], x.shape[1]), F32)], axis=0)
        return ref[bi, pl.ds(pl.multiple_of(ci * c, c), c), cols]

    def chunk(ci, _):
        for bi, hh in chains:
            qc = load(q_ref, bi, hh, DK_C, ci).astype(BF16)
            kc = load(k_ref, bi, hh, DK_C, ci)
            vc = load(v_ref, bi, hh, DV_C, ci).astype(BF16)
            s = s_ref[bi, hh]
            scores = _dot_nt(qc, kc.astype(BF16)) * idec_ref[hh]
            o = _dot(scores.astype(BF16), vc) + _dot(qc, s.astype(BF16)) * qdec_ref[hh, :, 0:1]
            s_ref[bi, hh] = cdec_ref[hh, 0:1, 0:1] * s + _dot_tn((kc * kdec_ref[hh, :, 0:1]).astype(BF16), vc)
            ob = _rms(o) * load(g_ref, bi, hh, DV_C, ci)
            cols = slice(hh * DV_C, (hh + 1) * DV_C)
            if n_valid < c:
                o_ref[bi, :, cols] = ob[:n_valid].astype(o_ref.dtype)
            else:
                o_ref[bi, pl.ds(pl.multiple_of(ci * c, c), c), cols] = ob.astype(o_ref.dtype)
        return 0

    lax.fori_loop(0, n_chunks, chunk, 0)

    @pl.when(t == pl.num_programs(2) - 1)
    def _():
        sout_ref[...] = s_ref[...]


def _ret_tables(c, chunk):
    f32 = jnp.float32
    log_gamma = jnp.log1p(-jnp.exp2(-5.0 - jnp.arange(H_C, dtype=f32)))
    idx = jnp.arange(c, dtype=f32)
    real = idx < chunk
    rel = idx[:, None] - idx[None, :]
    ok = (rel >= 0) & real[:, None] & real[None, :]
    idec = jnp.exp(jnp.where(ok[None], rel[None] * log_gamma[:, None, None], -jnp.inf))
    qdec = jnp.where(real[None, :], jnp.exp((idx[None, :] + 1.0) * log_gamma[:, None]), 0.0)
    kdec = jnp.where(real[None, :], jnp.exp((chunk - 1.0 - idx[None, :]) * log_gamma[:, None]), 0.0)
    cdec = jnp.exp(chunk * log_gamma)
    bc = lambda x: jnp.broadcast_to(x[:, :, None], (H_C, c, LANES))
    return idec, bc(qdec), bc(kdec), jnp.broadcast_to(cdec[:, None, None], (H_C, 8, LANES))


def _ret(q, k, v, g, s0, batch, seq, out_dtype, tb, nb, hpg):
    n = q.shape[0]
    if seq >= RET_CHUNK:
        c = RET_CHUNK
        n_valid, rows_blk, nt, n_chunks = c, tb, seq // tb, tb // c
    else:
        c = RET_PAD_CHUNK
        n_valid, rows_blk, nt, n_chunks = seq, seq, 1, 1
    idec, qdec, kdec, cdec = _ret_tables(c, n_valid)
    tok = lambda hg, b, t: (b, t, hg)
    st = lambda hg, b, t: (b, hg, 0, 0)
    hd = lambda hg, b, t: (hg, 0, 0)
    qk_blk = pl.BlockSpec((nb, rows_blk, hpg * DK_C), tok)
    vg_blk = pl.BlockSpec((nb, rows_blk, hpg * DV_C), tok)
    st_blk = pl.BlockSpec((nb, hpg, DK_C, DV_C), st)
    r3 = lambda x: x.reshape(batch, seq, x.shape[-1])
    mix, s_out = pl.pallas_call(
        functools.partial(_ret_kernel, c=c, n_chunks=n_chunks, n_valid=n_valid, nb=nb, hpg=hpg),
        grid=(H_C // hpg, batch // nb, nt),
        in_specs=[qk_blk, qk_blk, vg_blk, vg_blk, st_blk,
                  pl.BlockSpec((hpg, c, c), hd), pl.BlockSpec((hpg, c, LANES), hd),
                  pl.BlockSpec((hpg, c, LANES), hd), pl.BlockSpec((hpg, 8, LANES), hd)],
        out_specs=[vg_blk, st_blk],
        out_shape=[jax.ShapeDtypeStruct((batch, seq, W_C), out_dtype),
                   jax.ShapeDtypeStruct((batch, H_C, DK_C, DV_C), F32)],
        scratch_shapes=[pltpu.VMEM((nb, hpg, DK_C, DV_C), F32)],
        compiler_params=_params(("parallel", "parallel", "arbitrary")),
        name="retention",
    )(r3(q), r3(k), r3(v), r3(g), s0, idec, qdec, kdec, cdec)
    return mix.reshape(n, W_C), s_out


def _outproj_odd_kernel(x_ref, m_ref, w_ref, y_ref):
    y_ref[...] = x_ref[...] + _dot(m_ref[...].astype(BF16), w_ref[...].astype(BF16))


def _outproj_odd(x, m, w, tm):
    n, d = x.shape
    row = lambda i: (i, 0)
    return pl.pallas_call(
        _outproj_odd_kernel,
        grid=(n // tm,),
        in_specs=[pl.BlockSpec((tm, d), row), pl.BlockSpec((tm, W_C), row),
                  pl.BlockSpec(w.shape, lambda i: (0, 0), pipeline_mode=pl.Buffered(1))],
        out_specs=pl.BlockSpec((tm, d), row),
        out_shape=jax.ShapeDtypeStruct((n, d), F32),
        compiler_params=_params(("parallel",)),
        name="outproj_odd",
    )(x, m, w)


def _rope_tables(pos):
    half = DK_C // 2
    inv = 1.0 / (ROPE_BASE ** jnp.linspace(0.0, 1.0, half, dtype=F32))
    ang = pos[:, None] * inv[None, :]
    return jnp.cos(ang), jnp.sin(ang)


def _ntri(n):
    return jnp.asarray(-np.tril(np.ones((n, n), np.float32), -1), BF16)


def kernel(x_prompt, x_sample, cache_k, cache_v, page_table, state_hgrn, state_ret, norm_g,
           w_in_even, w_out_even, q_norm_g, k_norm_g, sb_logit_bias, hgrn_lower_bounds,
           hgrn_out_norm_g, w_in_odd, w_out_odd):
    bp, tp, d = x_prompt.shape
    bs, ts, _ = x_sample.shape
    n_pool, page_size = cache_k.shape[1], cache_k.shape[2]
    past_len = page_table.shape[1] * page_size
    depth = norm_g.shape[0]
    tm_p = 512
    tm_o = 512
    tm_s = bs * ts

    yp = x_prompt.reshape(bp * tp, d)
    ys = x_sample.reshape(bs * ts, d)
    grp = jnp.asarray(np.kron(np.eye(H_A, dtype=np.float32), np.ones((HD_A, HD_A), np.float32)), BF16)
    cos_p, sin_p = _rope_tables(jnp.arange(tp, dtype=F32))
    cos_s, sin_s = _rope_tables(past_len + jnp.arange(ts, dtype=F32))
    cos_s, sin_s = jnp.tile(cos_s, (bs, 1)), jnp.tile(sin_s, (bs, 1))

    k_p, v_p, k_s, v_s, hg_p, hg_s, rt_p, rt_s = [], [], [], [], [], [], [], []
    for layer in range(depth):
        e = layer // 2
        ng = norm_g[layer].reshape(1, d)
        if layer % 2 == 0:
            w_in = w_in_even[e]
            w_out = w_out_even[e].astype(BF16)
            qg = jnp.tile(q_norm_g[e], H_A).reshape(1, W_A)
            kg = jnp.tile(k_norm_g[e], H_A).reshape(1, W_A)
            og = hgrn_out_norm_g[e].reshape(1, DV_B)
            bias = sb_logit_bias[e].astype(F32)
            q, k, v, k32, v32, ga, qb, fb, ib, gb = _inproj_even(yp, ng, w_in, qg, kg, grp, BF16, tm_p, kv_seq=tp)
            ma = _sb_prompt(q, k, v, ga, bias, _ntri(SB_TQ), bp, tp, SB_TQ)
            mb, hs = _hgrn(qb, fb, ib, gb, jnp.zeros((bp, H_B, DK_B, DV_B), F32), hgrn_lower_bounds,
                           og, bp, tp, e, BF16, 512, HGRN_NB)
            fuse_next = layer + 1 < depth
            if fuse_next:
                ng_next = norm_g[layer + 1].reshape(1, d)
                w_in_next = w_in_odd[e].astype(BF16)
                yp, *odd_in_p = _outproj_even_inproj_odd(yp, ma, mb, w_out, ng_next, w_in_next, cos_p, sin_p, BF16, tm_p)
            else:
                yp = _outproj_even(yp, ma, mb, w_out, tm_o)
            k_p.append(jnp.transpose(k32.reshape(bp, H_A, HD_A, tp), (0, 3, 1, 2)))
            v_p.append(jnp.transpose(v32.reshape(bp, H_A, HD_A, tp), (0, 3, 1, 2)))
            hg_p.append(hs)
            q, k, v, k32, v32, ga, qb, fb, ib, gb = _inproj_even(ys, ng, w_in, qg, kg, grp, F32, tm_s)
            ma = _sb_sample(q, k, v, ga, jnp.repeat(bias * LOG2E, ts).reshape(H_A * ts, 1), _ntri(page_size),
                            jnp.transpose(cache_k[e], (0, 2, 3, 1)), jnp.transpose(cache_v[e], (0, 2, 3, 1)),
                            page_table, ts, SB_PAGES_PER_STEP)
            mb, hs = _hgrn(qb, fb, ib, gb, state_hgrn[e].astype(F32), hgrn_lower_bounds, og, bs, ts, e, F32, ts,
                           HGRN_NB)
            if fuse_next:
                ys, *odd_in_s = _outproj_even_inproj_odd(ys, ma, mb, w_out, ng_next, w_in_next, cos_s, sin_s, F32, tm_s)
            else:
                ys = _outproj_even(ys, ma, mb, w_out, tm_s)
            k_s.append(k32.reshape(bs, ts, H_A, HD_A))
            v_s.append(v32.reshape(bs, ts, H_A, HD_A))
            hg_s.append(hs)
        else:
            w_out = w_out_odd[e]
            q, k, v, g = odd_in_p
            m, s = _ret(q, k, v, g, jnp.zeros((bp, H_C, DK_C, DV_C), F32), bp, tp, BF16, 512, RET_NB, RET_HPG)
            yp = _outproj_odd(yp, m, w_out, tm_o)
            rt_p.append(s)
            q, k, v, g = odd_in_s
            m, s = _ret(q, k, v, g, state_ret[e].astype(F32), bs, ts, F32, ts, RET_NB, RET_HPG)
            ys = _outproj_odd(ys, m, w_out, tm_s)
            rt_s.append(s)
    return (yp.reshape(bp, tp, d), ys.reshape(bs, ts, d), jnp.stack(k_p), jnp.stack(v_p), jnp.stack(k_s),
            jnp.stack(v_s), jnp.stack(hg_p), jnp.stack(hg_s), jnp.stack(rt_p), jnp.stack(rt_s))
```

```python
import functools
import math

import numpy as np
import jax
import jax.numpy as jnp
from jax import lax
from jax.experimental import pallas as pl
from jax.experimental.pallas import tpu as pltpu

F32 = jnp.float32
BF16 = jnp.bfloat16

H_A, HD_A = 8, 64
W_A = H_A * HD_A
H_B, DK_B, DV_B = 4, 128, 128
F_B = H_B * DK_B
W_B = H_B * DV_B
H_C, DK_C, DV_C = 4, 256, 512
QK_C = H_C * DK_C
W_C = H_C * DV_C
HGRN_CHUNK = 64
RET_CHUNK = 256
RET_PAD_CHUNK = 64
ROPE_BASE = 10000.0
NORM_EPS = 1e-6

LANES = 128
VMEM_LIMIT = 56 * 1024 * 1024
SB_TQ = 256
RET_NB, RET_HPG = 2, 2
HGRN_NB = 2
SB_RING = 3
SB_UNROLL = 8 * SB_RING
SB_PAGES_PER_STEP = 16
LOG2E = 1.4426950408889634


def _dot(a, b):
    return jnp.dot(a, b, preferred_element_type=F32)


def _dot_nt(a, b):
    return lax.dot_general(a, b, (((1,), (1,)), ((), ())), preferred_element_type=F32)


def _dot_tn(a, b):
    return lax.dot_general(a, b, (((0,), (0,)), ((), ())), preferred_element_type=F32)


def _split2(x):
    hi = x.astype(BF16)
    lo = (x - hi.astype(F32)).astype(BF16)
    return hi, lo


def _split3(x):
    hi = x.astype(BF16)
    r = x - hi.astype(F32)
    mid = r.astype(BF16)
    lo = (r - mid.astype(F32)).astype(BF16)
    return hi, mid, lo


def _sigmoid(x):
    return 1.0 / (1.0 + jnp.exp(-x))


def _silu(x):
    return x * _sigmoid(x)


def _rms(x, eps=NORM_EPS):
    return x * lax.rsqrt(jnp.mean(x * x, axis=-1, keepdims=True) + eps)


def _params(sem):
    return pltpu.CompilerParams(dimension_semantics=sem, vmem_limit_bytes=VMEM_LIMIT)


def _inproj_even_kernel(x_ref, ng_ref, w_ref, qg_ref, kg_ref, grp_ref,
                        q_ref, k_ref, v_ref, k32_ref, v32_ref, ga_ref,
                        qb_ref, fb_ref, ib_ref, gb_ref, *, kv_token_minor):
    x = x_ref[...]
    h = (_rms(x) * ng_ref[...]).astype(BF16)

    def proj(c):
        return _dot(h, w_ref[:, c * W_A:(c + 1) * W_A].astype(BF16))

    def head_norm(a, gain):
        hi, lo = _split2(a * a)
        ssum = _dot(hi, grp_ref[...]) + _dot(lo, grp_ref[...])
        return a * lax.rsqrt(ssum * (1.0 / HD_A) + NORM_EPS) * gain

    qa = head_norm(proj(0), qg_ref[...])
    q_ref[...] = (qa * (HD_A ** -0.5 * LOG2E)).astype(q_ref.dtype)
    ka = head_norm(proj(1), kg_ref[...])
    k32_ref[...] = ka.T if kv_token_minor else ka
    k_ref[...] = ka.astype(k_ref.dtype)
    va = proj(2)
    v32_ref[...] = va.T if kv_token_minor else va
    v_ref[...] = va.astype(v_ref.dtype)
    ga_ref[...] = _silu(proj(3))
    qb_ref[...] = proj(4)
    fb_ref[...] = proj(5)
    ib_ref[...] = proj(6)
    gb_ref[...] = _silu(proj(7))


def _inproj_even(x, ng, w, qg, kg, grp, act_dtype, tm, kv_seq=None):
    n, d = x.shape
    row = lambda i: (i, 0)
    const = lambda i: (0, 0)
    blk = pl.BlockSpec((tm, W_A), row)
    outs = [jax.ShapeDtypeStruct((n, W_A), act_dtype)] * 3 + [jax.ShapeDtypeStruct((n, W_A), F32)] * 7
    out_specs = [blk] * 10
    if kv_seq is not None:
        nt = kv_seq // tm
        kv_blk = pl.BlockSpec((None, W_A, tm), lambda i: (i // nt, 0, i % nt))
        kv_shape = jax.ShapeDtypeStruct((n // kv_seq, W_A, kv_seq), F32)
        outs[3:5] = [kv_shape, kv_shape]
        out_specs[3:5] = [kv_blk, kv_blk]
    return pl.pallas_call(
        functools.partial(_inproj_even_kernel, kv_token_minor=kv_seq is not None),
        grid=(n // tm,),
        in_specs=[pl.BlockSpec((tm, d), row), pl.BlockSpec((1, d), const),
                  pl.BlockSpec(w.shape, const, pipeline_mode=pl.Buffered(1)), pl.BlockSpec((1, W_A), const),
                  pl.BlockSpec((1, W_A), const), pl.BlockSpec((W_A, W_A), const)],
        out_specs=out_specs,
        out_shape=outs,
        compiler_params=_params(("parallel",)),
        name="inproj_even",
    )(x, ng, w, qg, kg, grp)


def _sb_nl(z, mask):
    neg_abs = lax.bitcast_convert_type(lax.bitcast_convert_type(z, jnp.uint32) | jnp.uint32(0x80000000), F32)
    nl = jnp.maximum(z, 0.0) + jnp.log(1.0 + jnp.exp2(neg_abs)) * LOG2E
    if mask is not None:
        nl = jnp.where(mask, nl, 0.0)
    return nl


def _sb_later(nl, ntri):
    return _dot(nl.astype(BF16), ntri)


def _sb_prompt_kernel(bias_ref, q_ref, k_ref, v_ref, ga_ref, ntri_ref, o_ref, qs_ref, acc_ref, carry_ref,
                      z_ref, zi_ref, col_ref, *, tq, nq):
    hp = pl.program_id(1)
    rows = 2 * tq
    lane = lax.broadcasted_iota(jnp.int32, (tq, LANES), 1)

    b0 = bias_ref[2 * hp] * LOG2E
    b1 = bias_ref[2 * hp + 1] * LOG2E

    def blk(i):
        return pl.ds(pl.multiple_of(i * tq, tq), tq)

    def logits(qs, kb):
        zr = _dot_nt(qs, k_ref[blk(kb), :])
        return jnp.concatenate([zr[:tq] + b0, zr[tq:] + b1], axis=0)

    r2 = lax.broadcasted_iota(jnp.int32, (rows, tq), 0)
    c2 = lax.broadcasted_iota(jnp.int32, (rows, tq), 1)
    mask = c2 < jnp.where(r2 >= tq, r2 - tq, r2)

    def stage_a(p, slot, diag):
        if diag:
            q = q_ref[blk(p[0]), :].astype(F32)
            qs = jnp.concatenate([jnp.where(lane < HD_A, q, 0.0), jnp.where(lane >= HD_A, q, 0.0)],
                                 axis=0).astype(BF16)
            qs_ref[p[0]] = qs
        else:
            qs = qs_ref[p[0]]
        z_ref[slot] = logits(qs, p[1])

    def stage_b(slot, diag):
        z = z_ref[slot]
        nl = _sb_nl(z, mask if diag else None)
        zi_ref[slot] = z - nl
        later = _sb_later(nl, ntri_ref[...])
        zi_ref[slot] += later
        col_ref[slot] = later[:, 0:1] - nl[:, 0:1]

    def stage_c(p, slot, diag):
        def own_lanes(x):
            return jnp.where(lane < HD_A, x[:tq], x[tq:])

        if diag:
            w = jnp.where(mask, jnp.exp2(zi_ref[slot]), 0.0)
            acc_ref[p[0]] = own_lanes(_dot(w.astype(BF16), v_ref[blk(p[1]), :]))
            carry_ref[p[0]] = col_ref[slot]
        else:
            carry = carry_ref[p[0]]
            w = jnp.exp2(zi_ref[slot] + carry)
            acc_ref[p[0]] += own_lanes(_dot(w.astype(BF16), v_ref[blk(p[1]), :]))
            carry_ref[p[0]] = carry + col_ref[slot]

    def pipeline(n_steps, p0, nxt, diag):
        if n_steps == 0:
            return
        stage_a(p0, 0, diag)
        stage_b(0, diag)
        if n_steps == 1:
            stage_c(p0, 0, diag)
            return
        stage_a(nxt(p0), 1, diag)
        n_steady = n_steps - 2

        def step(t, pc, pa):
            stage_c(pc, t % SB_RING, diag)
            stage_b((t + 1) % SB_RING, diag)
            stage_a(pa, (t + 2) % SB_RING, diag)
            return nxt(pc), nxt(pa)

        def body(_, st):
            pc, pa = st[:2], st[2:]
            for u in range(SB_UNROLL):
                pc, pa = step(u, pc, pa)
            return (*pc, *pa)

        n_trips = n_steady // SB_UNROLL
        st = lax.fori_loop(0, n_trips, body, (*p0, *nxt(nxt(p0))))
        pc, pa = st[:2], st[2:]
        for t in range(n_trips * SB_UNROLL, n_steady):
            pc, pa = step(t, pc, pa)
        stage_c(pc, n_steady % SB_RING, diag)
        stage_b((n_steady + 1) % SB_RING, diag)
        stage_c(nxt(pc), (n_steady + 1) % SB_RING, diag)

    def next_off_diagonal(p):
        wrap = p[1] == 0
        return jnp.where(wrap, p[0] + 1, p[0]), jnp.where(wrap, p[0], p[1] - 1)

    pipeline(nq, (jnp.int32(0), jnp.int32(0)), lambda p: (p[0] + 1, p[1] + 1), True)
    pipeline(nq * (nq - 1) // 2, (jnp.int32(1), jnp.int32(0)), next_off_diagonal, False)

    def finish(qi, _):
        o_ref[blk(qi), :] = (acc_ref[qi] * ga_ref[blk(qi), :]).astype(o_ref.dtype)
        return 0

    lax.fori_loop(0, nq, finish, 0)


def _sb_prompt(q, k, v, ga, bias, ntri, batch, seq, tq):
    n = q.shape[0]
    nq = seq // tq
    seqmap = lambda b, hp: (b, hp)
    blk = pl.BlockSpec((seq, LANES), seqmap)
    return pl.pallas_call(
        functools.partial(_sb_prompt_kernel, tq=tq, nq=nq),
        grid=(batch, H_A // 2),
        in_specs=[pl.BlockSpec(memory_space=pltpu.SMEM), blk, blk, blk, blk,
                  pl.BlockSpec((tq, tq), lambda b, hp: (0, 0))],
        out_specs=blk,
        out_shape=jax.ShapeDtypeStruct((n, W_A), BF16),
        scratch_shapes=[pltpu.VMEM((nq, 2 * tq, LANES), BF16), pltpu.VMEM((nq, tq, LANES), F32),
                        pltpu.VMEM((nq, 2 * tq, 1), F32),
                        pltpu.VMEM((SB_RING, 2 * tq, tq), F32), pltpu.VMEM((SB_RING, 2 * tq, tq), F32),
                        pltpu.VMEM((SB_RING, 2 * tq, 1), F32)],
        compiler_params=_params(("parallel", "parallel")),
        name="sb_prompt",
    )(bias, q, k, v, ga, ntri)


def _sb_sample_kernel(pt_ref, q_ref, kn_ref, vn_ref, ga_ref, bias_ref, ntri_ref, *rest, ts, pages):
    page_refs = rest[:2 * pages]
    o_ref, acc_ref, carry_ref = rest[2 * pages:]
    s = pl.program_id(1)
    rows = H_A * ts
    ps = ntri_ref.shape[0]
    q = q_ref[...]
    qpair = [jnp.concatenate([q[:, (2 * p) * HD_A:(2 * p + 1) * HD_A], q[:, (2 * p + 1) * HD_A:(2 * p + 2) * HD_A]],
                             axis=0).astype(BF16) for p in range(H_A // 2)]
    bias = bias_ref[...]
    ntri = ntri_ref[...]

    def own_rows(x, h):
        return x[(h % 2) * ts:(h % 2 + 1) * ts]

    def sweep(n, qk, pv, carry, mask):
        z = jnp.concatenate(
            [jnp.concatenate([own_rows(qk(j, h, qpair[h // 2]), h) for h in range(H_A)], axis=0)
             for j in range(n)], axis=1) + bias
        nl = _sb_nl(z, mask)
        laters = [_sb_later(nl[:, j * ps:(j + 1) * ps], ntri) for j in range(n)]
        sums = []
        for j in range(n):
            sums.append(carry + laters[j])
            carry = carry + (laters[j][:, 0:1] - nl[:, j * ps:j * ps + 1])
        w = jnp.exp2((z - nl) + jnp.concatenate(sums, axis=1))
        if mask is not None:
            w = jnp.where(mask, w, 0.0)
        w = w.astype(BF16)
        parts = [jnp.concatenate([own_rows(pv(j, h, w[(h // 2) * 2 * ts:(h // 2 + 1) * 2 * ts, j * ps:(j + 1) * ps]), h)
                                  for h in range(H_A)], axis=0) for j in range(n)]
        contrib = parts[0]
        for c in parts[1:]:
            contrib = contrib + c
        return contrib, carry

    @pl.when(s == 0)
    def _():
        pad = jnp.zeros((ps - ts, HD_A), F32)
        kn = kn_ref[...]
        vn = vn_ref[...]
        new_k = lambda h: jnp.concatenate([kn[:, h * HD_A:(h + 1) * HD_A], pad], axis=0).astype(BF16)
        new_v = lambda h: jnp.concatenate([vn[:, h * HD_A:(h + 1) * HD_A], pad], axis=0).astype(BF16)
        r2 = lax.broadcasted_iota(jnp.int32, (rows, ps), 0)
        c2 = lax.broadcasted_iota(jnp.int32, (rows, ps), 1)
        mask = c2 < lax.rem(r2, ts)
        contrib, carry = sweep(1, lambda j, h, q16: _dot_nt(q16, new_k(h)), lambda j, h, w16: _dot(w16, new_v(h)),
                               jnp.zeros((rows, 1), F32), mask)
        acc_ref[...] = contrib
        carry_ref[...] = carry

    contrib, carry = sweep(pages,
                           lambda j, h, q16: _dot(q16, page_refs[2 * j][h].astype(BF16)),
                           lambda j, h, w16: _dot_nt(w16, page_refs[2 * j + 1][h].astype(BF16)),
                           carry_ref[...], None)
    acc_ref[...] += contrib
    carry_ref[...] = carry

    @pl.when(s == pl.num_programs(1) - 1)
    def _():
        acc = acc_ref[...]
        o = jnp.concatenate([acc[h * ts:(h + 1) * ts] for h in range(H_A)], axis=-1)
        o_ref[...] = (o * ga_ref[...]).astype(o_ref.dtype)


def _sb_sample(q, kn, vn, ga, bias_rows, ntri, cache_k, cache_v, page_table, ts, pages):
    n = q.shape[0]
    nb, n_pages = page_table.shape
    ps = cache_k.shape[3]
    steps = n_pages // pages
    tok = lambda b, s, pt: (b, 0)
    const = lambda b, s, pt: (0, 0)

    def page_map(j):
        return lambda b, s, pt: (pt[b, n_pages - 1 - (s * pages + j)], 0, 0, 0)

    page_specs, page_args = [], []
    for j in range(pages):
        page_specs += [pl.BlockSpec((None, H_A, HD_A, ps), page_map(j))] * 2
        page_args += [cache_k, cache_v]
    rows = H_A * ts
    grid_spec = pltpu.PrefetchScalarGridSpec(
        num_scalar_prefetch=1,
        grid=(nb, steps),
        in_specs=[pl.BlockSpec((ts, W_A), tok)] * 4
                 + [pl.BlockSpec((rows, 1), const), pl.BlockSpec((ps, ps), const)] + page_specs,
        out_specs=pl.BlockSpec((ts, W_A), tok),
        scratch_shapes=[pltpu.VMEM((rows, HD_A), F32), pltpu.VMEM((rows, 1), F32)],
    )
    return pl.pallas_call(
        functools.partial(_sb_sample_kernel, ts=ts, pages=pages),
        grid_spec=grid_spec,
        out_shape=jax.ShapeDtypeStruct((n, W_A), F32),
        compiler_params=_params(("parallel", "arbitrary")),
        name="sb_sample",
    )(page_table, q, kn, vn, ga, bias_rows, ntri, *page_args)


def _hgrn_tables(c, n_valid):
    levels = math.ceil(math.log2(n_valid))
    lm = np.zeros((levels, c, c), np.float32)
    r = np.arange(c)
    for l in range(levels):
        m = 2 ** l
        second = (r // m) % 2 == 1
        same = (r[:, None] // (2 * m)) == (r[None, :] // (2 * m))
        lm[l] = (same & second[:, None] & (~second)[None, :]).astype(np.float32)
    a = np.tril(np.ones((c, c), np.float32))
    a[:, n_valid:] = 0.0
    return a, lm


def _hold_mid(b, m, rowi):
    c = b.shape[0]
    if m == 1:
        return jnp.where(jnp.bitwise_and(rowi, 1) == 1, pltpu.roll(b, 1, 0), b)
    if m == 2:
        lo = jnp.concatenate([jnp.broadcast_to(b[g + 1:g + 2], (8, LANES)) for g in range(0, c, 8)], axis=0)
        hi = jnp.concatenate([jnp.broadcast_to(b[g + 5:g + 6], (8, LANES)) for g in range(0, c, 8)], axis=0)
        return jnp.where(jnp.bitwise_and(rowi, 4) == 0, lo, hi)
    return jnp.concatenate([jnp.broadcast_to(b[g + m - 1:g + m], (2 * m, LANES)) for g in range(0, c, 2 * m)], axis=0)


def _hgrn_kernel(q_ref, f_ref, i_ref, g_ref, s0_ref, lbp_ref, og_ref, a_ref, lm_ref,
                 o_ref, sout_ref, st_ref, *, c, n_chunks, n_valid, layer, nb):
    t = pl.program_id(1)
    levels = lm_ref.shape[0]
    chains = [(bi, h) for bi in range(nb) for h in range(H_B)]

    @pl.when(t == 0)
    def _():
        for bi, h in chains:
            st_ref[bi, h] = s0_ref[bi, h].T

    lbp = lbp_ref[...]
    p = jnp.exp(lbp - jnp.max(lbp, axis=0, keepdims=True))
    lb = jnp.sum(p[:layer + 1], axis=0, keepdims=True) / jnp.sum(p, axis=0, keepdims=True)
    og = og_ref[...]
    rowi = lax.broadcasted_iota(jnp.int32, (c, 1), 0)
    valid = rowi < n_valid

    def load(ref, bi, ci):
        if n_valid < c:
            x = ref[bi]
            return jnp.concatenate([x, jnp.zeros((c - x.shape[0], x.shape[1]), F32)], axis=0)
        return ref[bi, pl.ds(pl.multiple_of(ci * c, c), c), :]

    def chunk(ci, _):
        gs, kks, qqs, vvs = [], [], [], []
        for bi in range(nb):
            xf = load(f_ref, bi, ci)
            e = jnp.exp(-jnp.abs(xf))
            r = 1.0 / (1.0 + e)
            sig_pos = jnp.where(xf >= 0, r, e * r)
            sig_neg = jnp.where(xf >= 0, e * r, r)
            g = jnp.log(lb + (1.0 - lb) * sig_pos)
            kk = (1.0 - lb) * sig_neg
            if n_valid < c:
                g = jnp.where(valid, g, 0.0)
                kk = jnp.where(valid, kk, 0.0)
            gs.append(g)
            kks.append(kk)
            qqs.append(_silu(load(q_ref, bi, ci)))
            vvs.append(load(i_ref, bi, ci))
        ghi, gmid, glo = _split3(jnp.concatenate(gs, axis=1))
        a = a_ref[...]
        bcum_all = _dot(a, ghi) + _dot(a, gmid) + _dot(a, glo)
        outs = [[] for _ in range(nb)]
        for bi, h in chains:
            hl = slice(h * LANES, (h + 1) * LANES)
            el = slice((bi * H_B + h) * LANES, (bi * H_B + h + 1) * LANES)
            qq, kk, vv = qqs[bi][:, hl], kks[bi][:, hl], vvs[bi][:, hl]
            vv16 = vv.astype(BF16)
            bcum = bcum_all[:, el]
            rem = bcum[c - 1:c, :] - bcum
            st = st_ref[bi, h]
            o = _dot_nt((qq * jnp.exp(bcum)).astype(BF16), st.astype(BF16))
            scores = jnp.zeros((c, c), F32)
            for l in range(levels):
                second = jnp.bitwise_and(jnp.right_shift(rowi, l), 1) == 1
                d = bcum - _hold_mid(bcum, 2 ** l, rowi)
                x = jnp.exp(jnp.where(second, d, -d))
                scores = scores + _dot_nt((qq * x).astype(BF16), (kk * x).astype(BF16)) * lm_ref[l]
            o = o + _dot(scores.astype(BF16), vv16)
            o = o + jnp.sum(qq * kk, axis=-1, keepdims=True) * vv
            dec_last = jnp.exp(bcum[c - 1:c, :])
            st_ref[bi, h] = dec_last * st + _dot_tn(vv16, (kk * jnp.exp(rem)).astype(BF16))
            outs[bi].append(_rms(o) * og)
        for bi in range(nb):
            ob = jnp.concatenate(outs[bi], axis=1) * load(g_ref, bi, ci)
            if n_valid < c:
                o_ref[bi] = ob[:n_valid].astype(o_ref.dtype)
            else:
                o_ref[bi, pl.ds(pl.multiple_of(ci * c, c), c), :] = ob.astype(o_ref.dtype)
        return 0

    per_trip = 2 if n_chunks % 2 == 0 else 1

    def trip(i, _):
        for u in range(per_trip):
            chunk(i * per_trip + u, 0)
        return 0

    lax.fori_loop(0, n_chunks // per_trip, trip, 0)

    @pl.when(t == pl.num_programs(1) - 1)
    def _():
        for bi, h in chains:
            sout_ref[bi, h] = st_ref[bi, h].T


def _hgrn(qb, fb, ib, gb, s0, lbp, og, batch, seq, layer, out_dtype, tb, nb):
    n = qb.shape[0]
    c = HGRN_CHUNK
    if seq >= c:
        n_valid, rows_blk, nt, n_chunks = c, tb, seq // tb, tb // c
    else:
        n_valid, rows_blk, nt, n_chunks = seq, seq, 1, 1
    a_np, lm_np = _hgrn_tables(c, n_valid)
    a = jnp.asarray(a_np, BF16)
    lm = jnp.asarray(lm_np, F32)
    tok = lambda b, t: (b, t, 0)
    st = lambda b, t: (b, 0, 0, 0)
    blk = pl.BlockSpec((nb, rows_blk, W_B), tok)
    st_blk = pl.BlockSpec((nb, H_B, DK_B, DV_B), st)
    r3 = lambda x: x.reshape(batch, seq, W_B)
    mix, s_out = pl.pallas_call(
        functools.partial(_hgrn_kernel, c=c, n_chunks=n_chunks, n_valid=n_valid, layer=layer, nb=nb),
        grid=(batch // nb, nt),
        in_specs=[blk, blk, blk, blk, st_blk,
                  pl.BlockSpec(lbp.shape, lambda b, t: (0, 0)),
                  pl.BlockSpec((1, LANES), lambda b, t: (0, 0)),
                  pl.BlockSpec(a.shape, lambda b, t: (0, 0)),
                  pl.BlockSpec(lm.shape, lambda b, t: (0, 0, 0))],
        out_specs=[blk, st_blk],
        out_shape=[jax.ShapeDtypeStruct((batch, seq, W_B), out_dtype),
                   jax.ShapeDtypeStruct((batch, H_B, DK_B, DV_B), F32)],
        scratch_shapes=[pltpu.VMEM((nb, H_B, DV_B, DK_B), F32)],
        compiler_params=_params(("parallel", "arbitrary")),
        name="hgrn",
    )(r3(qb), r3(fb), r3(ib), r3(gb), s0, lbp, og, a, lm)
    return mix.reshape(n, W_B), s_out


def _outproj_even_kernel(x_ref, ma_ref, mb_ref, w_ref, y_ref):
    y = x_ref[...] + _dot(ma_ref[...].astype(BF16), w_ref[:W_A, :]) + _dot(mb_ref[...].astype(BF16), w_ref[W_A:, :])
    y_ref[...] = y


def _outproj_even(x, ma, mb, w, tm):
    n, d = x.shape
    row = lambda i: (i, 0)
    return pl.pallas_call(
        _outproj_even_kernel,
        grid=(n // tm,),
        in_specs=[pl.BlockSpec((tm, d), row), pl.BlockSpec((tm, W_A), row), pl.BlockSpec((tm, W_B), row),
                  pl.BlockSpec(w.shape, lambda i: (0, 0))],
        out_specs=pl.BlockSpec((tm, d), row),
        out_shape=jax.ShapeDtypeStruct((n, d), F32),
        compiler_params=_params(("parallel",)),
        name="outproj_even",
    )(x, ma, mb, w)


def _outproj_even_inproj_odd_kernel(x_ref, ma_ref, mb_ref, wo_ref, ng_ref, w_ref, cos_ref, sin_ref,
                                    y_ref, q_ref, k_ref, v_ref, g_ref):
    y = (x_ref[...] + _dot(ma_ref[...].astype(BF16), wo_ref[:W_A, :])
         + _dot(mb_ref[...].astype(BF16), wo_ref[W_A:, :]))
    y_ref[...] = y
    _inproj_odd_body(y, ng_ref, w_ref, cos_ref, sin_ref, q_ref, k_ref, v_ref, g_ref)


def _inproj_odd_body(x, ng_ref, w_ref, cos_ref, sin_ref, q_ref, k_ref, v_ref, g_ref):
    h = (_rms(x) * ng_ref[...]).astype(BF16)
    cos = cos_ref[...]
    sin = sin_ref[...]
    half = DK_C // 2

    def rot(a, scale):
        outs = []
        for hd in range(H_C):
            x1 = a[:, hd * DK_C:hd * DK_C + half]
            x2 = a[:, hd * DK_C + half:(hd + 1) * DK_C]
            outs += [(x1 * cos - x2 * sin) * scale, (x2 * cos + x1 * sin) * scale]
        return jnp.concatenate(outs, axis=-1)

    q_ref[...] = rot(_dot(h, w_ref[:, :QK_C]), 1.0).astype(q_ref.dtype)
    k_ref[...] = rot(_dot(h, w_ref[:, QK_C:2 * QK_C]), DK_C ** -0.5)
    for j in range(2):
        lo = 2 * QK_C + j * QK_C
        v_ref[:, j * QK_C:(j + 1) * QK_C] = _dot(h, w_ref[:, lo:lo + QK_C]).astype(v_ref.dtype)
    for j in range(2):
        lo = 2 * QK_C + W_C + j * QK_C
        g_ref[:, j * QK_C:(j + 1) * QK_C] = _silu(_dot(h, w_ref[:, lo:lo + QK_C]))


def _outproj_even_inproj_odd(x, ma, mb, wo, ng, w, cos, sin, act_dtype, tm):
    n, d = x.shape
    npos = cos.shape[0] // tm
    row = lambda i: (i, 0)
    const = lambda i: (0, 0)
    pos = lambda i: (i % npos, 0)
    return pl.pallas_call(
        _outproj_even_inproj_odd_kernel,
        grid=(n // tm,),
        in_specs=[pl.BlockSpec((tm, d), row), pl.BlockSpec((tm, W_A), row), pl.BlockSpec((tm, W_B), row),
                  pl.BlockSpec(wo.shape, const, pipeline_mode=pl.Buffered(1)), pl.BlockSpec((1, d), const),
                  pl.BlockSpec(w.shape, const, pipeline_mode=pl.Buffered(1)),
                  pl.BlockSpec((tm, DK_C // 2), pos), pl.BlockSpec((tm, DK_C // 2), pos)],
        out_specs=[pl.BlockSpec((tm, d), row), pl.BlockSpec((tm, QK_C), row), pl.BlockSpec((tm, QK_C), row),
                   pl.BlockSpec((tm, W_C), row), pl.BlockSpec((tm, W_C), row)],
        out_shape=[jax.ShapeDtypeStruct((n, d), F32),
                   jax.ShapeDtypeStruct((n, QK_C), act_dtype), jax.ShapeDtypeStruct((n, QK_C), F32),
                   jax.ShapeDtypeStruct((n, W_C), act_dtype), jax.ShapeDtypeStruct((n, W_C), F32)],
        compiler_params=_params(("parallel",)),
        name="outproj_even_inproj_odd",
    )(x, ma, mb, wo, ng, w, cos, sin)


def _ret_kernel(q_ref, k_ref, v_ref, g_ref, s0_ref, idec_ref, qdec_ref, kdec_ref, cdec_ref,
                o_ref, sout_ref, s_ref, *, c, n_chunks, n_valid, nb, hpg):
    t = pl.program_id(2)
    chains = [(bi, hh) for bi in range(nb) for hh in range(hpg)]

    @pl.when(t == 0)
    def _():
        s_ref[...] = s0_ref[...]

    def load(ref, bi, hh, width, ci):
        cols = slice(hh * width, (hh + 1) * width)
        if n_valid < c:
            x = ref[bi, :, cols].astype(F32)
            return jnp.concatenate([x, jnp.zeros((c - x.shape[0], x.shape[1]), F32)], axis=0)
        return ref[bi, pl.ds(pl.multiple_of(ci * c, c), c), cols]

    def chunk(ci, _):
        for bi, hh in chains:
            qc = load(q_ref, bi, hh, DK_C, ci).astype(BF16)
            kc = load(k_ref, bi, hh, DK_C, ci)
            vc = load(v_ref, bi, hh, DV_C, ci).astype(BF16)
            s = s_ref[bi, hh]
            scores = _dot_nt(qc, kc.astype(BF16)) * idec_ref[hh]
            o = _dot(scores.astype(BF16), vc) + _dot(qc, s.astype(BF16)) * qdec_ref[hh, :, 0:1]
            s_ref[bi, hh] = cdec_ref[hh, 0:1, 0:1] * s + _dot_tn((kc * kdec_ref[hh, :, 0:1]).astype(BF16), vc)
            ob = _rms(o) * load(g_ref, bi, hh, DV_C, ci)
            cols = slice(hh * DV_C, (hh + 1) * DV_C)
            if n_valid < c:
                o_ref[bi, :, cols] = ob[:n_valid].astype(o_ref.dtype)
            else:
                o_ref[bi, pl.ds(pl.multiple_of(ci * c, c), c), cols] = ob.astype(o_ref.dtype)
        return 0

    lax.fori_loop(0, n_chunks, chunk, 0)

    @pl.when(t == pl.num_programs(2) - 1)
    def _():
        sout_ref[...] = s_ref[...]


def _ret_tables(c, chunk):
    f32 = jnp.float32
    log_gamma = jnp.log1p(-jnp.exp2(-5.0 - jnp.arange(H_C, dtype=f32)))
    idx = jnp.arange(c, dtype=f32)
    real = idx < chunk
    rel = idx[:, None] - idx[None, :]
    ok = (rel >= 0) & real[:, None] & real[None, :]
    idec = jnp.exp(jnp.where(ok[None], rel[None] * log_gamma[:, None, None], -jnp.inf))
    qdec = jnp.where(real[None, :], jnp.exp((idx[None, :] + 1.0) * log_gamma[:, None]), 0.0)
    kdec = jnp.where(real[None, :], jnp.exp((chunk - 1.0 - idx[None, :]) * log_gamma[:, None]), 0.0)
    cdec = jnp.exp(chunk * log_gamma)
    bc = lambda x: jnp.broadcast_to(x[:, :, None], (H_C, c, LANES))
    return idec, bc(qdec), bc(kdec), jnp.broadcast_to(cdec[:, None, None], (H_C, 8, LANES))


def _ret(q, k, v, g, s0, batch, seq, out_dtype, tb, nb, hpg):
    n = q.shape[0]
    if seq >= RET_CHUNK:
        c = RET_CHUNK
        n_valid, rows_blk, nt, n_chunks = c, tb, seq // tb, tb // c
    else:
        c = RET_PAD_CHUNK
        n_valid, rows_blk, nt, n_chunks = seq, seq, 1, 1
    idec, qdec, kdec, cdec = _ret_tables(c, n_valid)
    tok = lambda hg, b, t: (b, t, hg)
    st = lambda hg, b, t: (b, hg, 0, 0)
    hd = lambda hg, b, t: (hg, 0, 0)
    qk_blk = pl.BlockSpec((nb, rows_blk, hpg * DK_C), tok)
    vg_blk = pl.BlockSpec((nb, rows_blk, hpg * DV_C), tok)
    st_blk = pl.BlockSpec((nb, hpg, DK_C, DV_C), st)
    r3 = lambda x: x.reshape(batch, seq, x.shape[-1])
    mix, s_out = pl.pallas_call(
        functools.partial(_ret_kernel, c=c, n_chunks=n_chunks, n_valid=n_valid, nb=nb, hpg=hpg),
        grid=(H_C // hpg, batch // nb, nt),
        in_specs=[qk_blk, qk_blk, vg_blk, vg_blk, st_blk,
                  pl.BlockSpec((hpg, c, c), hd), pl.BlockSpec((hpg, c, LANES), hd),
                  pl.BlockSpec((hpg, c, LANES), hd), pl.BlockSpec((hpg, 8, LANES), hd)],
        out_specs=[vg_blk, st_blk],
        out_shape=[jax.ShapeDtypeStruct((batch, seq, W_C), out_dtype),
                   jax.ShapeDtypeStruct((batch, H_C, DK_C, DV_C), F32)],
        scratch_shapes=[pltpu.VMEM((nb, hpg, DK_C, DV_C), F32)],
        compiler_params=_params(("parallel", "parallel", "arbitrary")),
        name="retention",
    )(r3(q), r3(k), r3(v), r3(g), s0, idec, qdec, kdec, cdec)
    return mix.reshape(n, W_C), s_out


def _outproj_odd_kernel(x_ref, m_ref, w_ref, y_ref):
    y_ref[...] = x_ref[...] + _dot(m_ref[...].astype(BF16), w_ref[...].astype(BF16))


def _outproj_odd(x, m, w, tm):
    n, d = x.shape
    row = lambda i: (i, 0)
    return pl.pallas_call(
        _outproj_odd_kernel,
        grid=(n // tm,),
        in_specs=[pl.BlockSpec((tm, d), row), pl.BlockSpec((tm, W_C), row),
                  pl.BlockSpec(w.shape, lambda i: (0, 0), pipeline_mode=pl.Buffered(1))],
        out_specs=pl.BlockSpec((tm, d), row),
        out_shape=jax.ShapeDtypeStruct((n, d), F32),
        compiler_params=_params(("parallel",)),
        name="outproj_odd",
    )(x, m, w)


def _rope_tables(pos):
    half = DK_C // 2
    inv = 1.0 / (ROPE_BASE ** jnp.linspace(0.0, 1.0, half, dtype=F32))
    ang = pos[:, None] * inv[None, :]
    return jnp.cos(ang), jnp.sin(ang)


def _ntri(n):
    return jnp.asarray(-np.tril(np.ones((n, n), np.float32), -1), BF16)


def kernel(x_prompt, x_sample, cache_k, cache_v, page_table, state_hgrn, state_ret, norm_g,
           w_in_even, w_out_even, q_norm_g, k_norm_g, sb_logit_bias, hgrn_lower_bounds,
           hgrn_out_norm_g, w_in_odd, w_out_odd):
    bp, tp, d = x_prompt.shape
    bs, ts, _ = x_sample.shape
    n_pool, page_size = cache_k.shape[1], cache_k.shape[2]
    past_len = page_table.shape[1] * page_size
    depth = norm_g.shape[0]
    tm_p = 512
    tm_o = 512
    tm_s = bs * ts

    yp = x_prompt.reshape(bp * tp, d)
    ys = x_sample.reshape(bs * ts, d)
    grp = jnp.asarray(np.kron(np.eye(H_A, dtype=np.float32), np.ones((HD_A, HD_A), np.float32)), BF16)
    cos_p, sin_p = _rope_tables(jnp.arange(tp, dtype=F32))
    cos_s, sin_s = _rope_tables(past_len + jnp.arange(ts, dtype=F32))
    cos_s, sin_s = jnp.tile(cos_s, (bs, 1)), jnp.tile(sin_s, (bs, 1))

    k_p, v_p, k_s, v_s, hg_p, hg_s, rt_p, rt_s = [], [], [], [], [], [], [], []
    for layer in range(depth):
        e = layer // 2
        ng = norm_g[layer].reshape(1, d)
        if layer % 2 == 0:
            w_in = w_in_even[e]
            w_out = w_out_even[e].astype(BF16)
            qg = jnp.tile(q_norm_g[e], H_A).reshape(1, W_A)
            kg = jnp.tile(k_norm_g[e], H_A).reshape(1, W_A)
            og = hgrn_out_norm_g[e].reshape(1, DV_B)
            bias = sb_logit_bias[e].astype(F32)
            q, k, v, k32, v32, ga, qb, fb, ib, gb = _inproj_even(yp, ng, w_in, qg, kg, grp, BF16, tm_p, kv_seq=tp)
            ma = _sb_prompt(q, k, v, ga, bias, _ntri(SB_TQ), bp, tp, SB_TQ)
            mb, hs = _hgrn(qb, fb, ib, gb, jnp.zeros((bp, H_B, DK_B, DV_B), F32), hgrn_lower_bounds,
                           og, bp, tp, e, BF16, 512, HGRN_NB)
            fuse_next = layer + 1 < depth
            if fuse_next:
                ng_next = norm_g[layer + 1].reshape(1, d)
                w_in_next = w_in_odd[e].astype(BF16)
                yp, *odd_in_p = _outproj_even_inproj_odd(yp, ma, mb, w_out, ng_next, w_in_next, cos_p, sin_p, BF16, tm_p)
            else:
                yp = _outproj_even(yp, ma, mb, w_out, tm_o)
            k_p.append(jnp.transpose(k32.reshape(bp, H_A, HD_A, tp), (0, 3, 1, 2)))
            v_p.append(jnp.transpose(v32.reshape(bp, H_A, HD_A, tp), (0, 3, 1, 2)))
            hg_p.append(hs)
            q, k, v, k32, v32, ga, qb, fb, ib, gb = _inproj_even(ys, ng, w_in, qg, kg, grp, F32, tm_s)
            ma = _sb_sample(q, k, v, ga, jnp.repeat(bias * LOG2E, ts).reshape(H_A * ts, 1), _ntri(page_size),
                            jnp.transpose(cache_k[e], (0, 2, 3, 1)), jnp.transpose(cache_v[e], (0, 2, 3, 1)),
                            page_table, ts, SB_PAGES_PER_STEP)
            mb, hs = _hgrn(qb, fb, ib, gb, state_hgrn[e].astype(F32), hgrn_lower_bounds, og, bs, ts, e, F32, ts,
                           HGRN_NB)
            if fuse_next:
                ys, *odd_in_s = _outproj_even_inproj_odd(ys, ma, mb, w_out, ng_next, w_in_next, cos_s, sin_s, F32, tm_s)
            else:
                ys = _outproj_even(ys, ma, mb, w_out, tm_s)
            k_s.append(k32.reshape(bs, ts, H_A, HD_A))
            v_s.append(v32.reshape(bs, ts, H_A, HD_A))
            hg_s.append(hs)
        else:
            w_out = w_out_odd[e]
            q, k, v, g = odd_in_p
            m, s = _ret(q, k, v, g, jnp.zeros((bp, H_C, DK_C, DV_C), F32), bp, tp, BF16, 512, RET_NB, RET_HPG)
            yp = _outproj_odd(yp, m, w_out, tm_o)
            rt_p.append(s)
            q, k, v, g = odd_in_s
            m, s = _ret(q, k, v, g, state_ret[e].astype(F32), bs, ts, F32, ts, RET_NB, RET_HPG)
            ys = _outproj_odd(ys, m, w_out, tm_s)
            rt_s.append(s)
    return (yp.reshape(bp, tp, d), ys.reshape(bs, ts, d), jnp.stack(k_p), jnp.stack(v_p), jnp.stack(k_s),
            jnp.stack(v_s), jnp.stack(hg_p), jnp.stack(hg_s), jnp.stack(rt_p), jnp.stack(rt_s))
```

```python
import functools
import math

import numpy as np
import jax
import jax.numpy as jnp
from jax import lax
from jax.experimental import pallas as pl
from jax.experimental.pallas import tpu as pltpu

F32 = jnp.float32
BF16 = jnp.bfloat16

H_A, HD_A = 8, 64
W_A = H_A * HD_A
H_B, DK_B, DV_B = 4, 128, 128
W_B = H_B * DV_B
H_C, DK_C, DV_C = 4, 256, 512
QK_C = H_C * DK_C
W_C = H_C * DV_C
ROPE_BASE = 10000.0
NORM_EPS = 1e-6
LOG2E = 1.4426950408889634

LANES = 128
MXU_DEPTH = 256
VMEM_LIMIT = 56 * 1024 * 1024

HGRN_CHUNK = 64
HGRN_NB = 2
RET_CHUNK = MXU_DEPTH
RET_PAD_CHUNK = 64
RET_NB, RET_HPG = 2, 2
SB_TQ = MXU_DEPTH
SB_RING = 3
SB_UNROLL = 8 * SB_RING
SB_PAGES_PER_STEP = 16


def _dot(a, b):
    return jnp.dot(a, b, preferred_element_type=F32)


def _dot_nt(a, b):
    return lax.dot_general(a, b, (((1,), (1,)), ((), ())), preferred_element_type=F32)


def _dot_tn(a, b):
    return lax.dot_general(a, b, (((0,), (0,)), ((), ())), preferred_element_type=F32)


def _split2(x):
    hi = x.astype(BF16)
    lo = (x - hi.astype(F32)).astype(BF16)
    return hi, lo


def _split3(x):
    hi = x.astype(BF16)
    r = x - hi.astype(F32)
    mid = r.astype(BF16)
    lo = (r - mid.astype(F32)).astype(BF16)
    return hi, mid, lo


def _sigmoid(x):
    return 1.0 / (1.0 + jnp.exp(-x))


def _silu(x):
    return x * _sigmoid(x)


def _rms(x, eps=NORM_EPS):
    return x * lax.rsqrt(jnp.mean(x * x, axis=-1, keepdims=True) + eps)


def _params(sem):
    return pltpu.CompilerParams(dimension_semantics=sem, vmem_limit_bytes=VMEM_LIMIT)


def _inproj_even_kernel(x_ref, ng_ref, w_ref, qg_ref, kg_ref, grp_ref,
                        q_ref, k_ref, v_ref, k32_ref, v32_ref, ga_ref,
                        qb_ref, fb_ref, ib_ref, gb_ref, *, kv_token_minor):
    x = x_ref[...]
    h = (_rms(x) * ng_ref[...]).astype(BF16)

    def proj(c):
        return _dot(h, w_ref[:, c * W_A:(c + 1) * W_A].astype(BF16))

    def head_norm(a, gain):
        hi, lo = _split2(a * a)
        ssum = _dot(hi, grp_ref[...]) + _dot(lo, grp_ref[...])
        return a * lax.rsqrt(ssum * (1.0 / HD_A) + NORM_EPS) * gain

    qa = head_norm(proj(0), qg_ref[...])
    q_ref[...] = (qa * (HD_A ** -0.5 * LOG2E)).astype(q_ref.dtype)
    ka = head_norm(proj(1), kg_ref[...])
    k32_ref[...] = ka.T if kv_token_minor else ka
    k_ref[...] = ka.astype(k_ref.dtype)
    va = proj(2)
    v32_ref[...] = va.T if kv_token_minor else va
    v_ref[...] = va.astype(v_ref.dtype)
    ga_ref[...] = _silu(proj(3))
    qb_ref[...] = proj(4)
    fb_ref[...] = proj(5)
    ib_ref[...] = proj(6)
    gb_ref[...] = _silu(proj(7))


def _inproj_even(x, ng, w, qg, kg, grp, act_dtype, tm, kv_seq=None):
    n, d = x.shape
    row = lambda i: (i, 0)
    const = lambda i: (0, 0)
    blk = pl.BlockSpec((tm, W_A), row)
    outs = [jax.ShapeDtypeStruct((n, W_A), act_dtype)] * 3 + [jax.ShapeDtypeStruct((n, W_A), F32)] * 7
    out_specs = [blk] * 10
    if kv_seq is not None:
        nt = kv_seq // tm
        kv_blk = pl.BlockSpec((None, W_A, tm), lambda i: (i // nt, 0, i % nt))
        kv_shape = jax.ShapeDtypeStruct((n // kv_seq, W_A, kv_seq), F32)
        outs[3:5] = [kv_shape, kv_shape]
        out_specs[3:5] = [kv_blk, kv_blk]
    return pl.pallas_call(
        functools.partial(_inproj_even_kernel, kv_token_minor=kv_seq is not None),
        grid=(n // tm,),
        in_specs=[pl.BlockSpec((tm, d), row), pl.BlockSpec((1, d), const),
                  pl.BlockSpec(w.shape, const, pipeline_mode=pl.Buffered(1)), pl.BlockSpec((1, W_A), const),
                  pl.BlockSpec((1, W_A), const), pl.BlockSpec((W_A, W_A), const)],
        out_specs=out_specs,
        out_shape=outs,
        compiler_params=_params(("parallel",)),
        name="inproj_even",
    )(x, ng, w, qg, kg, grp)


def _sb_nl(z, mask):
    neg_abs = lax.bitcast_convert_type(lax.bitcast_convert_type(z, jnp.uint32) | jnp.uint32(0x80000000), F32)
    nl = jnp.maximum(z, 0.0) + jnp.log(1.0 + jnp.exp2(neg_abs)) * LOG2E
    if mask is not None:
        nl = jnp.where(mask, nl, 0.0)
    return nl


def _sb_later(nl, ntri):
    return _dot(nl.astype(BF16), ntri)


def _sb_prompt_kernel(bias_ref, q_ref, k_ref, v_ref, ga_ref, ntri_ref, o_ref, qs_ref, acc_ref, carry_ref,
                      z_ref, zi_ref, col_ref, *, tq, nq):
    hp = pl.program_id(1)
    rows = 2 * tq
    lane = lax.broadcasted_iota(jnp.int32, (tq, LANES), 1)

    b0 = bias_ref[2 * hp] * LOG2E
    b1 = bias_ref[2 * hp + 1] * LOG2E

    def blk(i):
        return pl.ds(pl.multiple_of(i * tq, tq), tq)

    def logits(qs, kb):
        zr = _dot_nt(qs, k_ref[blk(kb), :])
        return jnp.concatenate([zr[:tq] + b0, zr[tq:] + b1], axis=0)

    r2 = lax.broadcasted_iota(jnp.int32, (rows, tq), 0)
    c2 = lax.broadcasted_iota(jnp.int32, (rows, tq), 1)
    mask = c2 < jnp.where(r2 >= tq, r2 - tq, r2)

    def stage_a(p, slot, diag):
        if diag:
            q = q_ref[blk(p[0]), :].astype(F32)
            qs = jnp.concatenate([jnp.where(lane < HD_A, q, 0.0), jnp.where(lane >= HD_A, q, 0.0)],
                                 axis=0).astype(BF16)
            qs_ref[p[0]] = qs
        else:
            qs = qs_ref[p[0]]
        z_ref[slot] = logits(qs, p[1])

    def stage_b(slot, diag):
        z = z_ref[slot]
        nl = _sb_nl(z, mask if diag else None)
        zi_ref[slot] = z - nl
        later = _sb_later(nl, ntri_ref[...])
        zi_ref[slot] += later
        col_ref[slot] = later[:, 0:1] - nl[:, 0:1]

    def stage_c(p, slot, diag):
        def own_lanes(x):
            return jnp.where(lane < HD_A, x[:tq], x[tq:])

        if diag:
            w = jnp.where(mask, jnp.exp2(zi_ref[slot]), 0.0)
            acc_ref[p[0]] = own_lanes(_dot(w.astype(BF16), v_ref[blk(p[1]), :]))
            carry_ref[p[0]] = col_ref[slot]
        else:
            carry = carry_ref[p[0]]
            w = jnp.exp2(zi_ref[slot] + carry)
            acc_ref[p[0]] += own_lanes(_dot(w.astype(BF16), v_ref[blk(p[1]), :]))
            carry_ref[p[0]] = carry + col_ref[slot]

    def pipeline(n_steps, p0, nxt, diag):
        if n_steps == 0:
            return
        stage_a(p0, 0, diag)
        stage_b(0, diag)
        if n_steps == 1:
            stage_c(p0, 0, diag)
            return
        stage_a(nxt(p0), 1, diag)
        n_steady = n_steps - 2

        def step(t, pc, pa):
            stage_c(pc, t % SB_RING, diag)
            stage_b((t + 1) % SB_RING, diag)
            stage_a(pa, (t + 2) % SB_RING, diag)
            return nxt(pc), nxt(pa)

        def body(_, st):
            pc, pa = st[:2], st[2:]
            for u in range(SB_UNROLL):
                pc, pa = step(u, pc, pa)
            return (*pc, *pa)

        n_trips = n_steady // SB_UNROLL
        st = lax.fori_loop(0, n_trips, body, (*p0, *nxt(nxt(p0))))
        pc, pa = st[:2], st[2:]
        for t in range(n_trips * SB_UNROLL, n_steady):
            pc, pa = step(t, pc, pa)
        stage_c(pc, n_steady % SB_RING, diag)
        stage_b((n_steady + 1) % SB_RING, diag)
        stage_c(nxt(pc), (n_steady + 1) % SB_RING, diag)

    def next_off_diagonal(p):
        wrap = p[1] == 0
        return jnp.where(wrap, p[0] + 1, p[0]), jnp.where(wrap, p[0], p[1] - 1)

    pipeline(nq, (jnp.int32(0), jnp.int32(0)), lambda p: (p[0] + 1, p[1] + 1), True)
    pipeline(nq * (nq - 1) // 2, (jnp.int32(1), jnp.int32(0)), next_off_diagonal, False)

    def finish(qi, _):
        o_ref[blk(qi), :] = (acc_ref[qi] * ga_ref[blk(qi), :]).astype(o_ref.dtype)
        return 0

    lax.fori_loop(0, nq, finish, 0)


def _sb_prompt(q, k, v, ga, bias, ntri, batch, seq, tq):
    n = q.shape[0]
    nq = seq // tq
    seqmap = lambda b, hp: (b, hp)
    blk = pl.BlockSpec((seq, LANES), seqmap)
    return pl.pallas_call(
        functools.partial(_sb_prompt_kernel, tq=tq, nq=nq),
        grid=(batch, H_A // 2),
        in_specs=[pl.BlockSpec(memory_space=pltpu.SMEM), blk, blk, blk, blk,
                  pl.BlockSpec((tq, tq), lambda b, hp: (0, 0))],
        out_specs=blk,
        out_shape=jax.ShapeDtypeStruct((n, W_A), BF16),
        scratch_shapes=[pltpu.VMEM((nq, 2 * tq, LANES), BF16), pltpu.VMEM((nq, tq, LANES), F32),
                        pltpu.VMEM((nq, 2 * tq, 1), F32),
                        pltpu.VMEM((SB_RING, 2 * tq, tq), F32), pltpu.VMEM((SB_RING, 2 * tq, tq), F32),
                        pltpu.VMEM((SB_RING, 2 * tq, 1), F32)],
        compiler_params=_params(("parallel", "parallel")),
        name="sb_prompt",
    )(bias, q, k, v, ga, ntri)


def _sb_sample_kernel(pt_ref, q_ref, kn_ref, vn_ref, ga_ref, bias_ref, ntri_ref, *rest, ts, pages):
    page_refs = rest[:2 * pages]
    o_ref, acc_ref, carry_ref = rest[2 * pages:]
    s = pl.program_id(1)
    rows = H_A * ts
    ps = ntri_ref.shape[0]
    q = q_ref[...]
    qpair = [jnp.concatenate([q[:, (2 * p) * HD_A:(2 * p + 1) * HD_A], q[:, (2 * p + 1) * HD_A:(2 * p + 2) * HD_A]],
                             axis=0).astype(BF16) for p in range(H_A // 2)]
    bias = bias_ref[...]
    ntri = ntri_ref[...]

    def own_rows(x, h):
        return x[(h % 2) * ts:(h % 2 + 1) * ts]

    def sweep(n, qk, pv, carry, mask):
        z = jnp.concatenate(
            [jnp.concatenate([own_rows(qk(j, h, qpair[h // 2]), h) for h in range(H_A)], axis=0)
             for j in range(n)], axis=1) + bias
        nl = _sb_nl(z, mask)
        laters = [_sb_later(nl[:, j * ps:(j + 1) * ps], ntri) for j in range(n)]
        sums = []
        for j in range(n):
            sums.append(carry + laters[j])
            carry = carry + (laters[j][:, 0:1] - nl[:, j * ps:j * ps + 1])
        w = jnp.exp2((z - nl) + jnp.concatenate(sums, axis=1))
        if mask is not None:
            w = jnp.where(mask, w, 0.0)
        w = w.astype(BF16)
        parts = [jnp.concatenate([own_rows(pv(j, h, w[(h // 2) * 2 * ts:(h // 2 + 1) * 2 * ts, j * ps:(j + 1) * ps]), h)
                                  for h in range(H_A)], axis=0) for j in range(n)]
        contrib = parts[0]
        for c in parts[1:]:
            contrib = contrib + c
        return contrib, carry

    @pl.when(s == 0)
    def _():
        pad = jnp.zeros((ps - ts, HD_A), F32)
        kn = kn_ref[...]
        vn = vn_ref[...]
        new_k = lambda h: jnp.concatenate([kn[:, h * HD_A:(h + 1) * HD_A], pad], axis=0).astype(BF16)
        new_v = lambda h: jnp.concatenate([vn[:, h * HD_A:(h + 1) * HD_A], pad], axis=0).astype(BF16)
        r2 = lax.broadcasted_iota(jnp.int32, (rows, ps), 0)
        c2 = lax.broadcasted_iota(jnp.int32, (rows, ps), 1)
        mask = c2 < lax.rem(r2, ts)
        contrib, carry = sweep(1, lambda j, h, q16: _dot_nt(q16, new_k(h)), lambda j, h, w16: _dot(w16, new_v(h)),
                               jnp.zeros((rows, 1), F32), mask)
        acc_ref[...] = contrib
        carry_ref[...] = carry

    contrib, carry = sweep(pages,
                           lambda j, h, q16: _dot(q16, page_refs[2 * j][h].astype(BF16)),
                           lambda j, h, w16: _dot_nt(w16, page_refs[2 * j + 1][h].astype(BF16)),
                           carry_ref[...], None)
    acc_ref[...] += contrib
    carry_ref[...] = carry

    @pl.when(s == pl.num_programs(1) - 1)
    def _():
        acc = acc_ref[...]
        o = jnp.concatenate([acc[h * ts:(h + 1) * ts] for h in range(H_A)], axis=-1)
        o_ref[...] = (o * ga_ref[...]).astype(o_ref.dtype)


def _sb_sample(q, kn, vn, ga, bias_rows, ntri, cache_k, cache_v, page_table, ts, pages):
    n = q.shape[0]
    nb, n_pages = page_table.shape
    ps = cache_k.shape[3]
    steps = n_pages // pages
    tok = lambda b, s, pt: (b, 0)
    const = lambda b, s, pt: (0, 0)

    def page_map(j):
        return lambda b, s, pt: (pt[b, n_pages - 1 - (s * pages + j)], 0, 0, 0)

    page_specs, page_args = [], []
    for j in range(pages):
        page_specs += [pl.BlockSpec((None, H_A, HD_A, ps), page_map(j))] * 2
        page_args += [cache_k, cache_v]
    rows = H_A * ts
    grid_spec = pltpu.PrefetchScalarGridSpec(
        num_scalar_prefetch=1,
        grid=(nb, steps),
        in_specs=[pl.BlockSpec((ts, W_A), tok)] * 4
                 + [pl.BlockSpec((rows, 1), const), pl.BlockSpec((ps, ps), const)] + page_specs,
        out_specs=pl.BlockSpec((ts, W_A), tok),
        scratch_shapes=[pltpu.VMEM((rows, HD_A), F32), pltpu.VMEM((rows, 1), F32)],
    )
    return pl.pallas_call(
        functools.partial(_sb_sample_kernel, ts=ts, pages=pages),
        grid_spec=grid_spec,
        out_shape=jax.ShapeDtypeStruct((n, W_A), F32),
        compiler_params=_params(("parallel", "arbitrary")),
        name="sb_sample",
    )(page_table, q, kn, vn, ga, bias_rows, ntri, *page_args)


def _hgrn_tables(c, n_valid):
    levels = math.ceil(math.log2(n_valid))
    lm = np.zeros((levels, c, c), np.float32)
    r = np.arange(c)
    for l in range(levels):
        m = 2 ** l
        second = (r // m) % 2 == 1
        same = (r[:, None] // (2 * m)) == (r[None, :] // (2 * m))
        lm[l] = (same & second[:, None] & (~second)[None, :]).astype(np.float32)
    a = np.tril(np.ones((c, c), np.float32))
    a[:, n_valid:] = 0.0
    return a, lm


def _hold_mid(b, m, rowi):
    c = b.shape[0]
    if m == 1:
        return jnp.where(jnp.bitwise_and(rowi, 1) == 1, pltpu.roll(b, 1, 0), b)
    if m == 2:
        lo = jnp.concatenate([jnp.broadcast_to(b[g + 1:g + 2], (8, LANES)) for g in range(0, c, 8)], axis=0)
        hi = jnp.concatenate([jnp.broadcast_to(b[g + 5:g + 6], (8, LANES)) for g in range(0, c, 8)], axis=0)
        return jnp.where(jnp.bitwise_and(rowi, 4) == 0, lo, hi)
    return jnp.concatenate([jnp.broadcast_to(b[g + m - 1:g + m], (2 * m, LANES)) for g in range(0, c, 2 * m)], axis=0)


def _hgrn_kernel(q_ref, f_ref, i_ref, g_ref, s0_ref, lbp_ref, og_ref, a_ref, lm_ref,
                 o_ref, sout_ref, st_ref, *, c, n_chunks, n_valid, layer, nb):
    t = pl.program_id(1)
    levels = lm_ref.shape[0]
    chains = [(bi, h) for bi in range(nb) for h in range(H_B)]

    @pl.when(t == 0)
    def _():
        for bi, h in chains:
            st_ref[bi, h] = s0_ref[bi, h].T

    lbp = lbp_ref[...]
    p = jnp.exp(lbp - jnp.max(lbp, axis=0, keepdims=True))
    lb = jnp.sum(p[:layer + 1], axis=0, keepdims=True) / jnp.sum(p, axis=0, keepdims=True)
    og = og_ref[...]
    rowi = lax.broadcasted_iota(jnp.int32, (c, 1), 0)
    valid = rowi < n_valid

    def load(ref, bi, ci):
        if n_valid < c:
            x = ref[bi]
            return jnp.concatenate([x, jnp.zeros((c - x.shape[0], x.shape[1]), F32)], axis=0)
        return ref[bi, pl.ds(pl.multiple_of(ci * c, c), c), :]

    def chunk(ci, _):
        gs, kks, qqs, vvs = [], [], [], []
        for bi in range(nb):
            xf = load(f_ref, bi, ci)
            e = jnp.exp(-jnp.abs(xf))
            r = 1.0 / (1.0 + e)
            sig_pos = jnp.where(xf >= 0, r, e * r)
            sig_neg = jnp.where(xf >= 0, e * r, r)
            g = jnp.log(lb + (1.0 - lb) * sig_pos)
            kk = (1.0 - lb) * sig_neg
            if n_valid < c:
                g = jnp.where(valid, g, 0.0)
                kk = jnp.where(valid, kk, 0.0)
            gs.append(g)
            kks.append(kk)
            qqs.append(_silu(load(q_ref, bi, ci)))
            vvs.append(load(i_ref, bi, ci))
        ghi, gmid, glo = _split3(jnp.concatenate(gs, axis=1))
        a = a_ref[...]
        bcum_all = _dot(a, ghi) + _dot(a, gmid) + _dot(a, glo)
        outs = [[] for _ in range(nb)]
        for bi, h in chains:
            hl = slice(h * LANES, (h + 1) * LANES)
            el = slice((bi * H_B + h) * LANES, (bi * H_B + h + 1) * LANES)
            qq, kk, vv = qqs[bi][:, hl], kks[bi][:, hl], vvs[bi][:, hl]
            vv16 = vv.astype(BF16)
            bcum = bcum_all[:, el]
            rem = bcum[c - 1:c, :] - bcum
            st = st_ref[bi, h]
            o = _dot_nt((qq * jnp.exp(bcum)).astype(BF16), st.astype(BF16))
            scores = jnp.zeros((c, c), F32)
            for l in range(levels):
                second = jnp.bitwise_and(jnp.right_shift(rowi, l), 1) == 1
                d = bcum - _hold_mid(bcum, 2 ** l, rowi)
                x = jnp.exp(jnp.where(second, d, -d))
                scores = scores + _dot_nt((qq * x).astype(BF16), (kk * x).astype(BF16)) * lm_ref[l]
            o = o + _dot(scores.astype(BF16), vv16)
            o = o + jnp.sum(qq * kk, axis=-1, keepdims=True) * vv
            dec_last = jnp.exp(bcum[c - 1:c, :])
            st_ref[bi, h] = dec_last * st + _dot_tn(vv16, (kk * jnp.exp(rem)).astype(BF16))
            outs[bi].append(_rms(o) * og)
        for bi in range(nb):
            ob = jnp.concatenate(outs[bi], axis=1) * load(g_ref, bi, ci)
            if n_valid < c:
                o_ref[bi] = ob[:n_valid].astype(o_ref.dtype)
            else:
                o_ref[bi, pl.ds(pl.multiple_of(ci * c, c), c), :] = ob.astype(o_ref.dtype)
        return 0

    per_trip = 2 if n_chunks % 2 == 0 else 1

    def trip(i, _):
        for u in range(per_trip):
            chunk(i * per_trip + u, 0)
        return 0

    lax.fori_loop(0, n_chunks // per_trip, trip, 0)

    @pl.when(t == pl.num_programs(1) - 1)
    def _():
        for bi, h in chains:
            sout_ref[bi, h] = st_ref[bi, h].T


def _hgrn(qb, fb, ib, gb, s0, lbp, og, batch, seq, layer, out_dtype, tb, nb):
    n = qb.shape[0]
    c = HGRN_CHUNK
    if seq >= c:
        n_valid, rows_blk, nt, n_chunks = c, tb, seq // tb, tb // c
    else:
        n_valid, rows_blk, nt, n_chunks = seq, seq, 1, 1
    a_np, lm_np = _hgrn_tables(c, n_valid)
    a = jnp.asarray(a_np, BF16)
    lm = jnp.asarray(lm_np, F32)
    tok = lambda b, t: (b, t, 0)
    st = lambda b, t: (b, 0, 0, 0)
    blk = pl.BlockSpec((nb, rows_blk, W_B), tok)
    st_blk = pl.BlockSpec((nb, H_B, DK_B, DV_B), st)
    r3 = lambda x: x.reshape(batch, seq, W_B)
    mix, s_out = pl.pallas_call(
        functools.partial(_hgrn_kernel, c=c, n_chunks=n_chunks, n_valid=n_valid, layer=layer, nb=nb),
        grid=(batch // nb, nt),
        in_specs=[blk, blk, blk, blk, st_blk,
                  pl.BlockSpec(lbp.shape, lambda b, t: (0, 0)),
                  pl.BlockSpec((1, LANES), lambda b, t: (0, 0)),
                  pl.BlockSpec(a.shape, lambda b, t: (0, 0)),
                  pl.BlockSpec(lm.shape, lambda b, t: (0, 0, 0))],
        out_specs=[blk, st_blk],
        out_shape=[jax.ShapeDtypeStruct((batch, seq, W_B), out_dtype),
                   jax.ShapeDtypeStruct((batch, H_B, DK_B, DV_B), F32)],
        scratch_shapes=[pltpu.VMEM((nb, H_B, DV_B, DK_B), F32)],
        compiler_params=_params(("parallel", "arbitrary")),
        name="hgrn",
    )(r3(qb), r3(fb), r3(ib), r3(gb), s0, lbp, og, a, lm)
    return mix.reshape(n, W_B), s_out


def _outproj_even_kernel(x_ref, ma_ref, mb_ref, w_ref, y_ref):
    y = x_ref[...] + _dot(ma_ref[...].astype(BF16), w_ref[:W_A, :]) + _dot(mb_ref[...].astype(BF16), w_ref[W_A:, :])
    y_ref[...] = y


def _outproj_even(x, ma, mb, w, tm):
    n, d = x.shape
    row = lambda i: (i, 0)
    return pl.pallas_call(
        _outproj_even_kernel,
        grid=(n // tm,),
        in_specs=[pl.BlockSpec((tm, d), row), pl.BlockSpec((tm, W_A), row), pl.BlockSpec((tm, W_B), row),
                  pl.BlockSpec(w.shape, lambda i: (0, 0))],
        out_specs=pl.BlockSpec((tm, d), row),
        out_shape=jax.ShapeDtypeStruct((n, d), F32),
        compiler_params=_params(("parallel",)),
        name="outproj_even",
    )(x, ma, mb, w)


def _outproj_even_inproj_odd_kernel(x_ref, ma_ref, mb_ref, wo_ref, ng_ref, w_ref, cos_ref, sin_ref,
                                    y_ref, q_ref, k_ref, v_ref, g_ref):
    y = (x_ref[...] + _dot(ma_ref[...].astype(BF16), wo_ref[:W_A, :])
         + _dot(mb_ref[...].astype(BF16), wo_ref[W_A:, :]))
    y_ref[...] = y
    _inproj_odd_body(y, ng_ref, w_ref, cos_ref, sin_ref, q_ref, k_ref, v_ref, g_ref)


def _inproj_odd_body(x, ng_ref, w_ref, cos_ref, sin_ref, q_ref, k_ref, v_ref, g_ref):
    h = (_rms(x) * ng_ref[...]).astype(BF16)
    cos = cos_ref[...]
    sin = sin_ref[...]
    half = DK_C // 2

    def rot(a, scale):
        outs = []
        for hd in range(H_C):
            x1 = a[:, hd * DK_C:hd * DK_C + half]
            x2 = a[:, hd * DK_C + half:(hd + 1) * DK_C]
            outs += [(x1 * cos - x2 * sin) * scale, (x2 * cos + x1 * sin) * scale]
        return jnp.concatenate(outs, axis=-1)

    q_ref[...] = rot(_dot(h, w_ref[:, :QK_C]), 1.0).astype(q_ref.dtype)
    k_ref[...] = rot(_dot(h, w_ref[:, QK_C:2 * QK_C]), DK_C ** -0.5)
    for j in range(2):
        lo = 2 * QK_C + j * QK_C
        v_ref[:, j * QK_C:(j + 1) * QK_C] = _dot(h, w_ref[:, lo:lo + QK_C]).astype(v_ref.dtype)
    for j in range(2):
        lo = 2 * QK_C + W_C + j * QK_C
        g_ref[:, j * QK_C:(j + 1) * QK_C] = _silu(_dot(h, w_ref[:, lo:lo + QK_C]))


def _outproj_even_inproj_odd(x, ma, mb, wo, ng, w, cos, sin, act_dtype, tm):
    n, d = x.shape
    npos = cos.shape[0] // tm
    row = lambda i: (i, 0)
    const = lambda i: (0, 0)
    pos = lambda i: (i % npos, 0)
    return pl.pallas_call(
        _outproj_even_inproj_odd_kernel,
        grid=(n // tm,),
        in_specs=[pl.BlockSpec((tm, d), row), pl.BlockSpec((tm, W_A), row), pl.BlockSpec((tm, W_B), row),
                  pl.BlockSpec(wo.shape, const, pipeline_mode=pl.Buffered(1)), pl.BlockSpec((1, d), const),
                  pl.BlockSpec(w.shape, const, pipeline_mode=pl.Buffered(1)),
                  pl.BlockSpec((tm, DK_C // 2), pos), pl.BlockSpec((tm, DK_C // 2), pos)],
        out_specs=[pl.BlockSpec((tm, d), row), pl.BlockSpec((tm, QK_C), row), pl.BlockSpec((tm, QK_C), row),
                   pl.BlockSpec((tm, W_C), row), pl.BlockSpec((tm, W_C), row)],
        out_shape=[jax.ShapeDtypeStruct((n, d), F32),
                   jax.ShapeDtypeStruct((n, QK_C), act_dtype), jax.ShapeDtypeStruct((n, QK_C), F32),
                   jax.ShapeDtypeStruct((n, W_C), act_dtype), jax.ShapeDtypeStruct((n, W_C), F32)],
        compiler_params=_params(("parallel",)),
        name="outproj_even_inproj_odd",
    )(x, ma, mb, wo, ng, w, cos, sin)


def _ret_kernel(q_ref, k_ref, v_ref, g_ref, s0_ref, idec_ref, qdec_ref, kdec_ref, cdec_ref,
                o_ref, sout_ref, s_ref, *, c, n_chunks, n_valid, nb, hpg):
    t = pl.program_id(2)
    chains = [(bi, hh) for bi in range(nb) for hh in range(hpg)]

    @pl.when(t == 0)
    def _():
        s_ref[...] = s0_ref[...]

    def load(ref, bi, hh, width, ci):
        cols = slice(hh * width, (hh + 1) * width)
        if n_valid < c:
            x = ref[bi, :, cols].astype(F32)
            return jnp.concatenate([x, jnp.zeros((c - x.shape[0], x.shape[1]), F32)], axis=0)
        return ref[bi, pl.ds(pl.multiple_of(ci * c, c), c), cols]

    def chunk(ci, _):
        for bi, hh in chains:
            qc = load(q_ref, bi, hh, DK_C, ci).astype(BF16)
            kc = load(k_ref, bi, hh, DK_C, ci)
            vc = load(v_ref, bi, hh, DV_C, ci).astype(BF16)
            s = s_ref[bi, hh]
            scores = _dot_nt(qc, kc.astype(BF16)) * idec_ref[hh]
            o = _dot(scores.astype(BF16), vc) + _dot(qc, s.astype(BF16)) * qdec_ref[hh, :, 0:1]
            s_ref[bi, hh] = cdec_ref[hh, 0:1, 0:1] * s + _dot_tn((kc * kdec_ref[hh, :, 0:1]).astype(BF16), vc)
            ob = _rms(o) * load(g_ref, bi, hh, DV_C, ci)
            cols = slice(hh * DV_C, (hh + 1) * DV_C)
            if n_valid < c:
                o_ref[bi, :, cols] = ob[:n_valid].astype(o_ref.dtype)
            else:
                o_ref[bi, pl.ds(pl.multiple_of(ci * c, c), c), cols] = ob.astype(o_ref.dtype)
        return 0

    lax.fori_loop(0, n_chunks, chunk, 0)

    @pl.when(t == pl.num_programs(2) - 1)
    def _():
        sout_ref[...] = s_ref[...]


def _ret_tables(c, chunk):
    f32 = jnp.float32
    log_gamma = jnp.log1p(-jnp.exp2(-5.0 - jnp.arange(H_C, dtype=f32)))
    idx = jnp.arange(c, dtype=f32)
    real = idx < chunk
    rel = idx[:, None] - idx[None, :]
    ok = (rel >= 0) & real[:, None] & real[None, :]
    idec = jnp.exp(jnp.where(ok[None], rel[None] * log_gamma[:, None, None], -jnp.inf))
    qdec = jnp.where(real[None, :], jnp.exp((idx[None, :] + 1.0) * log_gamma[:, None]), 0.0)
    kdec = jnp.where(real[None, :], jnp.exp((chunk - 1.0 - idx[None, :]) * log_gamma[:, None]), 0.0)
    cdec = jnp.exp(chunk * log_gamma)
    bc = lambda x: jnp.broadcast_to(x[:, :, None], (H_C, c, LANES))
    return idec, bc(qdec), bc(kdec), jnp.broadcast_to(cdec[:, None, None], (H_C, 8, LANES))


def _ret(q, k, v, g, s0, batch, seq, out_dtype, tb, nb, hpg):
    n = q.shape[0]
    if seq >= RET_CHUNK:
        c = RET_CHUNK
        n_valid, rows_blk, nt, n_chunks = c, tb, seq // tb, tb // c
    else:
        c = RET_PAD_CHUNK
        n_valid, rows_blk, nt, n_chunks = seq, seq, 1, 1
    idec, qdec, kdec, cdec = _ret_tables(c, n_valid)
    tok = lambda hg, b, t: (b, t, hg)
    st = lambda hg, b, t: (b, hg, 0, 0)
    hd = lambda hg, b, t: (hg, 0, 0)
    qk_blk = pl.BlockSpec((nb, rows_blk, hpg * DK_C), tok)
    vg_blk = pl.BlockSpec((nb, rows_blk, hpg * DV_C), tok)
    st_blk = pl.BlockSpec((nb, hpg, DK_C, DV_C), st)
    r3 = lambda x: x.reshape(batch, seq, x.shape[-1])
    mix, s_out = pl.pallas_call(
        functools.partial(_ret_kernel, c=c, n_chunks=n_chunks, n_valid=n_valid, nb=nb, hpg=hpg),
        grid=(H_C // hpg, batch // nb, nt),
        in_specs=[qk_blk, qk_blk, vg_blk, vg_blk, st_blk,
                  pl.BlockSpec((hpg, c, c), hd), pl.BlockSpec((hpg, c, LANES), hd),
                  pl.BlockSpec((hpg, c, LANES), hd), pl.BlockSpec((hpg, 8, LANES), hd)],
        out_specs=[vg_blk, st_blk],
        out_shape=[jax.ShapeDtypeStruct((batch, seq, W_C), out_dtype),
                   jax.ShapeDtypeStruct((batch, H_C, DK_C, DV_C), F32)],
        scratch_shapes=[pltpu.VMEM((nb, hpg, DK_C, DV_C), F32)],
        compiler_params=_params(("parallel", "parallel", "arbitrary")),
        name="retention",
    )(r3(q), r3(k), r3(v), r3(g), s0, idec, qdec, kdec, cdec)
    return mix.reshape(n, W_C), s_out


def _outproj_odd_kernel(x_ref, m_ref, w_ref, y_ref):
    y_ref[...] = x_ref[...] + _dot(m_ref[...].astype(BF16), w_ref[...].astype(BF16))


def _outproj_odd(x, m, w, tm):
    n, d = x.shape
    row = lambda i: (i, 0)
    return pl.pallas_call(
        _outproj_odd_kernel,
        grid=(n // tm,),
        in_specs=[pl.BlockSpec((tm, d), row), pl.BlockSpec((tm, W_C), row),
                  pl.BlockSpec(w.shape, lambda i: (0, 0), pipeline_mode=pl.Buffered(1))],
        out_specs=pl.BlockSpec((tm, d), row),
        out_shape=jax.ShapeDtypeStruct((n, d), F32),
        compiler_params=_params(("parallel",)),
        name="outproj_odd",
    )(x, m, w)


def _rope_tables(pos):
    half = DK_C // 2
    inv = 1.0 / (ROPE_BASE ** jnp.linspace(0.0, 1.0, half, dtype=F32))
    ang = pos[:, None] * inv[None, :]
    return jnp.cos(ang), jnp.sin(ang)


def _ntri(n):
    return jnp.asarray(-np.tril(np.ones((n, n), np.float32), -1), BF16)


def kernel(x_prompt, x_sample, cache_k, cache_v, page_table, state_hgrn, state_ret, norm_g,
           w_in_even, w_out_even, q_norm_g, k_norm_g, sb_logit_bias, hgrn_lower_bounds,
           hgrn_out_norm_g, w_in_odd, w_out_odd):
    bp, tp, d = x_prompt.shape
    bs, ts, _ = x_sample.shape
    n_pool, page_size = cache_k.shape[1], cache_k.shape[2]
    past_len = page_table.shape[1] * page_size
    depth = norm_g.shape[0]
    tm_p = 512
    tm_o = 512
    tm_s = bs * ts

    yp = x_prompt.reshape(bp * tp, d)
    ys = x_sample.reshape(bs * ts, d)
    grp = jnp.asarray(np.kron(np.eye(H_A, dtype=np.float32), np.ones((HD_A, HD_A), np.float32)), BF16)
    cos_p, sin_p = _rope_tables(jnp.arange(tp, dtype=F32))
    cos_s, sin_s = _rope_tables(past_len + jnp.arange(ts, dtype=F32))
    cos_s, sin_s = jnp.tile(cos_s, (bs, 1)), jnp.tile(sin_s, (bs, 1))

    k_p, v_p, k_s, v_s, hg_p, hg_s, rt_p, rt_s = [], [], [], [], [], [], [], []
    for layer in range(depth):
        e = layer // 2
        ng = norm_g[layer].reshape(1, d)
        if layer % 2 == 0:
            w_in = w_in_even[e]
            w_out = w_out_even[e].astype(BF16)
            qg = jnp.tile(q_norm_g[e], H_A).reshape(1, W_A)
            kg = jnp.tile(k_norm_g[e], H_A).reshape(1, W_A)
            og = hgrn_out_norm_g[e].reshape(1, DV_B)
            bias = sb_logit_bias[e].astype(F32)
            q, k, v, k32, v32, ga, qb, fb, ib, gb = _inproj_even(yp, ng, w_in, qg, kg, grp, BF16, tm_p, kv_seq=tp)
            ma = _sb_prompt(q, k, v, ga, bias, _ntri(SB_TQ), bp, tp, SB_TQ)
            mb, hs = _hgrn(qb, fb, ib, gb, jnp.zeros((bp, H_B, DK_B, DV_B), F32), hgrn_lower_bounds,
                           og, bp, tp, e, BF16, 512, HGRN_NB)
            fuse_next = layer + 1 < depth
            if fuse_next:
                ng_next = norm_g[layer + 1].reshape(1, d)
                w_in_next = w_in_odd[e].astype(BF16)
                yp, *odd_in_p = _outproj_even_inproj_odd(yp, ma, mb, w_out, ng_next, w_in_next, cos_p, sin_p, BF16, tm_p)
            else:
                yp = _outproj_even(yp, ma, mb, w_out, tm_o)
            k_p.append(jnp.transpose(k32.reshape(bp, H_A, HD_A, tp), (0, 3, 1, 2)))
            v_p.append(jnp.transpose(v32.reshape(bp, H_A, HD_A, tp), (0, 3, 1, 2)))
            hg_p.append(hs)
            q, k, v, k32, v32, ga, qb, fb, ib, gb = _inproj_even(ys, ng, w_in, qg, kg, grp, F32, tm_s)
            ma = _sb_sample(q, k, v, ga, jnp.repeat(bias * LOG2E, ts).reshape(H_A * ts, 1), _ntri(page_size),
                            jnp.transpose(cache_k[e], (0, 2, 3, 1)), jnp.transpose(cache_v[e], (0, 2, 3, 1)),
                            page_table, ts, SB_PAGES_PER_STEP)
            mb, hs = _hgrn(qb, fb, ib, gb, state_hgrn[e].astype(F32), hgrn_lower_bounds, og, bs, ts, e, F32, ts,
                           HGRN_NB)
            if fuse_next:
                ys, *odd_in_s = _outproj_even_inproj_odd(ys, ma, mb, w_out, ng_next, w_in_next, cos_s, sin_s, F32, tm_s)
            else:
                ys = _outproj_even(ys, ma, mb, w_out, tm_s)
            k_s.append(k32.reshape(bs, ts, H_A, HD_A))
            v_s.append(v32.reshape(bs, ts, H_A, HD_A))
            hg_s.append(hs)
        else:
            w_out = w_out_odd[e]
            q, k, v, g = odd_in_p
            m, s = _ret(q, k, v, g, jnp.zeros((bp, H_C, DK_C, DV_C), F32), bp, tp, BF16, 512, RET_NB, RET_HPG)
            yp = _outproj_odd(yp, m, w_out, tm_o)
            rt_p.append(s)
            q, k, v, g = odd_in_s
            m, s = _ret(q, k, v, g, state_ret[e].astype(F32), bs, ts, F32, ts, RET_NB, RET_HPG)
            ys = _outproj_odd(ys, m, w_out, tm_s)
            rt_s.append(s)
    return (yp.reshape(bp, tp, d), ys.reshape(bs, ts, d), jnp.stack(k_p), jnp.stack(v_p), jnp.stack(k_s),
            jnp.stack(v_s), jnp.stack(hg_p), jnp.stack(hg_s), jnp.stack(rt_p), jnp.stack(rt_s))
```

```python
import functools
import math

import numpy as np
import jax
import jax.numpy as jnp
from jax import lax
from jax.experimental import pallas as pl
from jax.experimental.pallas import tpu as pltpu

F32 = jnp.float32
BF16 = jnp.bfloat16

H_A, HD_A = 8, 64
W_A = H_A * HD_A
H_B, DK_B, DV_B = 4, 128, 128
W_B = H_B * DV_B
H_C, DK_C, DV_C = 4, 256, 512
QK_C = H_C * DK_C
W_C = H_C * DV_C
ROPE_BASE = 10000.0
NORM_EPS = 1e-6
LOG2E = 1.4426950408889634

LANES = 128
MXU_DEPTH = 256
VMEM_LIMIT = 56 * 1024 * 1024

HGRN_CHUNK = 64
HGRN_NB = 2
RET_CHUNK = MXU_DEPTH
RET_PAD_CHUNK = 64
RET_NB, RET_HPG = 2, 2
SB_TQ = MXU_DEPTH
SB_BIAS_TERMS = 3
SB_RING = 3
SB_UNROLL = 8 * SB_RING
SB_PAGES_PER_STEP = 16


def _dot(a, b):
    return jnp.dot(a, b, preferred_element_type=F32)


def _dot_nt(a, b):
    return lax.dot_general(a, b, (((1,), (1,)), ((), ())), preferred_element_type=F32)


def _dot_tn(a, b):
    return lax.dot_general(a, b, (((0,), (0,)), ((), ())), preferred_element_type=F32)


def _split2(x):
    hi = x.astype(BF16)
    lo = (x - hi.astype(F32)).astype(BF16)
    return hi, lo


def _split3(x):
    hi = x.astype(BF16)
    r = x - hi.astype(F32)
    mid = r.astype(BF16)
    lo = (r - mid.astype(F32)).astype(BF16)
    return hi, mid, lo


def _sigmoid(x):
    return 1.0 / (1.0 + jnp.exp(-x))


def _silu(x):
    return x * _sigmoid(x)


def _rms(x, eps=NORM_EPS):
    return x * lax.rsqrt(jnp.mean(x * x, axis=-1, keepdims=True) + eps)


def _params(sem):
    return pltpu.CompilerParams(dimension_semantics=sem, vmem_limit_bytes=VMEM_LIMIT)


def _inproj_even_kernel(x_ref, ng_ref, w_ref, qg_ref, kg_ref, grp_ref,
                        q_ref, k_ref, v_ref, k32_ref, v32_ref, ga_ref,
                        qb_ref, fb_ref, ib_ref, gb_ref, *, kv_token_minor, k_bias_lanes):
    x = x_ref[...]
    h = (_rms(x) * ng_ref[...]).astype(BF16)

    def proj(c):
        return _dot(h, w_ref[:, c * W_A:(c + 1) * W_A].astype(BF16))

    def head_norm(a, gain):
        hi, lo = _split2(a * a)
        ssum = _dot(hi, grp_ref[...]) + _dot(lo, grp_ref[...])
        return a * lax.rsqrt(ssum * (1.0 / HD_A) + NORM_EPS) * gain

    qa = head_norm(proj(0), qg_ref[...])
    q_ref[...] = (qa * (HD_A ** -0.5 * LOG2E)).astype(q_ref.dtype)
    ka = head_norm(proj(1), kg_ref[...])
    k32_ref[...] = ka.T if kv_token_minor else ka
    if k_bias_lanes:
        lane = lax.broadcasted_iota(jnp.int32, (ka.shape[0], LANES), 1)
        ones = jnp.where(lane < SB_BIAS_TERMS, 1.0, 0.0).astype(k_ref.dtype)
        pieces = []
        for p in range(H_A // 2):
            pieces += [ka[:, p * LANES:(p + 1) * LANES].astype(k_ref.dtype), ones]
        k_ref[...] = jnp.concatenate(pieces, axis=-1)
    else:
        k_ref[...] = ka.astype(k_ref.dtype)
    va = proj(2)
    v32_ref[...] = va.T if kv_token_minor else va
    v_ref[...] = va.astype(v_ref.dtype)
    ga_ref[...] = _silu(proj(3))
    qb_ref[...] = proj(4)
    fb_ref[...] = proj(5)
    ib_ref[...] = proj(6)
    gb_ref[...] = _silu(proj(7))


def _inproj_even(x, ng, w, qg, kg, grp, act_dtype, tm, kv_seq=None):
    n, d = x.shape
    row = lambda i: (i, 0)
    const = lambda i: (0, 0)
    blk = pl.BlockSpec((tm, W_A), row)
    outs = [jax.ShapeDtypeStruct((n, W_A), act_dtype)] * 3 + [jax.ShapeDtypeStruct((n, W_A), F32)] * 7
    out_specs = [blk] * 10
    if kv_seq is not None:
        nt = kv_seq // tm
        kv_blk = pl.BlockSpec((None, W_A, tm), lambda i: (i // nt, 0, i % nt))
        kv_shape = jax.ShapeDtypeStruct((n // kv_seq, W_A, kv_seq), F32)
        outs[3:5] = [kv_shape, kv_shape]
        out_specs[3:5] = [kv_blk, kv_blk]
        outs[1] = jax.ShapeDtypeStruct((n, 2 * W_A), act_dtype)
        out_specs[1] = pl.BlockSpec((tm, 2 * W_A), row)
    return pl.pallas_call(
        functools.partial(_inproj_even_kernel, kv_token_minor=kv_seq is not None, k_bias_lanes=kv_seq is not None),
        grid=(n // tm,),
        in_specs=[pl.BlockSpec((tm, d), row), pl.BlockSpec((1, d), const),
                  pl.BlockSpec(w.shape, const, pipeline_mode=pl.Buffered(1)), pl.BlockSpec((1, W_A), const),
                  pl.BlockSpec((1, W_A), const), pl.BlockSpec((W_A, W_A), const)],
        out_specs=out_specs,
        out_shape=outs,
        compiler_params=_params(("parallel",)),
        name="inproj_even",
    )(x, ng, w, qg, kg, grp)


def _sb_nl(z, mask):
    neg_abs = lax.bitcast_convert_type(lax.bitcast_convert_type(z, jnp.uint32) | jnp.uint32(0x80000000), F32)
    nl = jnp.maximum(z, 0.0) + jnp.log(1.0 + jnp.exp2(neg_abs)) * LOG2E
    if mask is not None:
        nl = jnp.where(mask, nl, 0.0)
    return nl


def _sb_later(nl, ntri):
    return _dot(nl.astype(BF16), ntri)


def _sb_prompt_kernel(bias_ref, q_ref, k_ref, v_ref, ga_ref, ntri_ref, o_ref, qs_ref, acc_ref, carry_ref,
                      z_ref, zi_ref, col_ref, *, tq, nq):
    hp = pl.program_id(1)
    rows = 2 * tq
    lane = lax.broadcasted_iota(jnp.int32, (tq, LANES), 1)

    def bias_lanes(b):
        rest = jnp.full((tq, LANES), b, F32)
        out = jnp.zeros((tq, LANES), F32)
        for i in range(SB_BIAS_TERMS):
            term = rest.astype(BF16).astype(F32)
            out = jnp.where(lane == i, term, out)
            rest = rest - term
        return out

    bias0 = bias_lanes(bias_ref[2 * hp] * LOG2E)
    bias1 = bias_lanes(bias_ref[2 * hp + 1] * LOG2E)

    def blk(i):
        return pl.ds(pl.multiple_of(i * tq, tq), tq)

    def logits(qs, kb):
        return _dot_nt(qs, k_ref[blk(kb), :])

    r2 = lax.broadcasted_iota(jnp.int32, (rows, tq), 0)
    c2 = lax.broadcasted_iota(jnp.int32, (rows, tq), 1)
    mask = c2 < jnp.where(r2 >= tq, r2 - tq, r2)

    def stage_a(p, slot, diag):
        if diag:
            q = q_ref[blk(p[0]), :].astype(F32)
            qs = jnp.concatenate([jnp.concatenate([jnp.where(lane < HD_A, q, 0.0), bias0], axis=1),
                                  jnp.concatenate([jnp.where(lane >= HD_A, q, 0.0), bias1], axis=1)],
                                 axis=0).astype(BF16)
            qs_ref[p[0]] = qs
        else:
            qs = qs_ref[p[0]]
        z_ref[slot] = logits(qs, p[1])

    def stage_b(slot, diag):
        z = z_ref[slot]
        nl = _sb_nl(z, mask if diag else None)
        zi_ref[slot] = z - nl
        later = _sb_later(nl, ntri_ref[...])
        zi_ref[slot] += later
        col_ref[slot] = later[:, 0:1] - nl[:, 0:1]

    def stage_c(p, slot, diag):
        def own_lanes(x):
            return jnp.where(lane < HD_A, x[:tq], x[tq:])

        if diag:
            w = jnp.where(mask, jnp.exp2(zi_ref[slot]), 0.0)
            acc_ref[p[0]] = own_lanes(_dot(w.astype(BF16), v_ref[blk(p[1]), :]))
            carry_ref[p[0]] = col_ref[slot]
        else:
            carry = carry_ref[p[0]]
            w = jnp.exp2(zi_ref[slot] + carry)
            acc_ref[p[0]] += own_lanes(_dot(w.astype(BF16), v_ref[blk(p[1]), :]))
            carry_ref[p[0]] = carry + col_ref[slot]

    def pipeline(n_steps, p0, nxt, diag):
        if n_steps == 0:
            return
        stage_a(p0, 0, diag)
        stage_b(0, diag)
        if n_steps == 1:
            stage_c(p0, 0, diag)
            return
        stage_a(nxt(p0), 1, diag)
        n_steady = n_steps - 2

        def step(t, pc, pa):
            stage_c(pc, t % SB_RING, diag)
            stage_b((t + 1) % SB_RING, diag)
            stage_a(pa, (t + 2) % SB_RING, diag)
            return nxt(pc), nxt(pa)

        def body(_, st):
            pc, pa = st[:2], st[2:]
            for u in range(SB_UNROLL):
                pc, pa = step(u, pc, pa)
            return (*pc, *pa)

        n_trips = n_steady // SB_UNROLL
        st = lax.fori_loop(0, n_trips, body, (*p0, *nxt(nxt(p0))))
        pc, pa = st[:2], st[2:]
        for t in range(n_trips * SB_UNROLL, n_steady):
            pc, pa = step(t, pc, pa)
        stage_c(pc, n_steady % SB_RING, diag)
        stage_b((n_steady + 1) % SB_RING, diag)
        stage_c(nxt(pc), (n_steady + 1) % SB_RING, diag)

    def next_off_diagonal(p):
        wrap = p[1] == 0
        return jnp.where(wrap, p[0] + 1, p[0]), jnp.where(wrap, p[0], p[1] - 1)

    pipeline(nq, (jnp.int32(0), jnp.int32(0)), lambda p: (p[0] + 1, p[1] + 1), True)
    pipeline(nq * (nq - 1) // 2, (jnp.int32(1), jnp.int32(0)), next_off_diagonal, False)

    def finish(qi, _):
        o_ref[blk(qi), :] = (acc_ref[qi] * ga_ref[blk(qi), :]).astype(o_ref.dtype)
        return 0

    lax.fori_loop(0, nq, finish, 0)


def _sb_prompt(q, k, v, ga, bias, ntri, batch, seq, tq):
    n = q.shape[0]
    nq = seq // tq
    seqmap = lambda b, hp: (b, hp)
    blk = pl.BlockSpec((seq, LANES), seqmap)
    return pl.pallas_call(
        functools.partial(_sb_prompt_kernel, tq=tq, nq=nq),
        grid=(batch, H_A // 2),
        in_specs=[pl.BlockSpec(memory_space=pltpu.SMEM), blk, pl.BlockSpec((seq, 2 * LANES), seqmap), blk, blk,
                  pl.BlockSpec((tq, tq), lambda b, hp: (0, 0))],
        out_specs=blk,
        out_shape=jax.ShapeDtypeStruct((n, W_A), BF16),
        scratch_shapes=[pltpu.VMEM((nq, 2 * tq, 2 * LANES), BF16), pltpu.VMEM((nq, tq, LANES), F32),
                        pltpu.VMEM((nq, 2 * tq, 1), F32),
                        pltpu.VMEM((SB_RING, 2 * tq, tq), F32), pltpu.VMEM((SB_RING, 2 * tq, tq), F32),
                        pltpu.VMEM((SB_RING, 2 * tq, 1), F32)],
        compiler_params=_params(("parallel", "parallel")),
        name="sb_prompt",
    )(bias, q, k, v, ga, ntri)


def _sb_sample_kernel(pt_ref, q_ref, kn_ref, vn_ref, ga_ref, bias_ref, ntri_ref, *rest, ts, pages):
    page_refs = rest[:2 * pages]
    o_ref, acc_ref, carry_ref = rest[2 * pages:]
    s = pl.program_id(1)
    rows = H_A * ts
    ps = ntri_ref.shape[0]
    q = q_ref[...]
    qpair = [jnp.concatenate([q[:, (2 * p) * HD_A:(2 * p + 1) * HD_A], q[:, (2 * p + 1) * HD_A:(2 * p + 2) * HD_A]],
                             axis=0).astype(BF16) for p in range(H_A // 2)]
    bias = bias_ref[...]
    ntri = ntri_ref[...]

    def own_rows(x, h):
        return x[(h % 2) * ts:(h % 2 + 1) * ts]

    def sweep(n, qk, pv, carry, mask):
        z = jnp.concatenate(
            [jnp.concatenate([own_rows(qk(j, h, qpair[h // 2]), h) for h in range(H_A)], axis=0)
             for j in range(n)], axis=1) + bias
        nl = _sb_nl(z, mask)
        laters = [_sb_later(nl[:, j * ps:(j + 1) * ps], ntri) for j in range(n)]
        sums = []
        for j in range(n):
            sums.append(carry + laters[j])
            carry = carry + (laters[j][:, 0:1] - nl[:, j * ps:j * ps + 1])
        w = jnp.exp2((z - nl) + jnp.concatenate(sums, axis=1))
        if mask is not None:
            w = jnp.where(mask, w, 0.0)
        w = w.astype(BF16)
        parts = [jnp.concatenate([own_rows(pv(j, h, w[(h // 2) * 2 * ts:(h // 2 + 1) * 2 * ts, j * ps:(j + 1) * ps]), h)
                                  for h in range(H_A)], axis=0) for j in range(n)]
        contrib = parts[0]
        for c in parts[1:]:
            contrib = contrib + c
        return contrib, carry

    @pl.when(s == 0)
    def _():
        pad = jnp.zeros((ps - ts, HD_A), F32)
        kn = kn_ref[...]
        vn = vn_ref[...]
        new_k = lambda h: jnp.concatenate([kn[:, h * HD_A:(h + 1) * HD_A], pad], axis=0).astype(BF16)
        new_v = lambda h: jnp.concatenate([vn[:, h * HD_A:(h + 1) * HD_A], pad], axis=0).astype(BF16)
        r2 = lax.broadcasted_iota(jnp.int32, (rows, ps), 0)
        c2 = lax.broadcasted_iota(jnp.int32, (rows, ps), 1)
        mask = c2 < lax.rem(r2, ts)
        contrib, carry = sweep(1, lambda j, h, q16: _dot_nt(q16, new_k(h)), lambda j, h, w16: _dot(w16, new_v(h)),
                               jnp.zeros((rows, 1), F32), mask)
        acc_ref[...] = contrib
        carry_ref[...] = carry

    contrib, carry = sweep(pages,
                           lambda j, h, q16: _dot(q16, page_refs[2 * j][h].astype(BF16)),
                           lambda j, h, w16: _dot_nt(w16, page_refs[2 * j + 1][h].astype(BF16)),
                           carry_ref[...], None)
    acc_ref[...] += contrib
    carry_ref[...] = carry

    @pl.when(s == pl.num_programs(1) - 1)
    def _():
        acc = acc_ref[...]
        o = jnp.concatenate([acc[h * ts:(h + 1) * ts] for h in range(H_A)], axis=-1)
        o_ref[...] = (o * ga_ref[...]).astype(o_ref.dtype)


def _sb_sample(q, kn, vn, ga, bias_rows, ntri, cache_k, cache_v, page_table, ts, pages):
    n = q.shape[0]
    nb, n_pages = page_table.shape
    ps = cache_k.shape[3]
    steps = n_pages // pages
    tok = lambda b, s, pt: (b, 0)
    const = lambda b, s, pt: (0, 0)

    def page_map(j):
        return lambda b, s, pt: (pt[b, n_pages - 1 - (s * pages + j)], 0, 0, 0)

    page_specs, page_args = [], []
    for j in range(pages):
        page_specs += [pl.BlockSpec((None, H_A, HD_A, ps), page_map(j))] * 2
        page_args += [cache_k, cache_v]
    rows = H_A * ts
    grid_spec = pltpu.PrefetchScalarGridSpec(
        num_scalar_prefetch=1,
        grid=(nb, steps),
        in_specs=[pl.BlockSpec((ts, W_A), tok)] * 4
                 + [pl.BlockSpec((rows, 1), const), pl.BlockSpec((ps, ps), const)] + page_specs,
        out_specs=pl.BlockSpec((ts, W_A), tok),
        scratch_shapes=[pltpu.VMEM((rows, HD_A), F32), pltpu.VMEM((rows, 1), F32)],
    )
    return pl.pallas_call(
        functools.partial(_sb_sample_kernel, ts=ts, pages=pages),
        grid_spec=grid_spec,
        out_shape=jax.ShapeDtypeStruct((n, W_A), F32),
        compiler_params=_params(("parallel", "arbitrary")),
        name="sb_sample",
    )(page_table, q, kn, vn, ga, bias_rows, ntri, *page_args)


def _hgrn_tables(c, n_valid):
    levels = math.ceil(math.log2(n_valid))
    lm = np.zeros((levels, c, c), np.float32)
    r = np.arange(c)
    for l in range(levels):
        m = 2 ** l
        second = (r // m) % 2 == 1
        same = (r[:, None] // (2 * m)) == (r[None, :] // (2 * m))
        lm[l] = (same & second[:, None] & (~second)[None, :]).astype(np.float32)
    a = np.tril(np.ones((c, c), np.float32))
    a[:, n_valid:] = 0.0
    return a, lm


def _hold_mid(b, m, rowi):
    c = b.shape[0]
    if m == 1:
        return jnp.where(jnp.bitwise_and(rowi, 1) == 1, pltpu.roll(b, 1, 0), b)
    if m == 2:
        lo = jnp.concatenate([jnp.broadcast_to(b[g + 1:g + 2], (8, LANES)) for g in range(0, c, 8)], axis=0)
        hi = jnp.concatenate([jnp.broadcast_to(b[g + 5:g + 6], (8, LANES)) for g in range(0, c, 8)], axis=0)
        return jnp.where(jnp.bitwise_and(rowi, 4) == 0, lo, hi)
    return jnp.concatenate([jnp.broadcast_to(b[g + m - 1:g + m], (2 * m, LANES)) for g in range(0, c, 2 * m)], axis=0)


def _hgrn_kernel(q_ref, f_ref, i_ref, g_ref, s0_ref, lbp_ref, og_ref, a_ref, lm_ref,
                 o_ref, sout_ref, st_ref, *, c, n_chunks, n_valid, layer, nb):
    t = pl.program_id(1)
    levels = lm_ref.shape[0]
    chains = [(bi, h) for bi in range(nb) for h in range(H_B)]

    @pl.when(t == 0)
    def _():
        for bi, h in chains:
            st_ref[bi, h] = s0_ref[bi, h].T

    lbp = lbp_ref[...]
    p = jnp.exp(lbp - jnp.max(lbp, axis=0, keepdims=True))
    lb = jnp.sum(p[:layer + 1], axis=0, keepdims=True) / jnp.sum(p, axis=0, keepdims=True)
    og = og_ref[...]
    rowi = lax.broadcasted_iota(jnp.int32, (c, 1), 0)
    valid = rowi < n_valid

    def load(ref, bi, ci):
        if n_valid < c:
            x = ref[bi]
            return jnp.concatenate([x, jnp.zeros((c - x.shape[0], x.shape[1]), F32)], axis=0)
        return ref[bi, pl.ds(pl.multiple_of(ci * c, c), c), :]

    def chunk(ci, _):
        gs, kks, qqs, vvs = [], [], [], []
        for bi in range(nb):
            xf = load(f_ref, bi, ci)
            e = jnp.exp(-jnp.abs(xf))
            r = 1.0 / (1.0 + e)
            sig_pos = jnp.where(xf >= 0, r, e * r)
            sig_neg = jnp.where(xf >= 0, e * r, r)
            g = jnp.log(lb + (1.0 - lb) * sig_pos)
            kk = (1.0 - lb) * sig_neg
            if n_valid < c:
                g = jnp.where(valid, g, 0.0)
                kk = jnp.where(valid, kk, 0.0)
            gs.append(g)
            kks.append(kk)
            qqs.append(_silu(load(q_ref, bi, ci)))
            vvs.append(load(i_ref, bi, ci))
        ghi, gmid, glo = _split3(jnp.concatenate(gs, axis=1))
        a = a_ref[...]
        bcum_all = _dot(a, ghi) + _dot(a, gmid) + _dot(a, glo)
        outs = [[] for _ in range(nb)]
        for bi, h in chains:
            hl = slice(h * LANES, (h + 1) * LANES)
            el = slice((bi * H_B + h) * LANES, (bi * H_B + h + 1) * LANES)
            qq, kk, vv = qqs[bi][:, hl], kks[bi][:, hl], vvs[bi][:, hl]
            vv16 = vv.astype(BF16)
            bcum = bcum_all[:, el]
            rem = bcum[c - 1:c, :] - bcum
            st = st_ref[bi, h]
            o = _dot_nt((qq * jnp.exp(bcum)).astype(BF16), st.astype(BF16))
            scores = jnp.zeros((c, c), F32)
            for l in range(levels):
                second = jnp.bitwise_and(jnp.right_shift(rowi, l), 1) == 1
                d = bcum - _hold_mid(bcum, 2 ** l, rowi)
                x = jnp.exp(jnp.where(second, d, -d))
                scores = scores + _dot_nt((qq * x).astype(BF16), (kk * x).astype(BF16)) * lm_ref[l]
            o = o + _dot(scores.astype(BF16), vv16)
            o = o + jnp.sum(qq * kk, axis=-1, keepdims=True) * vv
            dec_last = jnp.exp(bcum[c - 1:c, :])
            st_ref[bi, h] = dec_last * st + _dot_tn(vv16, (kk * jnp.exp(rem)).astype(BF16))
            outs[bi].append(_rms(o) * og)
        for bi in range(nb):
            ob = jnp.concatenate(outs[bi], axis=1) * load(g_ref, bi, ci)
            if n_valid < c:
                o_ref[bi] = ob[:n_valid].astype(o_ref.dtype)
            else:
                o_ref[bi, pl.ds(pl.multiple_of(ci * c, c), c), :] = ob.astype(o_ref.dtype)
        return 0

    per_trip = 2 if n_chunks % 2 == 0 else 1

    def trip(i, _):
        for u in range(per_trip):
            chunk(i * per_trip + u, 0)
        return 0

    lax.fori_loop(0, n_chunks // per_trip, trip, 0)

    @pl.when(t == pl.num_programs(1) - 1)
    def _():
        for bi, h in chains:
            sout_ref[bi, h] = st_ref[bi, h].T


def _hgrn(qb, fb, ib, gb, s0, lbp, og, batch, seq, layer, out_dtype, tb, nb):
    n = qb.shape[0]
    c = HGRN_CHUNK
    if seq >= c:
        n_valid, rows_blk, nt, n_chunks = c, tb, seq // tb, tb // c
    else:
        n_valid, rows_blk, nt, n_chunks = seq, seq, 1, 1
    a_np, lm_np = _hgrn_tables(c, n_valid)
    a = jnp.asarray(a_np, BF16)
    lm = jnp.asarray(lm_np, F32)
    tok = lambda b, t: (b, t, 0)
    st = lambda b, t: (b, 0, 0, 0)
    blk = pl.BlockSpec((nb, rows_blk, W_B), tok)
    st_blk = pl.BlockSpec((nb, H_B, DK_B, DV_B), st)
    r3 = lambda x: x.reshape(batch, seq, W_B)
    mix, s_out = pl.pallas_call(
        functools.partial(_hgrn_kernel, c=c, n_chunks=n_chunks, n_valid=n_valid, layer=layer, nb=nb),
        grid=(batch // nb, nt),
        in_specs=[blk, blk, blk, blk, st_blk,
                  pl.BlockSpec(lbp.shape, lambda b, t: (0, 0)),
                  pl.BlockSpec((1, LANES), lambda b, t: (0, 0)),
                  pl.BlockSpec(a.shape, lambda b, t: (0, 0)),
                  pl.BlockSpec(lm.shape, lambda b, t: (0, 0, 0))],
        out_specs=[blk, st_blk],
        out_shape=[jax.ShapeDtypeStruct((batch, seq, W_B), out_dtype),
                   jax.ShapeDtypeStruct((batch, H_B, DK_B, DV_B), F32)],
        scratch_shapes=[pltpu.VMEM((nb, H_B, DV_B, DK_B), F32)],
        compiler_params=_params(("parallel", "arbitrary")),
        name="hgrn",
    )(r3(qb), r3(fb), r3(ib), r3(gb), s0, lbp, og, a, lm)
    return mix.reshape(n, W_B), s_out


def _outproj_even_kernel(x_ref, ma_ref, mb_ref, w_ref, y_ref):
    y = x_ref[...] + _dot(ma_ref[...].astype(BF16), w_ref[:W_A, :]) + _dot(mb_ref[...].astype(BF16), w_ref[W_A:, :])
    y_ref[...] = y


def _outproj_even(x, ma, mb, w, tm):
    n, d = x.shape
    row = lambda i: (i, 0)
    return pl.pallas_call(
        _outproj_even_kernel,
        grid=(n // tm,),
        in_specs=[pl.BlockSpec((tm, d), row), pl.BlockSpec((tm, W_A), row), pl.BlockSpec((tm, W_B), row),
                  pl.BlockSpec(w.shape, lambda i: (0, 0))],
        out_specs=pl.BlockSpec((tm, d), row),
        out_shape=jax.ShapeDtypeStruct((n, d), F32),
        compiler_params=_params(("parallel",)),
        name="outproj_even",
    )(x, ma, mb, w)


def _outproj_even_inproj_odd_kernel(x_ref, ma_ref, mb_ref, wo_ref, ng_ref, w_ref, cos_ref, sin_ref,
                                    y_ref, q_ref, k_ref, v_ref, g_ref):
    y = (x_ref[...] + _dot(ma_ref[...].astype(BF16), wo_ref[:W_A, :])
         + _dot(mb_ref[...].astype(BF16), wo_ref[W_A:, :]))
    y_ref[...] = y
    _inproj_odd_body(y, ng_ref, w_ref, cos_ref, sin_ref, q_ref, k_ref, v_ref, g_ref)


def _inproj_odd_body(x, ng_ref, w_ref, cos_ref, sin_ref, q_ref, k_ref, v_ref, g_ref):
    h = (_rms(x) * ng_ref[...]).astype(BF16)
    cos = cos_ref[...]
    sin = sin_ref[...]
    half = DK_C // 2

    def rot(a, scale):
        outs = []
        for hd in range(H_C):
            x1 = a[:, hd * DK_C:hd * DK_C + half]
            x2 = a[:, hd * DK_C + half:(hd + 1) * DK_C]
            outs += [(x1 * cos - x2 * sin) * scale, (x2 * cos + x1 * sin) * scale]
        return jnp.concatenate(outs, axis=-1)

    q_ref[...] = rot(_dot(h, w_ref[:, :QK_C]), 1.0).astype(q_ref.dtype)
    k_ref[...] = rot(_dot(h, w_ref[:, QK_C:2 * QK_C]), DK_C ** -0.5)
    for j in range(2):
        lo = 2 * QK_C + j * QK_C
        v_ref[:, j * QK_C:(j + 1) * QK_C] = _dot(h, w_ref[:, lo:lo + QK_C]).astype(v_ref.dtype)
    for j in range(2):
        lo = 2 * QK_C + W_C + j * QK_C
        g_ref[:, j * QK_C:(j + 1) * QK_C] = _silu(_dot(h, w_ref[:, lo:lo + QK_C]))


def _outproj_even_inproj_odd(x, ma, mb, wo, ng, w, cos, sin, act_dtype, tm):
    n, d = x.shape
    npos = cos.shape[0] // tm
    row = lambda i: (i, 0)
    const = lambda i: (0, 0)
    pos = lambda i: (i % npos, 0)
    return pl.pallas_call(
        _outproj_even_inproj_odd_kernel,
        grid=(n // tm,),
        in_specs=[pl.BlockSpec((tm, d), row), pl.BlockSpec((tm, W_A), row), pl.BlockSpec((tm, W_B), row),
                  pl.BlockSpec(wo.shape, const, pipeline_mode=pl.Buffered(1)), pl.BlockSpec((1, d), const),
                  pl.BlockSpec(w.shape, const, pipeline_mode=pl.Buffered(1)),
                  pl.BlockSpec((tm, DK_C // 2), pos), pl.BlockSpec((tm, DK_C // 2), pos)],
        out_specs=[pl.BlockSpec((tm, d), row), pl.BlockSpec((tm, QK_C), row), pl.BlockSpec((tm, QK_C), row),
                   pl.BlockSpec((tm, W_C), row), pl.BlockSpec((tm, W_C), row)],
        out_shape=[jax.ShapeDtypeStruct((n, d), F32),
                   jax.ShapeDtypeStruct((n, QK_C), act_dtype), jax.ShapeDtypeStruct((n, QK_C), F32),
                   jax.ShapeDtypeStruct((n, W_C), act_dtype), jax.ShapeDtypeStruct((n, W_C), F32)],
        compiler_params=_params(("parallel",)),
        name="outproj_even_inproj_odd",
    )(x, ma, mb, wo, ng, w, cos, sin)


def _ret_kernel(q_ref, k_ref, v_ref, g_ref, s0_ref, idec_ref, qdec_ref, kdec_ref, cdec_ref,
                o_ref, sout_ref, s_ref, *, c, n_chunks, n_valid, nb, hpg):
    t = pl.program_id(2)
    chains = [(bi, hh) for bi in range(nb) for hh in range(hpg)]

    @pl.when(t == 0)
    def _():
        s_ref[...] = s0_ref[...]

    def load(ref, bi, hh, width, ci):
        cols = slice(hh * width, (hh + 1) * width)
        if n_valid < c:
            x = ref[bi, :, cols].astype(F32)
            return jnp.concatenate([x, jnp.zeros((c - x.shape[0], x.shape[1]), F32)], axis=0)
        return ref[bi, pl.ds(pl.multiple_of(ci * c, c), c), cols]

    def chunk(ci, _):
        for bi, hh in chains:
            qc = load(q_ref, bi, hh, DK_C, ci).astype(BF16)
            kc = load(k_ref, bi, hh, DK_C, ci)
            vc = load(v_ref, bi, hh, DV_C, ci).astype(BF16)
            s = s_ref[bi, hh]
            scores = _dot_nt(qc, kc.astype(BF16)) * idec_ref[hh]
            o = _dot(scores.astype(BF16), vc) + _dot(qc, s.astype(BF16)) * qdec_ref[hh, :, 0:1]
            s_ref[bi, hh] = cdec_ref[hh, 0:1, 0:1] * s + _dot_tn((kc * kdec_ref[hh, :, 0:1]).astype(BF16), vc)
            ob = _rms(o) * load(g_ref, bi, hh, DV_C, ci)
            cols = slice(hh * DV_C, (hh + 1) * DV_C)
            if n_valid < c:
                o_ref[bi, :, cols] = ob[:n_valid].astype(o_ref.dtype)
            else:
                o_ref[bi, pl.ds(pl.multiple_of(ci * c, c), c), cols] = ob.astype(o_ref.dtype)
        return 0

    lax.fori_loop(0, n_chunks, chunk, 0)

    @pl.when(t == pl.num_programs(2) - 1)
    def _():
        sout_ref[...] = s_ref[...]


def _ret_tables(c, chunk):
    f32 = jnp.float32
    log_gamma = jnp.log1p(-jnp.exp2(-5.0 - jnp.arange(H_C, dtype=f32)))
    idx = jnp.arange(c, dtype=f32)
    real = idx < chunk
    rel = idx[:, None] - idx[None, :]
    ok = (rel >= 0) & real[:, None] & real[None, :]
    idec = jnp.exp(jnp.where(ok[None], rel[None] * log_gamma[:, None, None], -jnp.inf))
    qdec = jnp.where(real[None, :], jnp.exp((idx[None, :] + 1.0) * log_gamma[:, None]), 0.0)
    kdec = jnp.where(real[None, :], jnp.exp((chunk - 1.0 - idx[None, :]) * log_gamma[:, None]), 0.0)
    cdec = jnp.exp(chunk * log_gamma)
    bc = lambda x: jnp.broadcast_to(x[:, :, None], (H_C, c, LANES))
    return idec, bc(qdec), bc(kdec), jnp.broadcast_to(cdec[:, None, None], (H_C, 8, LANES))


def _ret(q, k, v, g, s0, batch, seq, out_dtype, tb, nb, hpg):
    n = q.shape[0]
    if seq >= RET_CHUNK:
        c = RET_CHUNK
        n_valid, rows_blk, nt, n_chunks = c, tb, seq // tb, tb // c
    else:
        c = RET_PAD_CHUNK
        n_valid, rows_blk, nt, n_chunks = seq, seq, 1, 1
    idec, qdec, kdec, cdec = _ret_tables(c, n_valid)
    tok = lambda hg, b, t: (b, t, hg)
    st = lambda hg, b, t: (b, hg, 0, 0)
    hd = lambda hg, b, t: (hg, 0, 0)
    qk_blk = pl.BlockSpec((nb, rows_blk, hpg * DK_C), tok)
    vg_blk = pl.BlockSpec((nb, rows_blk, hpg * DV_C), tok)
    st_blk = pl.BlockSpec((nb, hpg, DK_C, DV_C), st)
    r3 = lambda x: x.reshape(batch, seq, x.shape[-1])
    mix, s_out = pl.pallas_call(
        functools.partial(_ret_kernel, c=c, n_chunks=n_chunks, n_valid=n_valid, nb=nb, hpg=hpg),
        grid=(H_C // hpg, batch // nb, nt),
        in_specs=[qk_blk, qk_blk, vg_blk, vg_blk, st_blk,
                  pl.BlockSpec((hpg, c, c), hd), pl.BlockSpec((hpg, c, LANES), hd),
                  pl.BlockSpec((hpg, c, LANES), hd), pl.BlockSpec((hpg, 8, LANES), hd)],
        out_specs=[vg_blk, st_blk],
        out_shape=[jax.ShapeDtypeStruct((batch, seq, W_C), out_dtype),
                   jax.ShapeDtypeStruct((batch, H_C, DK_C, DV_C), F32)],
        scratch_shapes=[pltpu.VMEM((nb, hpg, DK_C, DV_C), F32)],
        compiler_params=_params(("parallel", "parallel", "arbitrary")),
        name="retention",
    )(r3(q), r3(k), r3(v), r3(g), s0, idec, qdec, kdec, cdec)
    return mix.reshape(n, W_C), s_out


def _outproj_odd_kernel(x_ref, m_ref, w_ref, y_ref):
    y_ref[...] = x_ref[...] + _dot(m_ref[...].astype(BF16), w_ref[...].astype(BF16))


def _outproj_odd(x, m, w, tm):
    n, d = x.shape
    row = lambda i: (i, 0)
    return pl.pallas_call(
        _outproj_odd_kernel,
        grid=(n // tm,),
        in_specs=[pl.BlockSpec((tm, d), row), pl.BlockSpec((tm, W_C), row),
                  pl.BlockSpec(w.shape, lambda i: (0, 0), pipeline_mode=pl.Buffered(1))],
        out_specs=pl.BlockSpec((tm, d), row),
        out_shape=jax.ShapeDtypeStruct((n, d), F32),
        compiler_params=_params(("parallel",)),
        name="outproj_odd",
    )(x, m, w)


def _rope_tables(pos):
    half = DK_C // 2
    inv = 1.0 / (ROPE_BASE ** jnp.linspace(0.0, 1.0, half, dtype=F32))
    ang = pos[:, None] * inv[None, :]
    return jnp.cos(ang), jnp.sin(ang)


def _ntri(n):
    return jnp.asarray(-np.tril(np.ones((n, n), np.float32), -1), BF16)


def kernel(x_prompt, x_sample, cache_k, cache_v, page_table, state_hgrn, state_ret, norm_g,
           w_in_even, w_out_even, q_norm_g, k_norm_g, sb_logit_bias, hgrn_lower_bounds,
           hgrn_out_norm_g, w_in_odd, w_out_odd):
    bp, tp, d = x_prompt.shape
    bs, ts, _ = x_sample.shape
    n_pool, page_size = cache_k.shape[1], cache_k.shape[2]
    past_len = page_table.shape[1] * page_size
    depth = norm_g.shape[0]
    tm_p = 512
    tm_o = 512
    tm_s = bs * ts

    yp = x_prompt.reshape(bp * tp, d)
    ys = x_sample.reshape(bs * ts, d)
    grp = jnp.asarray(np.kron(np.eye(H_A, dtype=np.float32), np.ones((HD_A, HD_A), np.float32)), BF16)
    cos_p, sin_p = _rope_tables(jnp.arange(tp, dtype=F32))
    cos_s, sin_s = _rope_tables(past_len + jnp.arange(ts, dtype=F32))
    cos_s, sin_s = jnp.tile(cos_s, (bs, 1)), jnp.tile(sin_s, (bs, 1))

    k_p, v_p, k_s, v_s, hg_p, hg_s, rt_p, rt_s = [], [], [], [], [], [], [], []
    for layer in range(depth):
        e = layer // 2
        ng = norm_g[layer].reshape(1, d)
        if layer % 2 == 0:
            w_in = w_in_even[e]
            w_out = w_out_even[e].astype(BF16)
            qg = jnp.tile(q_norm_g[e], H_A).reshape(1, W_A)
            kg = jnp.tile(k_norm_g[e], H_A).reshape(1, W_A)
            og = hgrn_out_norm_g[e].reshape(1, DV_B)
            bias = sb_logit_bias[e].astype(F32)
            q, k, v, k32, v32, ga, qb, fb, ib, gb = _inproj_even(yp, ng, w_in, qg, kg, grp, BF16, tm_p, kv_seq=tp)
            ma = _sb_prompt(q, k, v, ga, bias, _ntri(SB_TQ), bp, tp, SB_TQ)
            mb, hs = _hgrn(qb, fb, ib, gb, jnp.zeros((bp, H_B, DK_B, DV_B), F32), hgrn_lower_bounds,
                           og, bp, tp, e, BF16, 512, HGRN_NB)
            fuse_next = layer + 1 < depth
            if fuse_next:
                ng_next = norm_g[layer + 1].reshape(1, d)
                w_in_next = w_in_odd[e].astype(BF16)
                yp, *odd_in_p = _outproj_even_inproj_odd(yp, ma, mb, w_out, ng_next, w_in_next, cos_p, sin_p, BF16, tm_p)
            else:
                yp = _outproj_even(yp, ma, mb, w_out, tm_o)
            k_p.append(jnp.transpose(k32.reshape(bp, H_A, HD_A, tp), (0, 3, 1, 2)))
            v_p.append(jnp.transpose(v32.reshape(bp, H_A, HD_A, tp), (0, 3, 1, 2)))
            hg_p.append(hs)
            q, k, v, k32, v32, ga, qb, fb, ib, gb = _inproj_even(ys, ng, w_in, qg, kg, grp, F32, tm_s)
            ma = _sb_sample(q, k, v, ga, jnp.repeat(bias * LOG2E, ts).reshape(H_A * ts, 1), _ntri(page_size),
                            jnp.transpose(cache_k[e], (0, 2, 3, 1)), jnp.transpose(cache_v[e], (0, 2, 3, 1)),
                            page_table, ts, SB_PAGES_PER_STEP)
            mb, hs = _hgrn(qb, fb, ib, gb, state_hgrn[e].astype(F32), hgrn_lower_bounds, og, bs, ts, e, F32, ts,
                           HGRN_NB)
            if fuse_next:
                ys, *odd_in_s = _outproj_even_inproj_odd(ys, ma, mb, w_out, ng_next, w_in_next, cos_s, sin_s, F32, tm_s)
            else:
                ys = _outproj_even(ys, ma, mb, w_out, tm_s)
            k_s.append(k32.reshape(bs, ts, H_A, HD_A))
            v_s.append(v32.reshape(bs, ts, H_A, HD_A))
            hg_s.append(hs)
        else:
            w_out = w_out_odd[e]
            q, k, v, g = odd_in_p
            m, s = _ret(q, k, v, g, jnp.zeros((bp, H_C, DK_C, DV_C), F32), bp, tp, BF16, 512, RET_NB, RET_HPG)
            yp = _outproj_odd(yp, m, w_out, tm_o)
            rt_p.append(s)
            q, k, v, g = odd_in_s
            m, s = _ret(q, k, v, g, state_ret[e].astype(F32), bs, ts, F32, ts, RET_NB, RET_HPG)
            ys = _outproj_odd(ys, m, w_out, tm_s)
            rt_s.append(s)
    return (yp.reshape(bp, tp, d), ys.reshape(bs, ts, d), jnp.stack(k_p), jnp.stack(v_p), jnp.stack(k_s),
            jnp.stack(v_s), jnp.stack(hg_p), jnp.stack(hg_s), jnp.stack(rt_p), jnp.stack(rt_s))
```

```python
import functools
import math

import numpy as np
import jax
import jax.numpy as jnp
from jax import lax
from jax.experimental import pallas as pl
from jax.experimental.pallas import tpu as pltpu

F32 = jnp.float32
BF16 = jnp.bfloat16

H_A, HD_A = 8, 64
W_A = H_A * HD_A
H_B, DK_B, DV_B = 4, 128, 128
W_B = H_B * DV_B
H_C, DK_C, DV_C = 4, 256, 512
QK_C = H_C * DK_C
W_C = H_C * DV_C
ROPE_BASE = 10000.0
NORM_EPS = 1e-6
LOG2E = 1.4426950408889634

LANES = 128
MXU_DEPTH = 256
VMEM_LIMIT = 56 * 1024 * 1024

HGRN_CHUNK = 64
HGRN_NB = 2
RET_CHUNK = MXU_DEPTH
RET_PAD_CHUNK = 64
RET_NB, RET_HPG = 2, 2
SB_TQ = MXU_DEPTH
SB_RING = 3
SB_UNROLL = 8 * SB_RING
SB_PAGES_PER_STEP = 8


def _dot(a, b):
    return jnp.dot(a, b, preferred_element_type=F32)


def _dot_nt(a, b):
    return lax.dot_general(a, b, (((1,), (1,)), ((), ())), preferred_element_type=F32)


def _dot_tn(a, b):
    return lax.dot_general(a, b, (((0,), (0,)), ((), ())), preferred_element_type=F32)


def _split2(x):
    hi = x.astype(BF16)
    lo = (x - hi.astype(F32)).astype(BF16)
    return hi, lo


def _split3(x):
    hi = x.astype(BF16)
    r = x - hi.astype(F32)
    mid = r.astype(BF16)
    lo = (r - mid.astype(F32)).astype(BF16)
    return hi, mid, lo


def _sigmoid(x):
    return 1.0 / (1.0 + jnp.exp(-x))


def _silu(x):
    return x * _sigmoid(x)


def _rms(x, eps=NORM_EPS):
    return x * lax.rsqrt(jnp.mean(x * x, axis=-1, keepdims=True) + eps)


def _params(sem):
    return pltpu.CompilerParams(dimension_semantics=sem, vmem_limit_bytes=VMEM_LIMIT)


def _inproj_even_kernel(x_ref, ng_ref, w_ref, qg_ref, kg_ref, grp_ref,
                        q_ref, k_ref, v_ref, k32_ref, v32_ref, ga_ref,
                        qb_ref, fb_ref, ib_ref, gb_ref, *, kv_token_minor):
    x = x_ref[...]
    h = (_rms(x) * ng_ref[...]).astype(BF16)

    def proj(c):
        return _dot(h, w_ref[:, c * W_A:(c + 1) * W_A].astype(BF16))

    def head_norm(a, gain):
        hi, lo = _split2(a * a)
        ssum = _dot(hi, grp_ref[...]) + _dot(lo, grp_ref[...])
        return a * lax.rsqrt(ssum * (1.0 / HD_A) + NORM_EPS) * gain

    qa = head_norm(proj(0), qg_ref[...])
    q_ref[...] = (qa * (HD_A ** -0.5 * LOG2E)).astype(q_ref.dtype)
    ka = head_norm(proj(1), kg_ref[...])
    k32_ref[...] = ka.T if kv_token_minor else ka
    k_ref[...] = ka.astype(k_ref.dtype)
    va = proj(2)
    v32_ref[...] = va.T if kv_token_minor else va
    v_ref[...] = va.astype(v_ref.dtype)
    ga_ref[...] = _silu(proj(3))
    qb_ref[...] = proj(4)
    fb_ref[...] = proj(5)
    ib_ref[...] = proj(6)
    gb_ref[...] = _silu(proj(7))


def _inproj_even(x, ng, w, qg, kg, grp, act_dtype, tm, kv_seq=None):
    n, d = x.shape
    row = lambda i: (i, 0)
    const = lambda i: (0, 0)
    blk = pl.BlockSpec((tm, W_A), row)
    outs = [jax.ShapeDtypeStruct((n, W_A), act_dtype)] * 3 + [jax.ShapeDtypeStruct((n, W_A), F32)] * 7
    out_specs = [blk] * 10
    if kv_seq is not None:
        nt = kv_seq // tm
        kv_blk = pl.BlockSpec((None, W_A, tm), lambda i: (i // nt, 0, i % nt))
        kv_shape = jax.ShapeDtypeStruct((n // kv_seq, W_A, kv_seq), F32)
        outs[3:5] = [kv_shape, kv_shape]
        out_specs[3:5] = [kv_blk, kv_blk]
    return pl.pallas_call(
        functools.partial(_inproj_even_kernel, kv_token_minor=kv_seq is not None),
        grid=(n // tm,),
        in_specs=[pl.BlockSpec((tm, d), row), pl.BlockSpec((1, d), const),
                  pl.BlockSpec(w.shape, const, pipeline_mode=pl.Buffered(1)), pl.BlockSpec((1, W_A), const),
                  pl.BlockSpec((1, W_A), const), pl.BlockSpec((W_A, W_A), const)],
        out_specs=out_specs,
        out_shape=outs,
        compiler_params=_params(("parallel",)),
        name="inproj_even",
    )(x, ng, w, qg, kg, grp)


def _sb_nl(z, mask):
    neg_abs = lax.bitcast_convert_type(lax.bitcast_convert_type(z, jnp.uint32) | jnp.uint32(0x80000000), F32)
    nl = jnp.maximum(z, 0.0) + jnp.log(1.0 + jnp.exp2(neg_abs)) * LOG2E
    if mask is not None:
        nl = jnp.where(mask, nl, 0.0)
    return nl


def _sb_later(nl, ntri):
    return _dot(nl.astype(BF16), ntri)


def _sb_prompt_kernel(bias_ref, q_ref, k_ref, v_ref, ga_ref, ntri_ref, o_ref, qs_ref, acc_ref, carry_ref,
                      z_ref, zi_ref, col_ref, *, tq, nq):
    hp = pl.program_id(1)
    rows = 2 * tq
    lane = lax.broadcasted_iota(jnp.int32, (tq, LANES), 1)

    b0 = bias_ref[2 * hp] * LOG2E
    b1 = bias_ref[2 * hp + 1] * LOG2E

    def blk(i):
        return pl.ds(pl.multiple_of(i * tq, tq), tq)

    def logits(qs, kb):
        zr = _dot_nt(qs, k_ref[blk(kb), :])
        return jnp.concatenate([zr[:tq] + b0, zr[tq:] + b1], axis=0)

    r2 = lax.broadcasted_iota(jnp.int32, (rows, tq), 0)
    c2 = lax.broadcasted_iota(jnp.int32, (rows, tq), 1)
    mask = c2 < jnp.where(r2 >= tq, r2 - tq, r2)

    def stage_a(p, slot, diag):
        if diag:
            q = q_ref[blk(p[0]), :].astype(F32)
            qs = jnp.concatenate([jnp.where(lane < HD_A, q, 0.0), jnp.where(lane >= HD_A, q, 0.0)],
                                 axis=0).astype(BF16)
            qs_ref[p[0]] = qs
        else:
            qs = qs_ref[p[0]]
        z_ref[slot] = logits(qs, p[1])

    def stage_b(slot, diag):
        z = z_ref[slot]
        nl = _sb_nl(z, mask if diag else None)
        zi_ref[slot] = z - nl
        later = _sb_later(nl, ntri_ref[...])
        zi_ref[slot] += later
        col_ref[slot] = later[:, 0:1] - nl[:, 0:1]

    def stage_c(p, slot, diag):
        def own_lanes(x):
            return jnp.where(lane < HD_A, x[:tq], x[tq:])

        if diag:
            w = jnp.where(mask, jnp.exp2(zi_ref[slot]), 0.0)
            acc_ref[p[0]] = own_lanes(_dot(w.astype(BF16), v_ref[blk(p[1]), :]))
            carry_ref[p[0]] = col_ref[slot]
        else:
            carry = carry_ref[p[0]]
            w = jnp.exp2(zi_ref[slot] + carry)
            acc_ref[p[0]] += own_lanes(_dot(w.astype(BF16), v_ref[blk(p[1]), :]))
            carry_ref[p[0]] = carry + col_ref[slot]

    def pipeline(n_steps, p0, nxt, diag):
        if n_steps == 0:
            return
        stage_a(p0, 0, diag)
        stage_b(0, diag)
        if n_steps == 1:
            stage_c(p0, 0, diag)
            return
        stage_a(nxt(p0), 1, diag)
        n_steady = n_steps - 2

        def step(t, pc, pa):
            stage_c(pc, t % SB_RING, diag)
            stage_b((t + 1) % SB_RING, diag)
            stage_a(pa, (t + 2) % SB_RING, diag)
            return nxt(pc), nxt(pa)

        def body(_, st):
            pc, pa = st[:2], st[2:]
            for u in range(SB_UNROLL):
                pc, pa = step(u, pc, pa)
            return (*pc, *pa)

        n_trips = n_steady // SB_UNROLL
        st = lax.fori_loop(0, n_trips, body, (*p0, *nxt(nxt(p0))))
        pc, pa = st[:2], st[2:]
        for t in range(n_trips * SB_UNROLL, n_steady):
            pc, pa = step(t, pc, pa)
        stage_c(pc, n_steady % SB_RING, diag)
        stage_b((n_steady + 1) % SB_RING, diag)
        stage_c(nxt(pc), (n_steady + 1) % SB_RING, diag)

    def next_off_diagonal(p):
        wrap = p[1] == 0
        return jnp.where(wrap, p[0] + 1, p[0]), jnp.where(wrap, p[0], p[1] - 1)

    pipeline(nq, (jnp.int32(0), jnp.int32(0)), lambda p: (p[0] + 1, p[1] + 1), True)
    pipeline(nq * (nq - 1) // 2, (jnp.int32(1), jnp.int32(0)), next_off_diagonal, False)

    def finish(qi, _):
        o_ref[blk(qi), :] = (acc_ref[qi] * ga_ref[blk(qi), :]).astype(o_ref.dtype)
        return 0

    lax.fori_loop(0, nq, finish, 0)


def _sb_prompt(q, k, v, ga, bias, ntri, batch, seq, tq):
    n = q.shape[0]
    nq = seq // tq
    seqmap = lambda b, hp: (b, hp)
    blk = pl.BlockSpec((seq, LANES), seqmap)
    return pl.pallas_call(
        functools.partial(_sb_prompt_kernel, tq=tq, nq=nq),
        grid=(batch, H_A // 2),
        in_specs=[pl.BlockSpec(memory_space=pltpu.SMEM), blk, blk, blk, blk,
                  pl.BlockSpec((tq, tq), lambda b, hp: (0, 0))],
        out_specs=blk,
        out_shape=jax.ShapeDtypeStruct((n, W_A), BF16),
        scratch_shapes=[pltpu.VMEM((nq, 2 * tq, LANES), BF16), pltpu.VMEM((nq, tq, LANES), F32),
                        pltpu.VMEM((nq, 2 * tq, 1), F32),
                        pltpu.VMEM((SB_RING, 2 * tq, tq), F32), pltpu.VMEM((SB_RING, 2 * tq, tq), F32),
                        pltpu.VMEM((SB_RING, 2 * tq, 1), F32)],
        compiler_params=_params(("parallel", "parallel")),
        name="sb_prompt",
    )(bias, q, k, v, ga, ntri)


def _sb_sample_kernel(pt_ref, q_ref, kn_ref, vn_ref, ga_ref, bias_ref, ntri_ref, *rest, ts, pages):
    page_refs = rest[:2 * pages]
    o_ref, acc_ref, carry_ref = rest[2 * pages:]
    s = pl.program_id(1)
    rows = H_A * ts
    ps = ntri_ref.shape[0]
    q = q_ref[...]
    qpair = [jnp.concatenate([q[:, (2 * p) * HD_A:(2 * p + 1) * HD_A], q[:, (2 * p + 1) * HD_A:(2 * p + 2) * HD_A]],
                             axis=0).astype(BF16) for p in range(H_A // 2)]
    bias = bias_ref[...]
    ntri = ntri_ref[...]

    def own_rows(x, h):
        return x[(h % 2) * ts:(h % 2 + 1) * ts]

    def sweep(n, qk, pv, carry, mask):
        z = jnp.concatenate(
            [jnp.concatenate([own_rows(qk(j, h, qpair[h // 2]), h) for h in range(H_A)], axis=0)
             for j in range(n)], axis=1) + bias
        nl = _sb_nl(z, mask)
        laters = [_sb_later(nl[:, j * ps:(j + 1) * ps], ntri) for j in range(n)]
        sums = []
        for j in range(n):
            sums.append(carry + laters[j])
            carry = carry + (laters[j][:, 0:1] - nl[:, j * ps:j * ps + 1])
        w = jnp.exp2((z - nl) + jnp.concatenate(sums, axis=1))
        if mask is not None:
            w = jnp.where(mask, w, 0.0)
        w = w.astype(BF16)
        parts = [jnp.concatenate([own_rows(pv(j, h, w[(h // 2) * 2 * ts:(h // 2 + 1) * 2 * ts, j * ps:(j + 1) * ps]), h)
                                  for h in range(H_A)], axis=0) for j in range(n)]
        contrib = parts[0]
        for c in parts[1:]:
            contrib = contrib + c
        return contrib, carry

    @pl.when(s == 0)
    def _():
        pad = jnp.zeros((ps - ts, HD_A), F32)
        kn = kn_ref[...]
        vn = vn_ref[...]
        new_k = lambda h: jnp.concatenate([kn[:, h * HD_A:(h + 1) * HD_A], pad], axis=0).astype(BF16)
        new_v = lambda h: jnp.concatenate([vn[:, h * HD_A:(h + 1) * HD_A], pad], axis=0).astype(BF16)
        r2 = lax.broadcasted_iota(jnp.int32, (rows, ps), 0)
        c2 = lax.broadcasted_iota(jnp.int32, (rows, ps), 1)
        mask = c2 < lax.rem(r2, ts)
        contrib, carry = sweep(1, lambda j, h, q16: _dot_nt(q16, new_k(h)), lambda j, h, w16: _dot(w16, new_v(h)),
                               jnp.zeros((rows, 1), F32), mask)
        acc_ref[...] = contrib
        carry_ref[...] = carry

    contrib, carry = sweep(pages,
                           lambda j, h, q16: _dot(q16, page_refs[2 * j][h].astype(BF16)),
                           lambda j, h, w16: _dot_nt(w16, page_refs[2 * j + 1][h].astype(BF16)),
                           carry_ref[...], None)
    acc_ref[...] += contrib
    carry_ref[...] = carry

    @pl.when(s == pl.num_programs(1) - 1)
    def _():
        acc = acc_ref[...]
        o = jnp.concatenate([acc[h * ts:(h + 1) * ts] for h in range(H_A)], axis=-1)
        o_ref[...] = (o * ga_ref[...]).astype(o_ref.dtype)


def _sb_sample(q, kn, vn, ga, bias_rows, ntri, cache_k, cache_v, page_table, ts, pages):
    n = q.shape[0]
    nb, n_pages = page_table.shape
    ps = cache_k.shape[3]
    steps = n_pages // pages
    tok = lambda b, s, pt: (b, 0)
    const = lambda b, s, pt: (0, 0)

    def page_map(j):
        return lambda b, s, pt: (pt[b, n_pages - 1 - (s * pages + j)], 0, 0, 0)

    page_specs, page_args = [], []
    for j in range(pages):
        page_specs += [pl.BlockSpec((None, H_A, HD_A, ps), page_map(j))] * 2
        page_args += [cache_k, cache_v]
    rows = H_A * ts
    grid_spec = pltpu.PrefetchScalarGridSpec(
        num_scalar_prefetch=1,
        grid=(nb, steps),
        in_specs=[pl.BlockSpec((ts, W_A), tok)] * 4
                 + [pl.BlockSpec((rows, 1), const), pl.BlockSpec((ps, ps), const)] + page_specs,
        out_specs=pl.BlockSpec((ts, W_A), tok),
        scratch_shapes=[pltpu.VMEM((rows, HD_A), F32), pltpu.VMEM((rows, 1), F32)],
    )
    return pl.pallas_call(
        functools.partial(_sb_sample_kernel, ts=ts, pages=pages),
        grid_spec=grid_spec,
        out_shape=jax.ShapeDtypeStruct((n, W_A), F32),
        compiler_params=_params(("parallel", "arbitrary")),
        name="sb_sample",
    )(page_table, q, kn, vn, ga, bias_rows, ntri, *page_args)


def _hgrn_tables(c, n_valid):
    levels = math.ceil(math.log2(n_valid))
    lm = np.zeros((levels, c, c), np.float32)
    r = np.arange(c)
    for l in range(levels):
        m = 2 ** l
        second = (r // m) % 2 == 1
        same = (r[:, None] // (2 * m)) == (r[None, :] // (2 * m))
        lm[l] = (same & second[:, None] & (~second)[None, :]).astype(np.float32)
    a = np.tril(np.ones((c, c), np.float32))
    a[:, n_valid:] = 0.0
    return a, lm


def _hold_mid(b, m, rowi):
    c = b.shape[0]
    if m == 1:
        return jnp.where(jnp.bitwise_and(rowi, 1) == 1, pltpu.roll(b, 1, 0), b)
    if m == 2:
        lo = jnp.concatenate([jnp.broadcast_to(b[g + 1:g + 2], (8, LANES)) for g in range(0, c, 8)], axis=0)
        hi = jnp.concatenate([jnp.broadcast_to(b[g + 5:g + 6], (8, LANES)) for g in range(0, c, 8)], axis=0)
        return jnp.where(jnp.bitwise_and(rowi, 4) == 0, lo, hi)
    return jnp.concatenate([jnp.broadcast_to(b[g + m - 1:g + m], (2 * m, LANES)) for g in range(0, c, 2 * m)], axis=0)


def _hgrn_kernel(q_ref, f_ref, i_ref, g_ref, s0_ref, lbp_ref, og_ref, a_ref, lm_ref,
                 o_ref, sout_ref, st_ref, *, c, n_chunks, n_valid, layer, nb):
    t = pl.program_id(1)
    levels = lm_ref.shape[0]
    chains = [(bi, h) for bi in range(nb) for h in range(H_B)]

    @pl.when(t == 0)
    def _():
        for bi, h in chains:
            st_ref[bi, h] = s0_ref[bi, h].T

    lbp = lbp_ref[...]
    p = jnp.exp(lbp - jnp.max(lbp, axis=0, keepdims=True))
    lb = jnp.sum(p[:layer + 1], axis=0, keepdims=True) / jnp.sum(p, axis=0, keepdims=True)
    og = og_ref[...]
    rowi = lax.broadcasted_iota(jnp.int32, (c, 1), 0)
    valid = rowi < n_valid

    def load(ref, bi, ci):
        if n_valid < c:
            x = ref[bi]
            return jnp.concatenate([x, jnp.zeros((c - x.shape[0], x.shape[1]), F32)], axis=0)
        return ref[bi, pl.ds(pl.multiple_of(ci * c, c), c), :]

    def chunk(ci, _):
        gs, kks, qqs, vvs = [], [], [], []
        for bi in range(nb):
            xf = load(f_ref, bi, ci)
            e = jnp.exp(-jnp.abs(xf))
            r = 1.0 / (1.0 + e)
            sig_pos = jnp.where(xf >= 0, r, e * r)
            sig_neg = jnp.where(xf >= 0, e * r, r)
            g = jnp.log(lb + (1.0 - lb) * sig_pos)
            kk = (1.0 - lb) * sig_neg
            if n_valid < c:
                g = jnp.where(valid, g, 0.0)
                kk = jnp.where(valid, kk, 0.0)
            gs.append(g)
            kks.append(kk)
            qqs.append(_silu(load(q_ref, bi, ci)))
            vvs.append(load(i_ref, bi, ci))
        ghi, gmid, glo = _split3(jnp.concatenate(gs, axis=1))
        a = a_ref[...]
        bcum_all = _dot(a, ghi) + _dot(a, gmid) + _dot(a, glo)
        outs = [[] for _ in range(nb)]
        for bi, h in chains:
            hl = slice(h * LANES, (h + 1) * LANES)
            el = slice((bi * H_B + h) * LANES, (bi * H_B + h + 1) * LANES)
            qq, kk, vv = qqs[bi][:, hl], kks[bi][:, hl], vvs[bi][:, hl]
            vv16 = vv.astype(BF16)
            bcum = bcum_all[:, el]
            rem = bcum[c - 1:c, :] - bcum
            st = st_ref[bi, h]
            o = _dot_nt((qq * jnp.exp(bcum)).astype(BF16), st.astype(BF16))
            scores = jnp.zeros((c, c), F32)
            for l in range(levels):
                second = jnp.bitwise_and(jnp.right_shift(rowi, l), 1) == 1
                d = bcum - _hold_mid(bcum, 2 ** l, rowi)
                x = jnp.exp(jnp.where(second, d, -d))
                scores = scores + _dot_nt((qq * x).astype(BF16), (kk * x).astype(BF16)) * lm_ref[l]
            o = o + _dot(scores.astype(BF16), vv16)
            o = o + jnp.sum(qq * kk, axis=-1, keepdims=True) * vv
            dec_last = jnp.exp(bcum[c - 1:c, :])
            st_ref[bi, h] = dec_last * st + _dot_tn(vv16, (kk * jnp.exp(rem)).astype(BF16))
            outs[bi].append(_rms(o) * og)
        for bi in range(nb):
            ob = jnp.concatenate(outs[bi], axis=1) * load(g_ref, bi, ci)
            if n_valid < c:
                o_ref[bi] = ob[:n_valid].astype(o_ref.dtype)
            else:
                o_ref[bi, pl.ds(pl.multiple_of(ci * c, c), c), :] = ob.astype(o_ref.dtype)
        return 0

    per_trip = 2 if n_chunks % 2 == 0 else 1

    def trip(i, _):
        for u in range(per_trip):
            chunk(i * per_trip + u, 0)
        return 0

    lax.fori_loop(0, n_chunks // per_trip, trip, 0)

    @pl.when(t == pl.num_programs(1) - 1)
    def _():
        for bi, h in chains:
            sout_ref[bi, h] = st_ref[bi, h].T


def _hgrn(qb, fb, ib, gb, s0, lbp, og, batch, seq, layer, out_dtype, tb, nb):
    n = qb.shape[0]
    c = HGRN_CHUNK
    if seq >= c:
        n_valid, rows_blk, nt, n_chunks = c, tb, seq // tb, tb // c
    else:
        n_valid, rows_blk, nt, n_chunks = seq, seq, 1, 1
    a_np, lm_np = _hgrn_tables(c, n_valid)
    a = jnp.asarray(a_np, BF16)
    lm = jnp.asarray(lm_np, F32)
    tok = lambda b, t: (b, t, 0)
    st = lambda b, t: (b, 0, 0, 0)
    blk = pl.BlockSpec((nb, rows_blk, W_B), tok)
    st_blk = pl.BlockSpec((nb, H_B, DK_B, DV_B), st)
    r3 = lambda x: x.reshape(batch, seq, W_B)
    mix, s_out = pl.pallas_call(
        functools.partial(_hgrn_kernel, c=c, n_chunks=n_chunks, n_valid=n_valid, layer=layer, nb=nb),
        grid=(batch // nb, nt),
        in_specs=[blk, blk, blk, blk, st_blk,
                  pl.BlockSpec(lbp.shape, lambda b, t: (0, 0)),
                  pl.BlockSpec((1, LANES), lambda b, t: (0, 0)),
                  pl.BlockSpec(a.shape, lambda b, t: (0, 0)),
                  pl.BlockSpec(lm.shape, lambda b, t: (0, 0, 0))],
        out_specs=[blk, st_blk],
        out_shape=[jax.ShapeDtypeStruct((batch, seq, W_B), out_dtype),
                   jax.ShapeDtypeStruct((batch, H_B, DK_B, DV_B), F32)],
        scratch_shapes=[pltpu.VMEM((nb, H_B, DV_B, DK_B), F32)],
        compiler_params=_params(("parallel", "arbitrary")),
        name="hgrn",
    )(r3(qb), r3(fb), r3(ib), r3(gb), s0, lbp, og, a, lm)
    return mix.reshape(n, W_B), s_out


def _outproj_even_kernel(x_ref, ma_ref, mb_ref, w_ref, y_ref):
    y = x_ref[...] + _dot(ma_ref[...].astype(BF16), w_ref[:W_A, :]) + _dot(mb_ref[...].astype(BF16), w_ref[W_A:, :])
    y_ref[...] = y


def _outproj_even(x, ma, mb, w, tm):
    n, d = x.shape
    row = lambda i: (i, 0)
    return pl.pallas_call(
        _outproj_even_kernel,
        grid=(n // tm,),
        in_specs=[pl.BlockSpec((tm, d), row), pl.BlockSpec((tm, W_A), row), pl.BlockSpec((tm, W_B), row),
                  pl.BlockSpec(w.shape, lambda i: (0, 0))],
        out_specs=pl.BlockSpec((tm, d), row),
        out_shape=jax.ShapeDtypeStruct((n, d), F32),
        compiler_params=_params(("parallel",)),
        name="outproj_even",
    )(x, ma, mb, w)


def _outproj_even_inproj_odd_kernel(x_ref, ma_ref, mb_ref, wo_ref, ng_ref, w_ref, cos_ref, sin_ref,
                                    y_ref, q_ref, k_ref, v_ref, g_ref):
    y = (x_ref[...] + _dot(ma_ref[...].astype(BF16), wo_ref[:W_A, :])
         + _dot(mb_ref[...].astype(BF16), wo_ref[W_A:, :]))
    y_ref[...] = y
    _inproj_odd_body(y, ng_ref, w_ref, cos_ref, sin_ref, q_ref, k_ref, v_ref, g_ref)


def _inproj_odd_body(x, ng_ref, w_ref, cos_ref, sin_ref, q_ref, k_ref, v_ref, g_ref):
    h = (_rms(x) * ng_ref[...]).astype(BF16)
    cos = cos_ref[...]
    sin = sin_ref[...]
    half = DK_C // 2

    def rot(a, scale):
        outs = []
        for hd in range(H_C):
            x1 = a[:, hd * DK_C:hd * DK_C + half]
            x2 = a[:, hd * DK_C + half:(hd + 1) * DK_C]
            outs += [(x1 * cos - x2 * sin) * scale, (x2 * cos + x1 * sin) * scale]
        return jnp.concatenate(outs, axis=-1)

    q_ref[...] = rot(_dot(h, w_ref[:, :QK_C]), 1.0).astype(q_ref.dtype)
    k_ref[...] = rot(_dot(h, w_ref[:, QK_C:2 * QK_C]), DK_C ** -0.5)
    for j in range(2):
        lo = 2 * QK_C + j * QK_C
        v_ref[:, j * QK_C:(j + 1) * QK_C] = _dot(h, w_ref[:, lo:lo + QK_C]).astype(v_ref.dtype)
    for j in range(2):
        lo = 2 * QK_C + W_C + j * QK_C
        g_ref[:, j * QK_C:(j + 1) * QK_C] = _silu(_dot(h, w_ref[:, lo:lo + QK_C]))


def _outproj_even_inproj_odd(x, ma, mb, wo, ng, w, cos, sin, act_dtype, tm):
    n, d = x.shape
    npos = cos.shape[0] // tm
    row = lambda i: (i, 0)
    const = lambda i: (0, 0)
    pos = lambda i: (i % npos, 0)
    return pl.pallas_call(
        _outproj_even_inproj_odd_kernel,
        grid=(n // tm,),
        in_specs=[pl.BlockSpec((tm, d), row), pl.BlockSpec((tm, W_A), row), pl.BlockSpec((tm, W_B), row),
                  pl.BlockSpec(wo.shape, const, pipeline_mode=pl.Buffered(1)), pl.BlockSpec((1, d), const),
                  pl.BlockSpec(w.shape, const, pipeline_mode=pl.Buffered(1)),
                  pl.BlockSpec((tm, DK_C // 2), pos), pl.BlockSpec((tm, DK_C // 2), pos)],
        out_specs=[pl.BlockSpec((tm, d), row), pl.BlockSpec((tm, QK_C), row), pl.BlockSpec((tm, QK_C), row),
                   pl.BlockSpec((tm, W_C), row), pl.BlockSpec((tm, W_C), row)],
        out_shape=[jax.ShapeDtypeStruct((n, d), F32),
                   jax.ShapeDtypeStruct((n, QK_C), act_dtype), jax.ShapeDtypeStruct((n, QK_C), F32),
                   jax.ShapeDtypeStruct((n, W_C), act_dtype), jax.ShapeDtypeStruct((n, W_C), F32)],
        compiler_params=_params(("parallel",)),
        name="outproj_even_inproj_odd",
    )(x, ma, mb, wo, ng, w, cos, sin)


def _ret_kernel(q_ref, k_ref, v_ref, g_ref, s0_ref, idec_ref, qdec_ref, kdec_ref, cdec_ref,
                o_ref, sout_ref, s_ref, *, c, n_chunks, n_valid, nb, hpg):
    t = pl.program_id(2)
    chains = [(bi, hh) for bi in range(nb) for hh in range(hpg)]

    @pl.when(t == 0)
    def _():
        s_ref[...] = s0_ref[...]

    def load(ref, bi, hh, width, ci):
        cols = slice(hh * width, (hh + 1) * width)
        if n_valid < c:
            x = ref[bi, :, cols].astype(F32)
            return jnp.concatenate([x, jnp.zeros((c - x.shape[0], x.shape[1]), F32)], axis=0)
        return ref[bi, pl.ds(pl.multiple_of(ci * c, c), c), cols]

    def chunk(ci, _):
        for bi, hh in chains:
            qc = load(q_ref, bi, hh, DK_C, ci).astype(BF16)
            kc = load(k_ref, bi, hh, DK_C, ci)
            vc = load(v_ref, bi, hh, DV_C, ci).astype(BF16)
            s = s_ref[bi, hh]
            scores = _dot_nt(qc, kc.astype(BF16)) * idec_ref[hh]
            o = _dot(scores.astype(BF16), vc) + _dot(qc, s.astype(BF16)) * qdec_ref[hh, :, 0:1]
            s_ref[bi, hh] = cdec_ref[hh, 0:1, 0:1] * s + _dot_tn((kc * kdec_ref[hh, :, 0:1]).astype(BF16), vc)
            ob = _rms(o) * load(g_ref, bi, hh, DV_C, ci)
            cols = slice(hh * DV_C, (hh + 1) * DV_C)
            if n_valid < c:
                o_ref[bi, :, cols] = ob[:n_valid].astype(o_ref.dtype)
            else:
                o_ref[bi, pl.ds(pl.multiple_of(ci * c, c), c), cols] = ob.astype(o_ref.dtype)
        return 0

    lax.fori_loop(0, n_chunks, chunk, 0)

    @pl.when(t == pl.num_programs(2) - 1)
    def _():
        sout_ref[...] = s_ref[...]


def _ret_tables(c, chunk):
    f32 = jnp.float32
    log_gamma = jnp.log1p(-jnp.exp2(-5.0 - jnp.arange(H_C, dtype=f32)))
    idx = jnp.arange(c, dtype=f32)
    real = idx < chunk
    rel = idx[:, None] - idx[None, :]
    ok = (rel >= 0) & real[:, None] & real[None, :]
    idec = jnp.exp(jnp.where(ok[None], rel[None] * log_gamma[:, None, None], -jnp.inf))
    qdec = jnp.where(real[None, :], jnp.exp((idx[None, :] + 1.0) * log_gamma[:, None]), 0.0)
    kdec = jnp.where(real[None, :], jnp.exp((chunk - 1.0 - idx[None, :]) * log_gamma[:, None]), 0.0)
    cdec = jnp.exp(chunk * log_gamma)
    bc = lambda x: jnp.broadcast_to(x[:, :, None], (H_C, c, LANES))
    return idec, bc(qdec), bc(kdec), jnp.broadcast_to(cdec[:, None, None], (H_C, 8, LANES))


def _ret(q, k, v, g, s0, batch, seq, out_dtype, tb, nb, hpg):
    n = q.shape[0]
    if seq >= RET_CHUNK:
        c = RET_CHUNK
        n_valid, rows_blk, nt, n_chunks = c, tb, seq // tb, tb // c
    else:
        c = RET_PAD_CHUNK
        n_valid, rows_blk, nt, n_chunks = seq, seq, 1, 1
    idec, qdec, kdec, cdec = _ret_tables(c, n_valid)
    tok = lambda hg, b, t: (b, t, hg)
    st = lambda hg, b, t: (b, hg, 0, 0)
    hd = lambda hg, b, t: (hg, 0, 0)
    qk_blk = pl.BlockSpec((nb, rows_blk, hpg * DK_C), tok)
    vg_blk = pl.BlockSpec((nb, rows_blk, hpg * DV_C), tok)
    st_blk = pl.BlockSpec((nb, hpg, DK_C, DV_C), st)
    r3 = lambda x: x.reshape(batch, seq, x.shape[-1])
    mix, s_out = pl.pallas_call(
        functools.partial(_ret_kernel, c=c, n_chunks=n_chunks, n_valid=n_valid, nb=nb, hpg=hpg),
        grid=(H_C // hpg, batch // nb, nt),
        in_specs=[qk_blk, qk_blk, vg_blk, vg_blk, st_blk,
                  pl.BlockSpec((hpg, c, c), hd), pl.BlockSpec((hpg, c, LANES), hd),
                  pl.BlockSpec((hpg, c, LANES), hd), pl.BlockSpec((hpg, 8, LANES), hd)],
        out_specs=[vg_blk, st_blk],
        out_shape=[jax.ShapeDtypeStruct((batch, seq, W_C), out_dtype),
                   jax.ShapeDtypeStruct((batch, H_C, DK_C, DV_C), F32)],
        scratch_shapes=[pltpu.VMEM((nb, hpg, DK_C, DV_C), F32)],
        compiler_params=_params(("parallel", "parallel", "arbitrary")),
        name="retention",
    )(r3(q), r3(k), r3(v), r3(g), s0, idec, qdec, kdec, cdec)
    return mix.reshape(n, W_C), s_out


def _outproj_odd_kernel(x_ref, m_ref, w_ref, y_ref):
    y_ref[...] = x_ref[...] + _dot(m_ref[...].astype(BF16), w_ref[...].astype(BF16))


def _outproj_odd(x, m, w, tm):
    n, d = x.shape
    row = lambda i: (i, 0)
    return pl.pallas_call(
        _outproj_odd_kernel,
        grid=(n // tm,),
        in_specs=[pl.BlockSpec((tm, d), row), pl.BlockSpec((tm, W_C), row),
                  pl.BlockSpec(w.shape, lambda i: (0, 0), pipeline_mode=pl.Buffered(1))],
        out_specs=pl.BlockSpec((tm, d), row),
        out_shape=jax.ShapeDtypeStruct((n, d), F32),
        compiler_params=_params(("parallel",)),
        name="outproj_odd",
    )(x, m, w)


def _rope_tables(pos):
    half = DK_C // 2
    inv = 1.0 / (ROPE_BASE ** jnp.linspace(0.0, 1.0, half, dtype=F32))
    ang = pos[:, None] * inv[None, :]
    return jnp.cos(ang), jnp.sin(ang)


def _ntri(n):
    return jnp.asarray(-np.tril(np.ones((n, n), np.float32), -1), BF16)


def kernel(x_prompt, x_sample, cache_k, cache_v, page_table, state_hgrn, state_ret, norm_g,
           w_in_even, w_out_even, q_norm_g, k_norm_g, sb_logit_bias, hgrn_lower_bounds,
           hgrn_out_norm_g, w_in_odd, w_out_odd):
    bp, tp, d = x_prompt.shape
    bs, ts, _ = x_sample.shape
    n_pool, page_size = cache_k.shape[1], cache_k.shape[2]
    past_len = page_table.shape[1] * page_size
    depth = norm_g.shape[0]
    tm_p = 512
    tm_o = 1024
    tm_s = bs * ts

    yp = x_prompt.reshape(bp * tp, d)
    ys = x_sample.reshape(bs * ts, d)
    grp = jnp.asarray(np.kron(np.eye(H_A, dtype=np.float32), np.ones((HD_A, HD_A), np.float32)), BF16)
    cos_p, sin_p = _rope_tables(jnp.arange(tp, dtype=F32))
    cos_s, sin_s = _rope_tables(past_len + jnp.arange(ts, dtype=F32))
    cos_s, sin_s = jnp.tile(cos_s, (bs, 1)), jnp.tile(sin_s, (bs, 1))

    k_p, v_p, k_s, v_s, hg_p, hg_s, rt_p, rt_s = [], [], [], [], [], [], [], []
    for layer in range(depth):
        e = layer // 2
        ng = norm_g[layer].reshape(1, d)
        if layer % 2 == 0:
            w_in = w_in_even[e]
            w_out = w_out_even[e].astype(BF16)
            qg = jnp.tile(q_norm_g[e], H_A).reshape(1, W_A)
            kg = jnp.tile(k_norm_g[e], H_A).reshape(1, W_A)
            og = hgrn_out_norm_g[e].reshape(1, DV_B)
            bias = sb_logit_bias[e].astype(F32)
            q, k, v, k32, v32, ga, qb, fb, ib, gb = _inproj_even(yp, ng, w_in, qg, kg, grp, BF16, tm_p, kv_seq=tp)
            ma = _sb_prompt(q, k, v, ga, bias, _ntri(SB_TQ), bp, tp, SB_TQ)
            mb, hs = _hgrn(qb, fb, ib, gb, jnp.zeros((bp, H_B, DK_B, DV_B), F32), hgrn_lower_bounds,
                           og, bp, tp, e, BF16, 512, HGRN_NB)
            fuse_next = layer + 1 < depth
            if fuse_next:
                ng_next = norm_g[layer + 1].reshape(1, d)
                w_in_next = w_in_odd[e].astype(BF16)
                yp, *odd_in_p = _outproj_even_inproj_odd(yp, ma, mb, w_out, ng_next, w_in_next, cos_p, sin_p, BF16, tm_p)
            else:
                yp = _outproj_even(yp, ma, mb, w_out, tm_o)
            k_p.append(jnp.transpose(k32.reshape(bp, H_A, HD_A, tp), (0, 3, 1, 2)))
            v_p.append(jnp.transpose(v32.reshape(bp, H_A, HD_A, tp), (0, 3, 1, 2)))
            hg_p.append(hs)
            q, k, v, k32, v32, ga, qb, fb, ib, gb = _inproj_even(ys, ng, w_in, qg, kg, grp, F32, tm_s)
            ma = _sb_sample(q, k, v, ga, jnp.repeat(bias * LOG2E, ts).reshape(H_A * ts, 1), _ntri(page_size),
                            jnp.transpose(cache_k[e], (0, 2, 3, 1)), jnp.transpose(cache_v[e], (0, 2, 3, 1)),
                            page_table, ts, SB_PAGES_PER_STEP)
            mb, hs = _hgrn(qb, fb, ib, gb, state_hgrn[e].astype(F32), hgrn_lower_bounds, og, bs, ts, e, F32, ts,
                           HGRN_NB)
            if fuse_next:
                ys, *odd_in_s = _outproj_even_inproj_odd(ys, ma, mb, w_out, ng_next, w_in_next, cos_s, sin_s, F32, tm_s)
            else:
                ys = _outproj_even(ys, ma, mb, w_out, tm_s)
            k_s.append(k32.reshape(bs, ts, H_A, HD_A))
            v_s.append(v32.reshape(bs, ts, H_A, HD_A))
            hg_s.append(hs)
        else:
            w_out = w_out_odd[e]
            q, k, v, g = odd_in_p
            m, s = _ret(q, k, v, g, jnp.zeros((bp, H_C, DK_C, DV_C), F32), bp, tp, BF16, 512, RET_NB, RET_HPG)
            yp = _outproj_odd(yp, m, w_out, tm_o)
            rt_p.append(s)
            q, k, v, g = odd_in_s
            m, s = _ret(q, k, v, g, state_ret[e].astype(F32), bs, ts, F32, ts, RET_NB, RET_HPG)
            ys = _outproj_odd(ys, m, w_out, tm_s)
            rt_s.append(s)
    return (yp.reshape(bp, tp, d), ys.reshape(bs, ts, d), jnp.stack(k_p), jnp.stack(v_p), jnp.stack(k_s),
            jnp.stack(v_s), jnp.stack(hg_p), jnp.stack(hg_s), jnp.stack(rt_p), jnp.stack(rt_s))
```
